```python
import math
import jax, jax.numpy as jnp
from jax import lax
import numpy as np

D_MODEL = 2048
BATCH = 1
SEQ = 8192
DEPTH = 4

CHUNK = 64
N_MIXERS = 3
N_A = (DEPTH + 2) // 3
N_B = (DEPTH + 1) // 3
N_C = DEPTH // 3
CONV_WIDTH = 31
POOL_WINDOWS = (2, 4, 8, 16)
N_POOL_GROUPS = len(POOL_WINDOWS)
POOL_GC = D_MODEL // N_POOL_GROUPS
HEAD_DIM = 64
N_HEADS = D_MODEL // HEAD_DIM
N_KV = 8
GROUP = N_HEADS // N_KV
WINDOW = 128
WINDOW_CHUNKS = WINDOW // CHUNK
QBLOCK = 128
NUM_BUCKETS = 32
REL_MAX_DIST = 128
D_FF = ((8 * D_MODEL // 3 + 255) // 256) * 256
PLE_DIM = 256
EPS = 1e-6
NEG_INF = -1e30

kernel_name = "hybrid_conv_pool_swa_trunk"


def rms_norm(x, g):
    xf = x.astype(jnp.float32)
    y = xf * lax.rsqrt(jnp.mean(xf * xf, axis=-1, keepdims=True) + EPS)
    return (y * g.astype(jnp.float32)).astype(x.dtype)


def layer_norm(x, g, b):
    xf = x.astype(jnp.float32)
    mu = jnp.mean(xf, axis=-1, keepdims=True)
    xc = xf - mu
    y = xc * lax.rsqrt(jnp.mean(xc * xc, axis=-1, keepdims=True) + EPS)
    return (y * g.astype(jnp.float32) + b.astype(jnp.float32)).astype(x.dtype)


def conformer_conv(h, w_in, b_in, w_dw, b_dw, ln_g, ln_b, w_out, b_out):
    u = h @ w_in + b_in
    a, gate = jnp.split(u, 2, axis=-1)
    u = a * jax.nn.sigmoid(gate)
    u = lax.conv_general_dilated(
        u, w_dw[:, None, :], window_strides=(1,), padding=[(CONV_WIDTH - 1, 0)],
        dimension_numbers=("NWC", "WIO", "NWC"), feature_group_count=D_MODEL) + b_dw
    u = jax.nn.silu(layer_norm(u, ln_g, ln_b))
    return u @ w_out + b_out


def multiscale_pool(h, w_grp, scale):
    B, S, D = h.shape
    hf = h.astype(jnp.float32)
    cs = jnp.concatenate([jnp.zeros((B, 1, D), jnp.float32), jnp.cumsum(hf, axis=1)], axis=1)
    t = jnp.arange(S)
    pooled = []
    for g, w in enumerate(POOL_WINDOWS):
        sl = slice(g * POOL_GC, (g + 1) * POOL_GC)
        start = jnp.maximum(t + 1 - w, 0)
        s = cs[:, 1:, sl] - cs[:, start, sl]
        cnt = (t + 1 - start).astype(jnp.float32)[None, :, None]
        pooled.append(s / cnt)
    pooled = jnp.stack(pooled, axis=2)
    mix = (pooled - hf.reshape(B, S, N_POOL_GROUPS, POOL_GC)).astype(h.dtype)
    y = jnp.einsum('bsgc,gcd->bsgd', mix, w_grp).reshape(B, S, D)
    return y * scale


def t5_bucket(rel):
    nb = NUM_BUCKETS // 2
    n = -rel
    ret = jnp.where(n < 0, nb, 0)
    n = jnp.abs(n)
    max_exact = nb // 2
    nf = jnp.maximum(n, 1).astype(jnp.float32)
    large = max_exact + (jnp.log(nf / max_exact) / math.log(REL_MAX_DIST / max_exact)
                         * (nb - max_exact)).astype(jnp.int32)
    large = jnp.minimum(large, nb - 1)
    return ret + jnp.where(n < max_exact, n, large)


def band_bias_and_mask(rel_bias, n_blocks):
    i = jnp.arange(QBLOCK)[:, None]
    j = jnp.arange(2 * QBLOCK)[None, :]
    rel = j - QBLOCK - i
    bias = rel_bias[t5_bucket(rel)]
    bias = jnp.transpose(bias, (2, 0, 1)).reshape(N_KV, GROUP, QBLOCK, 2 * QBLOCK)
    qc = i // CHUNK
    kc = jnp.floor_divide(j - QBLOCK, CHUNK)
    chunk_ok = (kc <= qc) & (kc >= qc - WINDOW_CHUNKS)
    blk = jnp.arange(n_blocks)[:, None, None]
    mask = chunk_ok[None] & ((blk > 0) | (j[None] >= QBLOCK))
    return bias, mask


def swa_sink_attention(h, w_qkv, q_g, k_g, sinks, w_o, rel_bias):
    B, S, _ = h.shape
    NB = S // QBLOCK
    qkv = h @ w_qkv
    q, k, v = jnp.split(qkv, [N_HEADS * HEAD_DIM, (N_HEADS + N_KV) * HEAD_DIM], axis=-1)
    q = rms_norm(q.reshape(B, S, N_KV, GROUP, HEAD_DIM), q_g)
    k = rms_norm(k.reshape(B, S, N_KV, HEAD_DIM), k_g)
    v = v.reshape(B, S, N_KV, HEAD_DIM)
    q = q.reshape(B, NB, QBLOCK, N_KV, GROUP, HEAD_DIM)

    def band(t):
        tb = t.reshape(B, NB, QBLOCK, N_KV, HEAD_DIM)
        prev = jnp.concatenate([jnp.zeros_like(tb[:, :1]), tb[:, :-1]], axis=1)
        return jnp.concatenate([prev, tb], axis=2)

    kb, vb = band(k), band(v)
    bias, mask = band_bias_and_mask(rel_bias, NB)
    logits = jnp.einsum('bnqhgd,bnkhd->bnhgqk', q, kb,
                        preferred_element_type=jnp.float32) * (HEAD_DIM ** -0.5)
    logits = logits + bias.astype(jnp.float32)
    logits = jnp.where(mask[None, :, None, None], logits, NEG_INF)
    sink = sinks.astype(jnp.float32).reshape(1, 1, N_KV, GROUP, 1, 1)
    m = jnp.maximum(jnp.max(logits, axis=-1, keepdims=True), sink)
    e = jnp.exp(logits - m)
    denom = jnp.sum(e, axis=-1, keepdims=True) + jnp.exp(sink - m)
    probs = (e / denom).astype(v.dtype)
    o = jnp.einsum('bnhgqk,bnkhd->bnqhgd', probs, vb).reshape(B, S, N_HEADS * HEAD_DIM)
    return o @ w_o


def setup_inputs(seed: int = 0) -> dict:
    key = jax.random.key(seed)
    ks = jax.random.split(key, 32)
    f32 = jnp.float32
    nrm = lambda k, shape, scale: jax.random.normal(k, shape, f32) * scale
    gain = lambda k, shape: 1.0 + 0.02 * jax.random.normal(k, shape, f32)
    D = D_MODEL
    return {
        "x": nrm(ks[0], (BATCH, SEQ, D), 1.0),
        "p": nrm(ks[1], (DEPTH, BATCH, SEQ, PLE_DIM), 1.0),
        "norm_mix": gain(ks[2], (DEPTH, D)),
        "norm_ffn": gain(ks[3], (DEPTH, D)),
        "norm_ple": gain(ks[4], (DEPTH, D)),
        "conv_w_in": nrm(ks[5], (N_A, D, 2 * D), D ** -0.5),
        "conv_b_in": nrm(ks[6], (N_A, 2 * D), 0.02),
        "conv_w_dw": nrm(ks[7], (N_A, CONV_WIDTH, D), CONV_WIDTH ** -0.5),
        "conv_b_dw": nrm(ks[8], (N_A, D), 0.02),
        "conv_ln_g": gain(ks[9], (N_A, D)),
        "conv_ln_b": nrm(ks[10], (N_A, D), 0.02),
        "conv_w_out": nrm(ks[11], (N_A, D, D), D ** -0.5),
        "conv_b_out": nrm(ks[12], (N_A, D), 0.02),
        "pool_w": nrm(ks[13], (N_B, N_POOL_GROUPS, POOL_GC, POOL_GC), POOL_GC ** -0.5),
        "pool_scale": 0.5 + 0.05 * jax.random.normal(ks[14], (N_B, D), f32),
        "attn_w_qkv": nrm(ks[15], (N_C, D, (N_HEADS + 2 * N_KV) * HEAD_DIM), D ** -0.5),
        "attn_q_norm": gain(ks[16], (N_C, HEAD_DIM)),
        "attn_k_norm": gain(ks[17], (N_C, HEAD_DIM)),
        "attn_sinks": nrm(ks[18], (N_C, N_HEADS), 0.5),
        "attn_w_o": nrm(ks[19], (N_C, N_HEADS * HEAD_DIM, D), (N_HEADS * HEAD_DIM) ** -0.5),
        "rel_bias": nrm(ks[20], (NUM_BUCKETS, N_HEADS), 0.5),
        "ffn_w_gate": nrm(ks[21], (DEPTH, D, D_FF), D ** -0.5),
        "ffn_w_up": nrm(ks[22], (DEPTH, D, D_FF), D ** -0.5),
        "ffn_w_down": nrm(ks[23], (DEPTH, D_FF, D), D_FF ** -0.5),
        "ple_w_proj": nrm(ks[24], (DEPTH, PLE_DIM, D), PLE_DIM ** -0.5),
        "ple_w_gate": nrm(ks[25], (DEPTH, D, D), D ** -0.5),
        "ple_b_gate": nrm(ks[26], (DEPTH, D), 0.02),
    }


def reference(x, p, norm_mix, norm_ffn, norm_ple,
              conv_w_in, conv_b_in, conv_w_dw, conv_b_dw, conv_ln_g, conv_ln_b, conv_w_out, conv_b_out,
              pool_w, pool_scale,
              attn_w_qkv, attn_q_norm, attn_k_norm, attn_sinks, attn_w_o, rel_bias,
              ffn_w_gate, ffn_w_up, ffn_w_down,
              ple_w_proj, ple_w_gate, ple_b_gate):
    for i in range(DEPTH):
        kind, j = i % N_MIXERS, i // N_MIXERS
        h = rms_norm(x, norm_mix[i])
        if kind == 0:
            y = conformer_conv(h, conv_w_in[j], conv_b_in[j], conv_w_dw[j], conv_b_dw[j],
                               conv_ln_g[j], conv_ln_b[j], conv_w_out[j], conv_b_out[j])
        elif kind == 1:
            y = multiscale_pool(h, pool_w[j], pool_scale[j])
        else:
            y = swa_sink_attention(h, attn_w_qkv[j], attn_q_norm[j], attn_k_norm[j],
                                   attn_sinks[j], attn_w_o[j], rel_bias)
        x = x + y
        h = rms_norm(x, norm_ffn[i])
        x = x + (jax.nn.silu(h @ ffn_w_gate[i]) * (h @ ffn_w_up[i])) @ ffn_w_down[i]
        g = jax.nn.sigmoid(rms_norm(x, norm_ple[i]) @ ple_w_gate[i] + ple_b_gate[i])
        x = x + g * (p[i] @ ple_w_proj[i])
    return x
```

```python
import functools
import math

import jax
import jax.numpy as jnp
from jax import lax
from jax.experimental import pallas as pl
from jax.experimental.pallas import tpu as pltpu

N_MIXERS = 3
CHUNK = 64
CONV_WIDTH = 31
POOL_WINDOWS = (2, 4, 8, 16)
HEAD_DIM = 64
WINDOW_CHUNKS = 2
QBLOCK = 128
NUM_BUCKETS = 32
REL_MAX_DIST = 128
EPS = 1e-6
NEG_INF = -1e30

V7X_VMEM_BYTES = 64 * 1024 * 1024
V7X_LANES = 128
V7X_SUBLANES = 8
VMEM_LIMIT_BYTES = V7X_VMEM_BYTES * 7 // 8

CONV_HALO = 32
POOL_HALO = 16
CONV_ROWS = 16
CONV_STRIP = 512

BF16 = jnp.bfloat16
F32 = jnp.float32


def _params(*semantics):
    return pltpu.CompilerParams(dimension_semantics=semantics, vmem_limit_bytes=VMEM_LIMIT_BYTES)


def _rms_norm(x, g):
    ms = jnp.mean(x * x, axis=-1, keepdims=True)
    return x * lax.rsqrt(ms + EPS) * g


def _dot(a, b):
    return jnp.dot(a, b, preferred_element_type=F32)


def _conv_in_kernel(x_ref, g_ref, wa_ref, wg_ref, ba_ref, bg_ref, u_ref, h_ref):
    @pl.when(pl.program_id(1) == 0)
    def _():
        h_ref[...] = _rms_norm(x_ref[...], g_ref[...]).astype(BF16)

    h = h_ref[...]
    a = _dot(h, wa_ref[...]) + ba_ref[...]
    gate = _dot(h, wg_ref[...]) + bg_ref[...]
    u_ref[...] = a * jax.nn.sigmoid(gate)


def _conv_in(x, gain, w_in, b_in, *, tm, tn):
    m, d = x.shape
    nj = d // tn
    return pl.pallas_call(
        _conv_in_kernel,
        grid=(m // tm, nj),
        in_specs=[
            pl.BlockSpec((tm, d), lambda i, j: (i, 0)),
            pl.BlockSpec((1, d), lambda i, j: (0, 0)),
            pl.BlockSpec((d, tn), lambda i, j: (0, j)),
            pl.BlockSpec((d, tn), lambda i, j: (0, j + nj)),
            pl.BlockSpec((1, tn), lambda i, j: (0, j)),
            pl.BlockSpec((1, tn), lambda i, j: (0, j + nj)),
        ],
        out_specs=pl.BlockSpec((tm, tn), lambda i, j: (i, j)),
        out_shape=jax.ShapeDtypeStruct((m, d), F32),
        scratch_shapes=[pltpu.VMEM((tm, d), BF16)],
        compiler_params=_params("parallel", "arbitrary"),
        name="conv_in",
    )(x, gain, w_in, w_in, b_in, b_in)


def _conv_out_kernel(u_ref, uprev_ref, wdw_ref, bdw_ref, lng_ref, lnb_ref, wout_ref, bout_ref, x_ref,
                     o_ref, ext_ref, conv_ref, v_ref, *, tm, tn):
    i = pl.program_id(0)
    j = pl.program_id(1)
    d = u_ref.shape[1]

    @pl.when(j == 0)
    def _():
        ext_ref[0:CONV_HALO, :] = jnp.where(i == 0, 0.0, uprev_ref[...])
        ext_ref[CONV_HALO:, :] = u_ref[...]

        def rows(c, carry):
            r0 = pl.multiple_of(c * CONV_ROWS, CONV_ROWS)
            for s in range(d // CONV_STRIP):
                cols = slice(s * CONV_STRIP, (s + 1) * CONV_STRIP)
                win = ext_ref[pl.ds(r0, CONV_ROWS + CONV_HALO), cols]
                acc = jnp.broadcast_to(bdw_ref[:, cols], (CONV_ROWS, CONV_STRIP))
                for k in range(CONV_WIDTH):
                    off = CONV_HALO - (CONV_WIDTH - 1) + k
                    acc = acc + wdw_ref[k:k + 1, cols] * win[off:off + CONV_ROWS, :]
                conv_ref[:, cols] = acc
            c_all = conv_ref[...]
            mu = jnp.mean(c_all, axis=-1, keepdims=True)
            xc = c_all - mu
            y = xc * lax.rsqrt(jnp.mean(xc * xc, axis=-1, keepdims=True) + EPS)
            y = y * lng_ref[...] + lnb_ref[...]
            v_ref[pl.ds(r0, CONV_ROWS), :] = (y * jax.nn.sigmoid(y)).astype(BF16)
            return carry

        lax.fori_loop(0, tm // CONV_ROWS, rows, 0)

    o_ref[...] = x_ref[...] + _dot(v_ref[...], wout_ref[...]) + bout_ref[...]


def _conv_out(u, x, w_dw, b_dw, ln_g, ln_b, w_out, b_out, *, tm, tn):
    m, d = u.shape
    hb = tm // CONV_HALO
    kernel = functools.partial(_conv_out_kernel, tm=tm, tn=tn)
    return pl.pallas_call(
        kernel,
        grid=(m // tm, d // tn),
        in_specs=[
            pl.BlockSpec((tm, d), lambda i, j: (i, 0)),
            pl.BlockSpec((CONV_HALO, d), lambda i, j: (jnp.maximum(i * hb - 1, 0), 0)),
            pl.BlockSpec((CONV_WIDTH, d), lambda i, j: (0, 0)),
            pl.BlockSpec((1, d), lambda i, j: (0, 0)),
            pl.BlockSpec((1, d), lambda i, j: (0, 0)),
            pl.BlockSpec((1, d), lambda i, j: (0, 0)),
            pl.BlockSpec((d, tn), lambda i, j: (0, j)),
            pl.BlockSpec((1, tn), lambda i, j: (0, j)),
            pl.BlockSpec((tm, tn), lambda i, j: (i, j)),
        ],
        out_specs=pl.BlockSpec((tm, tn), lambda i, j: (i, j)),
        out_shape=jax.ShapeDtypeStruct((m, d), F32),
        scratch_shapes=[
            pltpu.VMEM((tm + CONV_HALO, d), F32),
            pltpu.VMEM((CONV_ROWS, d), F32),
            pltpu.VMEM((tm, d), BF16),
        ],
        compiler_params=_params("parallel", "arbitrary"),
        name="conv_out",
    )(u, u, w_dw, b_dw, ln_g, ln_b, w_out, b_out, x)


def _pool_kernel(x_ref, xprev_ref, g_ref, w_ref, scale_ref, o_ref, ext_ref, *, tm):
    i = pl.program_id(0)
    d = x_ref.shape[1]
    gc = d // len(POOL_WINDOWS)
    x = x_ref[...]
    h = _rms_norm(x, g_ref[...])
    ext_ref[0:POOL_HALO, :] = jnp.where(i == 0, 0.0, _rms_norm(xprev_ref[...], g_ref[...]))
    ext_ref[POOL_HALO:, :] = h
    t = i * tm + lax.broadcasted_iota(jnp.int32, (tm, 1), 0)
    for g, w in enumerate(POOL_WINDOWS):
        cols = slice(g * gc, (g + 1) * gc)
        s = h[:, cols]
        for k in range(1, w):
            s = s + ext_ref[POOL_HALO - k:POOL_HALO - k + tm, cols]
        cnt = jnp.minimum(t + 1, w).astype(F32)
        mix = (s / cnt - h[:, cols]).astype(BF16)
        y = _dot(mix, w_ref[g]) * scale_ref[:, cols]
        o_ref[:, cols] = x[:, cols] + y


def _pool(x, gain, w_grp, scale, *, tm):
    m, d = x.shape
    ng, gc, _ = w_grp.shape
    hb = tm // POOL_HALO
    kernel = functools.partial(_pool_kernel, tm=tm)
    return pl.pallas_call(
        kernel,
        grid=(m // tm,),
        in_specs=[
            pl.BlockSpec((tm, d), lambda i: (i, 0)),
            pl.BlockSpec((POOL_HALO, d), lambda i: (jnp.maximum(i * hb - 1, 0), 0)),
            pl.BlockSpec((1, d), lambda i: (0, 0)),
            pl.BlockSpec((ng, gc, gc), lambda i: (0, 0, 0)),
            pl.BlockSpec((1, d), lambda i: (0, 0)),
        ],
        out_specs=pl.BlockSpec((tm, d), lambda i: (i, 0)),
        out_shape=jax.ShapeDtypeStruct((m, d), F32),
        scratch_shapes=[pltpu.VMEM((tm + POOL_HALO, d), F32)],
        compiler_params=_params("parallel"),
        name="pool",
    )(x, x, gain, w_grp, scale)


def _norm_matmul_kernel(x_ref, g_ref, w_ref, o_ref, h_ref):
    @pl.when(pl.program_id(1) == 0)
    def _():
        h_ref[...] = _rms_norm(x_ref[...], g_ref[...]).astype(BF16)

    o_ref[...] = _dot(h_ref[...], w_ref[...])


def _norm_matmul(x, gain, w, *, tm, tn):
    m, d = x.shape
    n = w.shape[1]
    return pl.pallas_call(
        _norm_matmul_kernel,
        grid=(m // tm, n // tn),
        in_specs=[
            pl.BlockSpec((tm, d), lambda i, j: (i, 0)),
            pl.BlockSpec((1, d), lambda i, j: (0, 0)),
            pl.BlockSpec((d, tn), lambda i, j: (0, j)),
        ],
        out_specs=pl.BlockSpec((tm, tn), lambda i, j: (i, j)),
        out_shape=jax.ShapeDtypeStruct((m, n), F32),
        scratch_shapes=[pltpu.VMEM((tm, d), BF16)],
        compiler_params=_params("parallel", "arbitrary"),
        name="qkv_proj",
    )(x, gain, w)


def _t5_bucket(rel):
    nb = NUM_BUCKETS // 2
    n = -rel
    ret = jnp.where(n < 0, nb, 0)
    n = jnp.abs(n)
    max_exact = nb // 2
    nf = jnp.maximum(n, 1).astype(F32)
    large = max_exact + (jnp.log(nf / max_exact) / math.log(REL_MAX_DIST / max_exact)
                         * (nb - max_exact)).astype(jnp.int32)
    large = jnp.minimum(large, nb - 1)
    return ret + jnp.where(n < max_exact, n, large)


def _band_bias_kernel(bucket_ref, rel_bias_ref, o_ref):
    h = pl.program_id(0)
    bucket = bucket_ref[...]
    acc = jnp.zeros(bucket.shape, F32)
    for b in range(NUM_BUCKETS):
        acc = jnp.where(bucket == b, rel_bias_ref[b, h], acc)
    o_ref[0] = acc


def _band_bias(rel_bias):
    n_heads = rel_bias.shape[1]
    i = jnp.arange(QBLOCK)[:, None]
    j = jnp.arange(2 * QBLOCK)[None, :]
    bucket = _t5_bucket(j - QBLOCK - i).astype(jnp.int32)
    return pl.pallas_call(
        _band_bias_kernel,
        grid=(n_heads,),
        in_specs=[
            pl.BlockSpec((QBLOCK, 2 * QBLOCK), lambda h: (0, 0)),
            pl.BlockSpec(memory_space=pltpu.SMEM),
        ],
        out_specs=pl.BlockSpec((1, QBLOCK, 2 * QBLOCK), lambda h: (h, 0, 0)),
        out_shape=jax.ShapeDtypeStruct((n_heads, QBLOCK, 2 * QBLOCK), F32),
        compiler_params=_params("parallel"),
        name="band_bias",
    )(bucket, rel_bias)


def _half_rms_norm(x, g, low):
    sq = x * x
    ms_lo = jnp.sum(jnp.where(low, sq, 0.0), axis=-1, keepdims=True) * (1.0 / HEAD_DIM)
    ms_hi = jnp.sum(jnp.where(low, 0.0, sq), axis=-1, keepdims=True) * (1.0 / HEAD_DIM)
    r = jnp.where(low, lax.rsqrt(ms_lo + EPS), lax.rsqrt(ms_hi + EPS))
    return x * r * g


def _attn_kernel(sinks_ref, q_ref, kc_ref, kp_ref, vc_ref, vp_ref, bias_ref, qg_ref, kg_ref, o_ref,
                 *, n_kv, group):
    n = pl.program_id(0)
    pair = 2 * HEAD_DIM
    qi = lax.broadcasted_iota(jnp.int32, (QBLOCK, 2 * QBLOCK), 0)
    kj = lax.broadcasted_iota(jnp.int32, (QBLOCK, 2 * QBLOCK), 1)
    qc = qi // CHUNK
    kc = kj // CHUNK - QBLOCK // CHUNK
    visible = (kc <= qc) & (kc >= qc - WINDOW_CHUNKS) & ((n > 0) | (kj >= QBLOCK))
    low = lax.broadcasted_iota(jnp.int32, (1, pair), 1) < HEAD_DIM
    scale = HEAD_DIM ** -0.5

    for hp in range(n_kv // 2):
        kcols = slice(hp * pair, (hp + 1) * pair)
        k2 = jnp.concatenate([kp_ref[:, kcols], kc_ref[:, kcols]], axis=0)
        k2 = _half_rms_norm(k2, kg_ref[...], low).astype(BF16)
        v2 = jnp.concatenate([vp_ref[:, kcols], vc_ref[:, kcols]], axis=0).astype(BF16)
        for half in range(2):
            kv_head = 2 * hp + half
            for qp in range(group // 2):
                col = kv_head * group * HEAD_DIM + qp * pair
                q2 = _half_rms_norm(q_ref[:, col:col + pair], qg_ref[...], low) * scale
                outs = []
                for qhalf in range(2):
                    head = kv_head * group + 2 * qp + qhalf
                    qz = jnp.where(low == (qhalf == 0), q2, 0.0)
                    if qhalf != half:
                        qz = pltpu.roll(qz, HEAD_DIM, axis=1)
                    s = lax.dot_general(qz.astype(BF16), k2, (((1,), (1,)), ((), ())),
                                        preferred_element_type=F32)
                    s = jnp.where(visible, s + bias_ref[head], NEG_INF)
                    sink = sinks_ref[head]
                    mx = jnp.maximum(jnp.max(s, axis=-1, keepdims=True), sink)
                    e = jnp.exp(s - mx)
                    denom = jnp.sum(e, axis=-1, keepdims=True) + jnp.exp(sink - mx)
                    p = (e * (1.0 / denom)).astype(BF16)
                    o2 = _dot(p, v2)
                    if qhalf != half:
                        o2 = pltpu.roll(o2, HEAD_DIM, axis=1)
                    outs.append(o2)
                o_ref[:, col:col + pair] = jnp.where(low, outs[0], outs[1]).astype(BF16)


def _attention(qkv, bias, sinks, q_gain, k_gain, *, n_heads, n_kv):
    m = qkv.shape[0]
    group = n_heads // n_kv
    dq = n_heads * HEAD_DIM
    dkv = n_kv * HEAD_DIM
    kblk = dq // dkv
    kernel = functools.partial(_attn_kernel, n_kv=n_kv, group=group)
    prev = lambda n: jnp.maximum(n - 1, 0)
    return pl.pallas_call(
        kernel,
        grid=(m // QBLOCK,),
        in_specs=[
            pl.BlockSpec(memory_space=pltpu.SMEM),
            pl.BlockSpec((QBLOCK, dq), lambda n: (n, 0)),
            pl.BlockSpec((QBLOCK, dkv), lambda n: (n, kblk)),
            pl.BlockSpec((QBLOCK, dkv), lambda n: (prev(n), kblk)),
            pl.BlockSpec((QBLOCK, dkv), lambda n: (n, kblk + 1)),
            pl.BlockSpec((QBLOCK, dkv), lambda n: (prev(n), kblk + 1)),
            pl.BlockSpec((n_heads, QBLOCK, 2 * QBLOCK), lambda n: (0, 0, 0)),
            pl.BlockSpec((1, 2 * HEAD_DIM), lambda n: (0, 0)),
            pl.BlockSpec((1, 2 * HEAD_DIM), lambda n: (0, 0)),
        ],
        out_specs=pl.BlockSpec((QBLOCK, dq), lambda n: (n, 0)),
        out_shape=jax.ShapeDtypeStruct((m, dq), BF16),
        compiler_params=_params("parallel"),
        name="band_attention",
    )(sinks, qkv, qkv, qkv, qkv, qkv, bias, q_gain, k_gain)


def _matmul_residual_kernel(a_ref, w_ref, x_ref, o_ref):
    o_ref[...] = x_ref[...] + _dot(a_ref[...], w_ref[...])


def _matmul_residual(a, w, x, *, tm, tn):
    m, k = a.shape
    n = w.shape[1]
    return pl.pallas_call(
        _matmul_residual_kernel,
        grid=(m // tm, n // tn),
        in_specs=[
            pl.BlockSpec((tm, k), lambda i, j: (i, 0)),
            pl.BlockSpec((k, tn), lambda i, j: (0, j)),
            pl.BlockSpec((tm, tn), lambda i, j: (i, j)),
        ],
        out_specs=pl.BlockSpec((tm, tn), lambda i, j: (i, j)),
        out_shape=jax.ShapeDtypeStruct((m, n), F32),
        compiler_params=_params("parallel", "arbitrary"),
        name="attn_out_proj",
    )(a, w, x)


def _ffn_kernel(x_ref, g_ref, wg_ref, wu_ref, wd_ref, o_ref, h_ref):
    @pl.when(pl.program_id(1) == 0)
    def _():
        x = x_ref[...]
        h_ref[...] = _rms_norm(x, g_ref[...]).astype(BF16)
        o_ref[...] = x

    h = h_ref[...]
    gate = _dot(h, wg_ref[...])
    up = _dot(h, wu_ref[...])
    act = (gate * jax.nn.sigmoid(gate) * up).astype(BF16)
    o_ref[...] += _dot(act, wd_ref[...])


def _ffn(x, gain, w_gate, w_up, w_down, *, tm, tf):
    m, d = x.shape
    f = w_gate.shape[1]
    return pl.pallas_call(
        _ffn_kernel,
        grid=(m // tm, f // tf),
        in_specs=[
            pl.BlockSpec((tm, d), lambda i, j: (i, 0)),
            pl.BlockSpec((1, d), lambda i, j: (0, 0)),
            pl.BlockSpec((d, tf), lambda i, j: (0, j)),
            pl.BlockSpec((d, tf), lambda i, j: (0, j)),
            pl.BlockSpec((tf, d), lambda i, j: (j, 0)),
        ],
        out_specs=pl.BlockSpec((tm, d), lambda i, j: (i, 0)),
        out_shape=jax.ShapeDtypeStruct((m, d), F32),
        scratch_shapes=[pltpu.VMEM((tm, d), BF16)],
        compiler_params=_params("parallel", "arbitrary"),
        name="swiglu_ffn",
    )(x, gain, w_gate, w_up, w_down)


def _ple_kernel(x_ref, p_ref, g_ref, wg_ref, bg_ref, wp_ref, o_ref, h_ref, pb_ref, *, tn):
    j = pl.program_id(1)

    @pl.when(j == 0)
    def _():
        h_ref[...] = _rms_norm(x_ref[...], g_ref[...]).astype(BF16)
        pb_ref[...] = p_ref[...].astype(BF16)

    gate = jax.nn.sigmoid(_dot(h_ref[...], wg_ref[...]) + bg_ref[...])
    col0 = pl.multiple_of(j * tn, tn)
    o_ref[...] = x_ref[:, pl.ds(col0, tn)] + gate * _dot(pb_ref[...], wp_ref[...])


def _ple(x, p, gain, w_gate, b_gate, w_proj, *, tm, tn):
    m, d = x.shape
    pd = p.shape[1]
    kernel = functools.partial(_ple_kernel, tn=tn)
    return pl.pallas_call(
        kernel,
        grid=(m // tm, d // tn),
        in_specs=[
            pl.BlockSpec((tm, d), lambda i, j: (i, 0)),
            pl.BlockSpec((tm, pd), lambda i, j: (i, 0)),
            pl.BlockSpec((1, d), lambda i, j: (0, 0)),
            pl.BlockSpec((d, tn), lambda i, j: (0, j)),
            pl.BlockSpec((1, tn), lambda i, j: (0, j)),
            pl.BlockSpec((pd, tn), lambda i, j: (0, j)),
        ],
        out_specs=pl.BlockSpec((tm, tn), lambda i, j: (i, j)),
        out_shape=jax.ShapeDtypeStruct((m, d), F32),
        scratch_shapes=[pltpu.VMEM((tm, d), BF16), pltpu.VMEM((tm, pd), BF16)],
        compiler_params=_params("parallel", "arbitrary"),
        name="gated_embedding",
    )(x, p, gain, w_gate, b_gate, w_proj)


def _tile(n, target):
    if n <= target:
        return n
    t = target - target % V7X_LANES
    while n % t:
        t -= V7X_LANES
    return t


def kernel(x, p, norm_mix, norm_ffn, norm_ple, conv_w_in, conv_b_in, conv_w_dw, conv_b_dw, conv_ln_g,
           conv_ln_b, conv_w_out, conv_b_out, pool_w, pool_scale, attn_w_qkv, attn_q_norm, attn_k_norm,
           attn_sinks, attn_w_o, rel_bias, ffn_w_gate, ffn_w_up, ffn_w_down, ple_w_proj, ple_w_gate,
           ple_b_gate):
    batch, seq, d = x.shape
    depth = norm_mix.shape[0]
    n_heads = d // HEAD_DIM
    n_kv = (attn_w_qkv.shape[-1] // HEAD_DIM - n_heads) // 2
    group = n_heads // n_kv
    assert n_kv % 2 == 0 and group % 2 == 0, "attention kernel pairs heads into 128-lane tiles"
    assert d % (len(POOL_WINDOWS) * V7X_LANES) == 0
    m = batch * seq
    assert batch == 1 and m % QBLOCK == 0, "row tiles assume one sequence"

    tm = _tile(m, 512)
    tn = _tile(d, 512)
    tf = _tile(ffn_w_gate.shape[-1], 512)
    tp = _tile(m, 256)
    row = lambda v: v.reshape(1, -1)

    xs = x.reshape(m, d)
    bias = None
    for i in range(depth):
        kind, l = i % N_MIXERS, i // N_MIXERS
        gain = row(norm_mix[i])
        if kind == 0:
            u = _conv_in(xs, gain, conv_w_in[l].astype(BF16), row(conv_b_in[l]), tm=tm, tn=tn)
            xs = _conv_out(u, xs, conv_w_dw[l], row(conv_b_dw[l]), row(conv_ln_g[l]), row(conv_ln_b[l]),
                           conv_w_out[l].astype(BF16), row(conv_b_out[l]), tm=tm, tn=tn)
        elif kind == 1:
            xs = _pool(xs, gain, pool_w[l].astype(BF16), row(pool_scale[l]), tm=tp)
        else:
            if bias is None:
                bias = _band_bias(rel_bias)
            qkv = _norm_matmul(xs, gain, attn_w_qkv[l].astype(BF16), tm=tm,
                               tn=_tile(attn_w_qkv.shape[-1], 512))
            o = _attention(qkv, bias, attn_sinks[l], row(jnp.tile(attn_q_norm[l], 2)),
                           row(jnp.tile(attn_k_norm[l], 2)), n_heads=n_heads, n_kv=n_kv)
            xs = _matmul_residual(o, attn_w_o[l].astype(BF16), xs, tm=tm, tn=tn)
        xs = _ffn(xs, row(norm_ffn[i]), ffn_w_gate[i].astype(BF16), ffn_w_up[i].astype(BF16),
                  ffn_w_down[i].astype(BF16), tm=tm, tf=tf)
        xs = _ple(xs, p[i].reshape(m, -1), row(norm_ple[i]), ple_w_gate[i].astype(BF16),
                  row(ple_b_gate[i]), ple_w_proj[i].astype(BF16), tm=tm, tn=tn)
    return xs.reshape(batch, seq, d)
```

```python
import functools
import math

import jax
import jax.numpy as jnp
from jax import lax
from jax.experimental import pallas as pl
from jax.experimental.pallas import tpu as pltpu

N_MIXERS = 3
CHUNK = 64
CONV_WIDTH = 31
POOL_WINDOWS = (2, 4, 8, 16)
HEAD_DIM = 64
WINDOW_CHUNKS = 2
QBLOCK = 128
NUM_BUCKETS = 32
REL_MAX_DIST = 128
EPS = 1e-6
NEG_INF = -1e30

V7X_VMEM_BYTES = 64 * 1024 * 1024
V7X_LANES = 128
V7X_SUBLANES = 8
VMEM_LIMIT_BYTES = V7X_VMEM_BYTES * 7 // 8

CONV_HALO = 32
POOL_HALO = 16
CONV_ROWS = 64
CONV_STRIP = V7X_LANES
NORM_ROWS = 64

BF16 = jnp.bfloat16
F32 = jnp.float32


def _params(*semantics):
    return pltpu.CompilerParams(dimension_semantics=semantics, vmem_limit_bytes=VMEM_LIMIT_BYTES)


def _rms_norm(x, g):
    ms = jnp.mean(x * x, axis=-1, keepdims=True)
    return x * lax.rsqrt(ms + EPS) * g


def _dot(a, b):
    return jnp.dot(a, b, preferred_element_type=F32)


def _conv_in_kernel(x_ref, g_ref, wa_ref, wg_ref, ba_ref, bg_ref, u_ref, h_ref):
    @pl.when(pl.program_id(1) == 0)
    def _():
        h_ref[...] = _rms_norm(x_ref[...], g_ref[...]).astype(BF16)

    h = h_ref[...]
    a = _dot(h, wa_ref[...]) + ba_ref[...]
    gate = _dot(h, wg_ref[...]) + bg_ref[...]
    u_ref[...] = a * jax.nn.sigmoid(gate)


def _conv_in(x, gain, w_in, b_in, *, tm, tn):
    m, d = x.shape
    nj = d // tn
    return pl.pallas_call(
        _conv_in_kernel,
        grid=(m // tm, nj),
        in_specs=[
            pl.BlockSpec((tm, d), lambda i, j: (i, 0)),
            pl.BlockSpec((1, d), lambda i, j: (0, 0)),
            pl.BlockSpec((d, tn), lambda i, j: (0, j)),
            pl.BlockSpec((d, tn), lambda i, j: (0, j + nj)),
            pl.BlockSpec((1, tn), lambda i, j: (0, j)),
            pl.BlockSpec((1, tn), lambda i, j: (0, j + nj)),
        ],
        out_specs=pl.BlockSpec((tm, tn), lambda i, j: (i, j)),
        out_shape=jax.ShapeDtypeStruct((m, d), F32),
        scratch_shapes=[pltpu.VMEM((tm, d), BF16)],
        compiler_params=_params("parallel", "arbitrary"),
        name="conv_in",
    )(x, gain, w_in, w_in, b_in, b_in)


def _conv_out_kernel(u_ref, uprev_ref, wdw_ref, bdw_ref, lng_ref, lnb_ref, wout_ref, bout_ref, x_ref,
                     o_ref, ext_ref, conv_ref, v_ref, *, tm, tn):
    i = pl.program_id(0)
    j = pl.program_id(1)
    d = u_ref.shape[1]
    first = CONV_HALO - (CONV_WIDTH - 1)

    @pl.when(j == 0)
    def _():
        for s in range(d // CONV_STRIP):
            cols = slice(s * CONV_STRIP, (s + 1) * CONV_STRIP)
            ext_ref[s, 0:CONV_HALO, :] = jnp.where(i == 0, 0.0, uprev_ref[:, cols])
            ext_ref[s, CONV_HALO:, :] = u_ref[:, cols]
            taps = [jnp.broadcast_to(wdw_ref[k:k + 1, cols], (CONV_ROWS, CONV_STRIP))
                    for k in range(CONV_WIDTH)]
            bias = jnp.broadcast_to(bdw_ref[:, cols], (CONV_ROWS, CONV_STRIP))

            def conv_rows(c, carry, s=s, cols=cols, taps=taps, bias=bias):
                r0 = pl.multiple_of(c * CONV_ROWS, CONV_ROWS)
                acc = bias
                for k in range(CONV_WIDTH):
                    acc = acc + taps[k] * ext_ref[s, pl.ds(r0 + first + k, CONV_ROWS), :]
                conv_ref[pl.ds(r0, CONV_ROWS), cols] = acc
                return carry

            lax.fori_loop(0, tm // CONV_ROWS, conv_rows, 0)

        def norm_rows(c, carry):
            r0 = pl.multiple_of(c * NORM_ROWS, NORM_ROWS)
            conv = conv_ref[pl.ds(r0, NORM_ROWS), :]
            mu = jnp.mean(conv, axis=-1, keepdims=True)
            xc = conv - mu
            y = xc * lax.rsqrt(jnp.mean(xc * xc, axis=-1, keepdims=True) + EPS)
            y = y * lng_ref[...] + lnb_ref[...]
            v_ref[pl.ds(r0, NORM_ROWS), :] = (y * jax.nn.sigmoid(y)).astype(BF16)
            return carry

        lax.fori_loop(0, tm // NORM_ROWS, norm_rows, 0)

    o_ref[...] = x_ref[...] + _dot(v_ref[...], wout_ref[...]) + bout_ref[...]


def _conv_out(u, x, w_dw, b_dw, ln_g, ln_b, w_out, b_out, *, tm, tn):
    m, d = u.shape
    hb = tm // CONV_HALO
    kernel = functools.partial(_conv_out_kernel, tm=tm, tn=tn)
    return pl.pallas_call(
        kernel,
        grid=(m // tm, d // tn),
        in_specs=[
            pl.BlockSpec((tm, d), lambda i, j: (i, 0)),
            pl.BlockSpec((CONV_HALO, d), lambda i, j: (jnp.maximum(i * hb - 1, 0), 0)),
            pl.BlockSpec((CONV_WIDTH, d), lambda i, j: (0, 0)),
            pl.BlockSpec((1, d), lambda i, j: (0, 0)),
            pl.BlockSpec((1, d), lambda i, j: (0, 0)),
            pl.BlockSpec((1, d), lambda i, j: (0, 0)),
            pl.BlockSpec((d, tn), lambda i, j: (0, j)),
            pl.BlockSpec((1, tn), lambda i, j: (0, j)),
            pl.BlockSpec((tm, tn), lambda i, j: (i, j)),
        ],
        out_specs=pl.BlockSpec((tm, tn), lambda i, j: (i, j)),
        out_shape=jax.ShapeDtypeStruct((m, d), F32),
        scratch_shapes=[
            pltpu.VMEM((d // CONV_STRIP, tm + CONV_HALO, CONV_STRIP), F32),
            pltpu.VMEM((tm, d), F32),
            pltpu.VMEM((tm, d), BF16),
        ],
        compiler_params=_params("parallel", "arbitrary"),
        name="conv_out",
    )(u, u, w_dw, b_dw, ln_g, ln_b, w_out, b_out, x)


def _pool_kernel(x_ref, xprev_ref, g_ref, w_ref, scale_ref, o_ref, ext_ref, *, tm):
    i = pl.program_id(0)
    d = x_ref.shape[1]
    gc = d // len(POOL_WINDOWS)
    x = x_ref[...]
    h = _rms_norm(x, g_ref[...])
    ext_ref[0:POOL_HALO, :] = jnp.where(i == 0, 0.0, _rms_norm(xprev_ref[...], g_ref[...]))
    ext_ref[POOL_HALO:, :] = h
    t = i * tm + lax.broadcasted_iota(jnp.int32, (tm, 1), 0)
    for g, w in enumerate(POOL_WINDOWS):
        cols = slice(g * gc, (g + 1) * gc)
        s = h[:, cols]
        for k in range(1, w):
            s = s + ext_ref[POOL_HALO - k:POOL_HALO - k + tm, cols]
        cnt = jnp.minimum(t + 1, w).astype(F32)
        mix = (s / cnt - h[:, cols]).astype(BF16)
        y = _dot(mix, w_ref[g]) * scale_ref[:, cols]
        o_ref[:, cols] = x[:, cols] + y


def _pool(x, gain, w_grp, scale, *, tm):
    m, d = x.shape
    ng, gc, _ = w_grp.shape
    hb = tm // POOL_HALO
    kernel = functools.partial(_pool_kernel, tm=tm)
    return pl.pallas_call(
        kernel,
        grid=(m // tm,),
        in_specs=[
            pl.BlockSpec((tm, d), lambda i: (i, 0)),
            pl.BlockSpec((POOL_HALO, d), lambda i: (jnp.maximum(i * hb - 1, 0), 0)),
            pl.BlockSpec((1, d), lambda i: (0, 0)),
            pl.BlockSpec((ng, gc, gc), lambda i: (0, 0, 0)),
            pl.BlockSpec((1, d), lambda i: (0, 0)),
        ],
        out_specs=pl.BlockSpec((tm, d), lambda i: (i, 0)),
        out_shape=jax.ShapeDtypeStruct((m, d), F32),
        scratch_shapes=[pltpu.VMEM((tm + POOL_HALO, d), F32)],
        compiler_params=_params("parallel"),
        name="pool",
    )(x, x, gain, w_grp, scale)


def _norm_matmul_kernel(x_ref, g_ref, w_ref, o_ref, h_ref):
    @pl.when(pl.program_id(1) == 0)
    def _():
        h_ref[...] = _rms_norm(x_ref[...], g_ref[...]).astype(BF16)

    o_ref[...] = _dot(h_ref[...], w_ref[...])


def _norm_matmul(x, gain, w, *, tm, tn):
    m, d = x.shape
    n = w.shape[1]
    return pl.pallas_call(
        _norm_matmul_kernel,
        grid=(m // tm, n // tn),
        in_specs=[
            pl.BlockSpec((tm, d), lambda i, j: (i, 0)),
            pl.BlockSpec((1, d), lambda i, j: (0, 0)),
            pl.BlockSpec((d, tn), lambda i, j: (0, j)),
        ],
        out_specs=pl.BlockSpec((tm, tn), lambda i, j: (i, j)),
        out_shape=jax.ShapeDtypeStruct((m, n), F32),
        scratch_shapes=[pltpu.VMEM((tm, d), BF16)],
        compiler_params=_params("parallel", "arbitrary"),
        name="qkv_proj",
    )(x, gain, w)


def _t5_bucket(rel):
    nb = NUM_BUCKETS // 2
    n = -rel
    ret = jnp.where(n < 0, nb, 0)
    n = jnp.abs(n)
    max_exact = nb // 2
    nf = jnp.maximum(n, 1).astype(F32)
    large = max_exact + (jnp.log(nf / max_exact) / math.log(REL_MAX_DIST / max_exact)
                         * (nb - max_exact)).astype(jnp.int32)
    large = jnp.minimum(large, nb - 1)
    return ret + jnp.where(n < max_exact, n, large)


def _band_bias_kernel(bucket_ref, rel_bias_ref, o_ref):
    h = pl.program_id(0)
    bucket = bucket_ref[...]
    acc = jnp.zeros(bucket.shape, F32)
    for b in range(NUM_BUCKETS):
        acc = jnp.where(bucket == b, rel_bias_ref[b, h], acc)
    o_ref[0] = acc


def _band_bias(rel_bias):
    n_heads = rel_bias.shape[1]
    i = jnp.arange(QBLOCK)[:, None]
    j = jnp.arange(2 * QBLOCK)[None, :]
    bucket = _t5_bucket(j - QBLOCK - i).astype(jnp.int32)
    return pl.pallas_call(
        _band_bias_kernel,
        grid=(n_heads,),
        in_specs=[
            pl.BlockSpec((QBLOCK, 2 * QBLOCK), lambda h: (0, 0)),
            pl.BlockSpec(memory_space=pltpu.SMEM),
        ],
        out_specs=pl.BlockSpec((1, QBLOCK, 2 * QBLOCK), lambda h: (h, 0, 0)),
        out_shape=jax.ShapeDtypeStruct((n_heads, QBLOCK, 2 * QBLOCK), F32),
        compiler_params=_params("parallel"),
        name="band_bias",
    )(bucket, rel_bias)


def _half_rms_norm(x, g, low):
    sq = x * x
    ms_lo = jnp.sum(jnp.where(low, sq, 0.0), axis=-1, keepdims=True) * (1.0 / HEAD_DIM)
    ms_hi = jnp.sum(jnp.where(low, 0.0, sq), axis=-1, keepdims=True) * (1.0 / HEAD_DIM)
    r = jnp.where(low, lax.rsqrt(ms_lo + EPS), lax.rsqrt(ms_hi + EPS))
    return x * r * g


def _attn_kernel(sinks_ref, q_ref, kc_ref, kp_ref, vc_ref, vp_ref, bias_ref, qg_ref, kg_ref, o_ref,
                 *, n_kv, group):
    n = pl.program_id(0)
    pair = 2 * HEAD_DIM
    qi = lax.broadcasted_iota(jnp.int32, (QBLOCK, 2 * QBLOCK), 0)
    kj = lax.broadcasted_iota(jnp.int32, (QBLOCK, 2 * QBLOCK), 1)
    qc = qi // CHUNK
    kc = kj // CHUNK - QBLOCK // CHUNK
    visible = (kc <= qc) & (kc >= qc - WINDOW_CHUNKS) & ((n > 0) | (kj >= QBLOCK))
    low = lax.broadcasted_iota(jnp.int32, (1, pair), 1) < HEAD_DIM
    scale = HEAD_DIM ** -0.5

    for hp in range(n_kv // 2):
        kcols = slice(hp * pair, (hp + 1) * pair)
        k2 = jnp.concatenate([kp_ref[:, kcols], kc_ref[:, kcols]], axis=0)
        k2 = _half_rms_norm(k2, kg_ref[...], low).astype(BF16)
        v2 = jnp.concatenate([vp_ref[:, kcols], vc_ref[:, kcols]], axis=0).astype(BF16)
        for half in range(2):
            kv_head = 2 * hp + half
            for qp in range(group // 2):
                col = kv_head * group * HEAD_DIM + qp * pair
                q2 = _half_rms_norm(q_ref[:, col:col + pair], qg_ref[...], low) * scale
                outs = []
                for qhalf in range(2):
                    head = kv_head * group + 2 * qp + qhalf
                    qz = jnp.where(low == (qhalf == 0), q2, 0.0)
                    if qhalf != half:
                        qz = pltpu.roll(qz, HEAD_DIM, axis=1)
                    s = lax.dot_general(qz.astype(BF16), k2, (((1,), (1,)), ((), ())),
                                        preferred_element_type=F32)
                    s = jnp.where(visible, s + bias_ref[head], NEG_INF)
                    sink = sinks_ref[head]
                    mx = jnp.maximum(jnp.max(s, axis=-1, keepdims=True), sink)
                    e = jnp.exp(s - mx)
                    denom = jnp.sum(e, axis=-1, keepdims=True) + jnp.exp(sink - mx)
                    p = (e * (1.0 / denom)).astype(BF16)
                    o2 = _dot(p, v2)
                    if qhalf != half:
                        o2 = pltpu.roll(o2, HEAD_DIM, axis=1)
                    outs.append(o2)
                o_ref[:, col:col + pair] = jnp.where(low, outs[0], outs[1]).astype(BF16)


def _attention(qkv, bias, sinks, q_gain, k_gain, *, n_heads, n_kv):
    m = qkv.shape[0]
    group = n_heads // n_kv
    dq = n_heads * HEAD_DIM
    dkv = n_kv * HEAD_DIM
    kblk = dq // dkv
    kernel = functools.partial(_attn_kernel, n_kv=n_kv, group=group)
    prev = lambda n: jnp.maximum(n - 1, 0)
    return pl.pallas_call(
        kernel,
        grid=(m // QBLOCK,),
        in_specs=[
            pl.BlockSpec(memory_space=pltpu.SMEM),
            pl.BlockSpec((QBLOCK, dq), lambda n: (n, 0)),
            pl.BlockSpec((QBLOCK, dkv), lambda n: (n, kblk)),
            pl.BlockSpec((QBLOCK, dkv), lambda n: (prev(n), kblk)),
            pl.BlockSpec((QBLOCK, dkv), lambda n: (n, kblk + 1)),
            pl.BlockSpec((QBLOCK, dkv), lambda n: (prev(n), kblk + 1)),
            pl.BlockSpec((n_heads, QBLOCK, 2 * QBLOCK), lambda n: (0, 0, 0)),
            pl.BlockSpec((1, 2 * HEAD_DIM), lambda n: (0, 0)),
            pl.BlockSpec((1, 2 * HEAD_DIM), lambda n: (0, 0)),
        ],
        out_specs=pl.BlockSpec((QBLOCK, dq), lambda n: (n, 0)),
        out_shape=jax.ShapeDtypeStruct((m, dq), BF16),
        compiler_params=_params("parallel"),
        name="band_attention",
    )(sinks, qkv, qkv, qkv, qkv, qkv, bias, q_gain, k_gain)


def _matmul_residual_kernel(a_ref, w_ref, x_ref, o_ref):
    o_ref[...] = x_ref[...] + _dot(a_ref[...], w_ref[...])


def _matmul_residual(a, w, x, *, tm, tn):
    m, k = a.shape
    n = w.shape[1]
    return pl.pallas_call(
        _matmul_residual_kernel,
        grid=(m // tm, n // tn),
        in_specs=[
            pl.BlockSpec((tm, k), lambda i, j: (i, 0)),
            pl.BlockSpec((k, tn), lambda i, j: (0, j)),
            pl.BlockSpec((tm, tn), lambda i, j: (i, j)),
        ],
        out_specs=pl.BlockSpec((tm, tn), lambda i, j: (i, j)),
        out_shape=jax.ShapeDtypeStruct((m, n), F32),
        compiler_params=_params("parallel", "arbitrary"),
        name="attn_out_proj",
    )(a, w, x)


def _ffn_kernel(x_ref, g_ref, wg_ref, wu_ref, wd_ref, o_ref, h_ref):
    @pl.when(pl.program_id(1) == 0)
    def _():
        x = x_ref[...]
        h_ref[...] = _rms_norm(x, g_ref[...]).astype(BF16)
        o_ref[...] = x

    h = h_ref[...]
    gate = _dot(h, wg_ref[...])
    up = _dot(h, wu_ref[...])
    act = (gate * jax.nn.sigmoid(gate) * up).astype(BF16)
    o_ref[...] += _dot(act, wd_ref[...])


def _ffn(x, gain, w_gate, w_up, w_down, *, tm, tf):
    m, d = x.shape
    f = w_gate.shape[1]
    return pl.pallas_call(
        _ffn_kernel,
        grid=(m // tm, f // tf),
        in_specs=[
            pl.BlockSpec((tm, d), lambda i, j: (i, 0)),
            pl.BlockSpec((1, d), lambda i, j: (0, 0)),
            pl.BlockSpec((d, tf), lambda i, j: (0, j)),
            pl.BlockSpec((d, tf), lambda i, j: (0, j)),
            pl.BlockSpec((tf, d), lambda i, j: (j, 0)),
        ],
        out_specs=pl.BlockSpec((tm, d), lambda i, j: (i, 0)),
        out_shape=jax.ShapeDtypeStruct((m, d), F32),
        scratch_shapes=[pltpu.VMEM((tm, d), BF16)],
        compiler_params=_params("parallel", "arbitrary"),
        name="swiglu_ffn",
    )(x, gain, w_gate, w_up, w_down)


def _ple_kernel(x_ref, p_ref, g_ref, wg_ref, bg_ref, wp_ref, o_ref, h_ref, pb_ref, *, tn):
    j = pl.program_id(1)

    @pl.when(j == 0)
    def _():
        h_ref[...] = _rms_norm(x_ref[...], g_ref[...]).astype(BF16)
        pb_ref[...] = p_ref[...].astype(BF16)

    gate = jax.nn.sigmoid(_dot(h_ref[...], wg_ref[...]) + bg_ref[...])
    col0 = pl.multiple_of(j * tn, tn)
    o_ref[...] = x_ref[:, pl.ds(col0, tn)] + gate * _dot(pb_ref[...], wp_ref[...])


def _ple(x, p, gain, w_gate, b_gate, w_proj, *, tm, tn):
    m, d = x.shape
    pd = p.shape[1]
    kernel = functools.partial(_ple_kernel, tn=tn)
    return pl.pallas_call(
        kernel,
        grid=(m // tm, d // tn),
        in_specs=[
            pl.BlockSpec((tm, d), lambda i, j: (i, 0)),
            pl.BlockSpec((tm, pd), lambda i, j: (i, 0)),
            pl.BlockSpec((1, d), lambda i, j: (0, 0)),
            pl.BlockSpec((d, tn), lambda i, j: (0, j)),
            pl.BlockSpec((1, tn), lambda i, j: (0, j)),
            pl.BlockSpec((pd, tn), lambda i, j: (0, j)),
        ],
        out_specs=pl.BlockSpec((tm, tn), lambda i, j: (i, j)),
        out_shape=jax.ShapeDtypeStruct((m, d), F32),
        scratch_shapes=[pltpu.VMEM((tm, d), BF16), pltpu.VMEM((tm, pd), BF16)],
        compiler_params=_params("parallel", "arbitrary"),
        name="gated_embedding",
    )(x, p, gain, w_gate, b_gate, w_proj)


def _tile(n, target):
    if n <= target:
        return n
    t = target - target % V7X_LANES
    while n % t:
        t -= V7X_LANES
    return t


def kernel(x, p, norm_mix, norm_ffn, norm_ple, conv_w_in, conv_b_in, conv_w_dw, conv_b_dw, conv_ln_g,
           conv_ln_b, conv_w_out, conv_b_out, pool_w, pool_scale, attn_w_qkv, attn_q_norm, attn_k_norm,
           attn_sinks, attn_w_o, rel_bias, ffn_w_gate, ffn_w_up, ffn_w_down, ple_w_proj, ple_w_gate,
           ple_b_gate):
    batch, seq, d = x.shape
    depth = norm_mix.shape[0]
    n_heads = d // HEAD_DIM
    n_kv = (attn_w_qkv.shape[-1] // HEAD_DIM - n_heads) // 2
    group = n_heads // n_kv
    assert n_kv % 2 == 0 and group % 2 == 0, "attention kernel pairs heads into 128-lane tiles"
    assert d % (len(POOL_WINDOWS) * V7X_LANES) == 0
    m = batch * seq
    assert batch == 1 and m % QBLOCK == 0, "row tiles assume one sequence"

    tm = _tile(m, 512)
    tn = _tile(d, 512)
    tf = _tile(ffn_w_gate.shape[-1], 512)
    tp = _tile(m, 256)
    row = lambda v: v.reshape(1, -1)

    xs = x.reshape(m, d)
    bias = None
    for i in range(depth):
        kind, l = i % N_MIXERS, i // N_MIXERS
        gain = row(norm_mix[i])
        if kind == 0:
            u = _conv_in(xs, gain, conv_w_in[l].astype(BF16), row(conv_b_in[l]), tm=tm, tn=tn)
            xs = _conv_out(u, xs, conv_w_dw[l], row(conv_b_dw[l]), row(conv_ln_g[l]), row(conv_ln_b[l]),
                           conv_w_out[l].astype(BF16), row(conv_b_out[l]), tm=tm, tn=tn)
        elif kind == 1:
            xs = _pool(xs, gain, pool_w[l].astype(BF16), row(pool_scale[l]), tm=tp)
        else:
            if bias is None:
                bias = _band_bias(rel_bias)
            qkv = _norm_matmul(xs, gain, attn_w_qkv[l].astype(BF16), tm=tm,
                               tn=_tile(attn_w_qkv.shape[-1], 512))
            o = _attention(qkv, bias, attn_sinks[l], row(jnp.tile(attn_q_norm[l], 2)),
                           row(jnp.tile(attn_k_norm[l], 2)), n_heads=n_heads, n_kv=n_kv)
            xs = _matmul_residual(o, attn_w_o[l].astype(BF16), xs, tm=tm, tn=tn)
        xs = _ffn(xs, row(norm_ffn[i]), ffn_w_gate[i].astype(BF16), ffn_w_up[i].astype(BF16),
                  ffn_w_down[i].astype(BF16), tm=tm, tf=tf)
        xs = _ple(xs, p[i].reshape(m, -1), row(norm_ple[i]), ple_w_gate[i].astype(BF16),
                  row(ple_b_gate[i]), ple_w_proj[i].astype(BF16), tm=tm, tn=tn)
    return xs.reshape(batch, seq, d)
```

```python
import functools
import math

import jax
import jax.numpy as jnp
from jax import lax
from jax.experimental import pallas as pl
from jax.experimental.pallas import tpu as pltpu

N_MIXERS = 3
CHUNK = 64
CONV_WIDTH = 31
POOL_WINDOWS = (2, 4, 8, 16)
HEAD_DIM = 64
WINDOW_CHUNKS = 2
QBLOCK = 128
NUM_BUCKETS = 32
REL_MAX_DIST = 128
EPS = 1e-6
NEG_INF = -1e30

V7X_VMEM_BYTES = 64 * 1024 * 1024
V7X_LANES = 128
V7X_SUBLANES = 8
BF16_ROWS = 2 * V7X_SUBLANES
VMEM_LIMIT_BYTES = V7X_VMEM_BYTES * 7 // 8

CONV_HALO = 32
POOL_HALO = 16
CONV_ROWS = 64
CONV_STRIP = V7X_LANES
NORM_ROWS = 64

BF16 = jnp.bfloat16
F32 = jnp.float32


def _params(*semantics):
    return pltpu.CompilerParams(dimension_semantics=semantics, vmem_limit_bytes=VMEM_LIMIT_BYTES)


def _rms_norm(x, g):
    ms = jnp.mean(x * x, axis=-1, keepdims=True)
    return x * lax.rsqrt(ms + EPS) * g


def _dot(a, b):
    return jnp.dot(a, b, preferred_element_type=F32)


def _tile(n, target, multiple=V7X_LANES):
    if n <= target:
        return n
    t = target - target % multiple
    while n % t:
        t -= multiple
    return t


def _cast_kernel(src_ref, dst_ref):
    dst_ref[...] = src_ref[...].astype(BF16)


def _cast(a):
    r, c = a.shape
    rb = _tile(r, 256, BF16_ROWS)
    return pl.pallas_call(
        _cast_kernel,
        grid=(r // rb,),
        in_specs=[pl.BlockSpec((rb, c), lambda i: (i, 0))],
        out_specs=pl.BlockSpec((rb, c), lambda i: (i, 0)),
        out_shape=jax.ShapeDtypeStruct((r, c), BF16),
        compiler_params=_params("parallel"),
        name="weight_cast",
    )(a)


def _cast_plan(arrays, gi, gj, j_outer):
    in_specs, out_specs, out_shapes = [], [], []
    for a in arrays:
        r, c = a.shape
        if r % (gi * BF16_ROWS) == 0 and c % (gj * V7X_LANES) == 0:
            block, imap = (r // gi, c // gj), (lambda i, j: (i, j))
        elif r % (gj * BF16_ROWS) == 0 and c % (gi * V7X_LANES) == 0:
            block, imap = (r // gj, c // gi), (lambda i, j: (j, i))
        else:
            assert r % (gi * BF16_ROWS) == 0, a.shape
            block, imap = (r // gi, c), (lambda i, j: (i, 0))
        if j_outer:
            imap = (lambda f: lambda j, i: f(i, j))(imap)
        in_specs.append(pl.BlockSpec(block, imap))
        out_specs.append(pl.BlockSpec(block, imap))
        out_shapes.append(jax.ShapeDtypeStruct((r, c), BF16))
    return in_specs, out_specs, out_shapes


def _fused_call(body, *, name, grid, j_outer, in_specs, inputs, out_spec, out_shape, scratch=(), casts=()):
    n_in, n_cast = len(inputs), len(casts)
    gi, gj = (grid[1], grid[0]) if j_outer else grid
    cast_in, cast_out, cast_shapes = _cast_plan(casts, gi, gj, j_outer)

    def kernel(*refs):
        ins, rest = refs[:n_in], refs[n_in:]
        cast_src, rest = rest[:n_cast], rest[n_cast:]
        out, rest = rest[0], rest[1:]
        cast_dst, scr = rest[:n_cast], rest[n_cast:]
        for src, dst in zip(cast_src, cast_dst):
            dst[...] = src[...].astype(BF16)
        body(*ins, out, *scr)

    res = pl.pallas_call(
        kernel,
        grid=grid,
        in_specs=[*in_specs, *cast_in],
        out_specs=[out_spec, *cast_out],
        out_shape=[out_shape, *cast_shapes],
        scratch_shapes=list(scratch),
        compiler_params=_params("parallel", "parallel" if j_outer else "arbitrary"),
        name=name,
    )(*inputs, *casts)
    return res[0], list(res[1:])


def _conv_in_body(x_ref, g_ref, wa_ref, wg_ref, ba_ref, bg_ref, u_ref):
    h = _rms_norm(x_ref[...], g_ref[...]).astype(BF16)
    a = _dot(h, wa_ref[...]) + ba_ref[...]
    gate = _dot(h, wg_ref[...]) + bg_ref[...]
    u_ref[...] = a * jax.nn.sigmoid(gate)


def _conv_in(x, gain, w_in, b_in, *, tm, tn, casts):
    m, d = x.shape
    nj = d // tn
    return _fused_call(
        _conv_in_body,
        name="conv_in",
        grid=(nj, m // tm),
        j_outer=True,
        in_specs=[
            pl.BlockSpec((tm, d), lambda j, i: (i, 0)),
            pl.BlockSpec((1, d), lambda j, i: (0, 0)),
            pl.BlockSpec((d, tn), lambda j, i: (0, j)),
            pl.BlockSpec((d, tn), lambda j, i: (0, j + nj)),
            pl.BlockSpec((1, tn), lambda j, i: (0, j)),
            pl.BlockSpec((1, tn), lambda j, i: (0, j + nj)),
        ],
        inputs=(x, gain, w_in, w_in, b_in, b_in),
        out_spec=pl.BlockSpec((tm, tn), lambda j, i: (i, j)),
        out_shape=jax.ShapeDtypeStruct((m, d), F32),
        casts=casts,
    )


def _conv_out_body(u_ref, uprev_ref, wdw_ref, bdw_ref, lng_ref, lnb_ref, wout_ref, bout_ref, x_ref,
                   o_ref, ext_ref, conv_ref, v_ref, *, tm):
    i = pl.program_id(0)
    j = pl.program_id(1)
    d = u_ref.shape[1]
    first = CONV_HALO - (CONV_WIDTH - 1)

    @pl.when(j == 0)
    def _():
        for s in range(d // CONV_STRIP):
            cols = slice(s * CONV_STRIP, (s + 1) * CONV_STRIP)
            ext_ref[s, 0:CONV_HALO, :] = jnp.where(i == 0, 0.0, uprev_ref[:, cols])
            ext_ref[s, CONV_HALO:, :] = u_ref[:, cols]
            taps = [jnp.broadcast_to(wdw_ref[k:k + 1, cols], (CONV_ROWS, CONV_STRIP))
                    for k in range(CONV_WIDTH)]
            bias = jnp.broadcast_to(bdw_ref[:, cols], (CONV_ROWS, CONV_STRIP))

            def conv_rows(c, carry, s=s, cols=cols, taps=taps, bias=bias):
                r0 = pl.multiple_of(c * CONV_ROWS, CONV_ROWS)
                acc = bias
                for k in range(CONV_WIDTH):
                    acc = acc + taps[k] * ext_ref[s, pl.ds(r0 + first + k, CONV_ROWS), :]
                conv_ref[pl.ds(r0, CONV_ROWS), cols] = acc
                return carry

            lax.fori_loop(0, tm // CONV_ROWS, conv_rows, 0)

        def norm_rows(c, carry):
            r0 = pl.multiple_of(c * NORM_ROWS, NORM_ROWS)
            conv = conv_ref[pl.ds(r0, NORM_ROWS), :]
            mu = jnp.mean(conv, axis=-1, keepdims=True)
            xc = conv - mu
            y = xc * lax.rsqrt(jnp.mean(xc * xc, axis=-1, keepdims=True) + EPS)
            y = y * lng_ref[...] + lnb_ref[...]
            v_ref[pl.ds(r0, NORM_ROWS), :] = (y * jax.nn.sigmoid(y)).astype(BF16)
            return carry

        lax.fori_loop(0, tm // NORM_ROWS, norm_rows, 0)

    o_ref[...] = x_ref[...] + _dot(v_ref[...], wout_ref[...]) + bout_ref[...]


def _conv_out(u, x, w_dw, b_dw, ln_g, ln_b, w_out, b_out, *, tm, tn, casts):
    m, d = u.shape
    hb = tm // CONV_HALO
    return _fused_call(
        functools.partial(_conv_out_body, tm=tm),
        name="conv_out",
        grid=(m // tm, d // tn),
        j_outer=False,
        in_specs=[
            pl.BlockSpec((tm, d), lambda i, j: (i, 0)),
            pl.BlockSpec((CONV_HALO, d), lambda i, j: (jnp.maximum(i * hb - 1, 0), 0)),
            pl.BlockSpec((CONV_WIDTH, d), lambda i, j: (0, 0)),
            pl.BlockSpec((1, d), lambda i, j: (0, 0)),
            pl.BlockSpec((1, d), lambda i, j: (0, 0)),
            pl.BlockSpec((1, d), lambda i, j: (0, 0)),
            pl.BlockSpec((d, tn), lambda i, j: (0, j)),
            pl.BlockSpec((1, tn), lambda i, j: (0, j)),
            pl.BlockSpec((tm, tn), lambda i, j: (i, j)),
        ],
        inputs=(u, u, w_dw, b_dw, ln_g, ln_b, w_out, b_out, x),
        out_spec=pl.BlockSpec((tm, tn), lambda i, j: (i, j)),
        out_shape=jax.ShapeDtypeStruct((m, d), F32),
        scratch=[
            pltpu.VMEM((d // CONV_STRIP, tm + CONV_HALO, CONV_STRIP), F32),
            pltpu.VMEM((tm, d), F32),
            pltpu.VMEM((tm, d), BF16),
        ],
        casts=casts,
    )


def _pool_kernel(x_ref, xprev_ref, g_ref, w_ref, scale_ref, o_ref, ext_ref, *, tm):
    i = pl.program_id(0)
    d = x_ref.shape[1]
    gc = d // len(POOL_WINDOWS)
    x = x_ref[...]
    h = _rms_norm(x, g_ref[...])
    ext_ref[0:POOL_HALO, :] = jnp.where(i == 0, 0.0, _rms_norm(xprev_ref[...], g_ref[...]))
    ext_ref[POOL_HALO:, :] = h
    t = i * tm + lax.broadcasted_iota(jnp.int32, (tm, 1), 0)
    for g, w in enumerate(POOL_WINDOWS):
        cols = slice(g * gc, (g + 1) * gc)
        s = h[:, cols]
        for k in range(1, w):
            s = s + ext_ref[POOL_HALO - k:POOL_HALO - k + tm, cols]
        cnt = jnp.minimum(t + 1, w).astype(F32)
        mix = (s / cnt - h[:, cols]).astype(BF16)
        y = _dot(mix, w_ref[g]) * scale_ref[:, cols]
        o_ref[:, cols] = x[:, cols] + y


def _pool(x, gain, w_grp, scale, *, tm):
    m, d = x.shape
    ng, gc, _ = w_grp.shape
    hb = tm // POOL_HALO
    kernel = functools.partial(_pool_kernel, tm=tm)
    return pl.pallas_call(
        kernel,
        grid=(m // tm,),
        in_specs=[
            pl.BlockSpec((tm, d), lambda i: (i, 0)),
            pl.BlockSpec((POOL_HALO, d), lambda i: (jnp.maximum(i * hb - 1, 0), 0)),
            pl.BlockSpec((1, d), lambda i: (0, 0)),
            pl.BlockSpec((ng, gc, gc), lambda i: (0, 0, 0)),
            pl.BlockSpec((1, d), lambda i: (0, 0)),
        ],
        out_specs=pl.BlockSpec((tm, d), lambda i: (i, 0)),
        out_shape=jax.ShapeDtypeStruct((m, d), F32),
        scratch_shapes=[pltpu.VMEM((tm + POOL_HALO, d), F32)],
        compiler_params=_params("parallel"),
        name="pool",
    )(x, x, gain, w_grp, scale)


def _norm_matmul_body(x_ref, g_ref, w_ref, o_ref):
    o_ref[...] = _dot(_rms_norm(x_ref[...], g_ref[...]).astype(BF16), w_ref[...])


def _norm_matmul(x, gain, w, *, tm, tn, casts):
    m, d = x.shape
    n = w.shape[1]
    return _fused_call(
        _norm_matmul_body,
        name="qkv_proj",
        grid=(n // tn, m // tm),
        j_outer=True,
        in_specs=[
            pl.BlockSpec((tm, d), lambda j, i: (i, 0)),
            pl.BlockSpec((1, d), lambda j, i: (0, 0)),
            pl.BlockSpec((d, tn), lambda j, i: (0, j)),
        ],
        inputs=(x, gain, w),
        out_spec=pl.BlockSpec((tm, tn), lambda j, i: (i, j)),
        out_shape=jax.ShapeDtypeStruct((m, n), F32),
        casts=casts,
    )


def _t5_bucket(rel):
    nb = NUM_BUCKETS // 2
    n = -rel
    ret = jnp.where(n < 0, nb, 0)
    n = jnp.abs(n)
    max_exact = nb // 2
    nf = jnp.maximum(n, 1).astype(F32)
    large = max_exact + (jnp.log(nf / max_exact) / math.log(REL_MAX_DIST / max_exact)
                         * (nb - max_exact)).astype(jnp.int32)
    large = jnp.minimum(large, nb - 1)
    return ret + jnp.where(n < max_exact, n, large)


def _band_bias_kernel(bucket_ref, rel_bias_ref, o_ref):
    h = pl.program_id(0)
    bucket = bucket_ref[...]
    acc = jnp.zeros(bucket.shape, F32)
    for b in range(NUM_BUCKETS):
        acc = jnp.where(bucket == b, rel_bias_ref[b, h], acc)
    o_ref[0] = acc


def _band_bias(rel_bias):
    n_heads = rel_bias.shape[1]
    i = jnp.arange(QBLOCK)[:, None]
    j = jnp.arange(2 * QBLOCK)[None, :]
    bucket = _t5_bucket(j - QBLOCK - i).astype(jnp.int32)
    return pl.pallas_call(
        _band_bias_kernel,
        grid=(n_heads,),
        in_specs=[
            pl.BlockSpec((QBLOCK, 2 * QBLOCK), lambda h: (0, 0)),
            pl.BlockSpec(memory_space=pltpu.SMEM),
        ],
        out_specs=pl.BlockSpec((1, QBLOCK, 2 * QBLOCK), lambda h: (h, 0, 0)),
        out_shape=jax.ShapeDtypeStruct((n_heads, QBLOCK, 2 * QBLOCK), F32),
        compiler_params=_params("parallel"),
        name="band_bias",
    )(bucket, rel_bias)


def _half_rms_norm(x, g, low):
    sq = x * x
    ms_lo = jnp.sum(jnp.where(low, sq, 0.0), axis=-1, keepdims=True) * (1.0 / HEAD_DIM)
    ms_hi = jnp.sum(jnp.where(low, 0.0, sq), axis=-1, keepdims=True) * (1.0 / HEAD_DIM)
    r = jnp.where(low, lax.rsqrt(ms_lo + EPS), lax.rsqrt(ms_hi + EPS))
    return x * r * g


def _attn_kernel(sinks_ref, q_ref, kc_ref, kp_ref, vc_ref, vp_ref, bias_ref, qg_ref, kg_ref, o_ref,
                 *, n_kv, group):
    n = pl.program_id(0)
    pair = 2 * HEAD_DIM
    n_heads = n_kv * group
    qi = lax.broadcasted_iota(jnp.int32, (QBLOCK, 2 * QBLOCK), 0)
    kj = lax.broadcasted_iota(jnp.int32, (QBLOCK, 2 * QBLOCK), 1)
    qc = qi // CHUNK
    kc = kj // CHUNK - QBLOCK // CHUNK
    visible = (kc <= qc) & (kc >= qc - WINDOW_CHUNKS) & ((n > 0) | (kj >= QBLOCK))
    low = lax.broadcasted_iota(jnp.int32, (1, pair), 1) < HEAD_DIM
    scale = HEAD_DIM ** -0.5

    q_all = jnp.concatenate([q_ref[:, c * pair:(c + 1) * pair] for c in range(n_heads // 2)], axis=0)
    q_all = _half_rms_norm(q_all, qg_ref[...], low) * scale
    k_all = jnp.concatenate([ref[:, c * pair:(c + 1) * pair]
                             for c in range(n_kv // 2) for ref in (kp_ref, kc_ref)], axis=0)
    k_all = _half_rms_norm(k_all, kg_ref[...], low).astype(BF16)

    logits, values = [], []
    heads_per_tile = 2 * group
    for hp in range(n_kv // 2):
        kcols = slice(hp * pair, (hp + 1) * pair)
        values.append(jnp.concatenate([vp_ref[:, kcols], vc_ref[:, kcols]], axis=0).astype(BF16))
        stacked = []
        for head in range(hp * heads_per_tile, (hp + 1) * heads_per_tile):
            qhalf, half = head % 2, (head // group) % 2
            q2 = q_all[(head // 2) * QBLOCK:(head // 2 + 1) * QBLOCK]
            qz = jnp.where(low == (qhalf == 0), q2, 0.0)
            if qhalf != half:
                qz = pltpu.roll(qz, HEAD_DIM, axis=1)
            stacked.append(qz.astype(BF16))
        k2 = k_all[hp * 2 * QBLOCK:(hp + 1) * 2 * QBLOCK]
        s = lax.dot_general(jnp.concatenate(stacked, axis=0), k2, (((1,), (1,)), ((), ())),
                            preferred_element_type=F32)
        for g in range(heads_per_tile):
            rows = s[g * QBLOCK:(g + 1) * QBLOCK]
            logits.append(jnp.where(visible, rows + bias_ref[hp * heads_per_tile + g], NEG_INF))

    s = jnp.concatenate(logits, axis=0)
    sink = jnp.concatenate([jnp.full((QBLOCK, 1), sinks_ref[h], F32) for h in range(n_heads)], axis=0)
    mx = jnp.maximum(jnp.max(s, axis=-1, keepdims=True), sink)
    e = jnp.exp(s - mx)
    denom = jnp.sum(e, axis=-1, keepdims=True) + jnp.exp(sink - mx)
    r = 1.0 / denom
    e = e.astype(BF16)

    for hp in range(n_kv // 2):
        rows = slice(hp * heads_per_tile * QBLOCK, (hp + 1) * heads_per_tile * QBLOCK)
        o = _dot(e[rows], values[hp]) * r[rows]
        for c in range(group):
            halves = []
            for qhalf in range(2):
                g = 2 * c + qhalf
                o2 = o[g * QBLOCK:(g + 1) * QBLOCK]
                if qhalf != (g // group) % 2:
                    o2 = pltpu.roll(o2, HEAD_DIM, axis=1)
                halves.append(o2)
            col = (hp * group + c) * pair
            o_ref[:, col:col + pair] = jnp.where(low, halves[0], halves[1]).astype(BF16)


def _attention(qkv, bias, sinks, q_gain, k_gain, *, n_heads, n_kv):
    m = qkv.shape[0]
    group = n_heads // n_kv
    dq = n_heads * HEAD_DIM
    dkv = n_kv * HEAD_DIM
    kblk = dq // dkv
    kernel = functools.partial(_attn_kernel, n_kv=n_kv, group=group)
    prev = lambda n: jnp.maximum(n - 1, 0)
    return pl.pallas_call(
        kernel,
        grid=(m // QBLOCK,),
        in_specs=[
            pl.BlockSpec(memory_space=pltpu.SMEM),
            pl.BlockSpec((QBLOCK, dq), lambda n: (n, 0)),
            pl.BlockSpec((QBLOCK, dkv), lambda n: (n, kblk)),
            pl.BlockSpec((QBLOCK, dkv), lambda n: (prev(n), kblk)),
            pl.BlockSpec((QBLOCK, dkv), lambda n: (n, kblk + 1)),
            pl.BlockSpec((QBLOCK, dkv), lambda n: (prev(n), kblk + 1)),
            pl.BlockSpec((n_heads, QBLOCK, 2 * QBLOCK), lambda n: (0, 0, 0)),
            pl.BlockSpec((1, 2 * HEAD_DIM), lambda n: (0, 0)),
            pl.BlockSpec((1, 2 * HEAD_DIM), lambda n: (0, 0)),
        ],
        out_specs=pl.BlockSpec((QBLOCK, dq), lambda n: (n, 0)),
        out_shape=jax.ShapeDtypeStruct((m, dq), BF16),
        compiler_params=_params("parallel"),
        name="band_attention",
    )(sinks, qkv, qkv, qkv, qkv, qkv, bias, q_gain, k_gain)


def _matmul_residual_body(a_ref, w_ref, x_ref, o_ref):
    o_ref[...] = x_ref[...] + _dot(a_ref[...], w_ref[...])


def _matmul_residual(a, w, x, *, tm, tn, casts):
    m, k = a.shape
    n = w.shape[1]
    return _fused_call(
        _matmul_residual_body,
        name="attn_out_proj",
        grid=(n // tn, m // tm),
        j_outer=True,
        in_specs=[
            pl.BlockSpec((tm, k), lambda j, i: (i, 0)),
            pl.BlockSpec((k, tn), lambda j, i: (0, j)),
            pl.BlockSpec((tm, tn), lambda j, i: (i, j)),
        ],
        inputs=(a, w, x),
        out_spec=pl.BlockSpec((tm, tn), lambda j, i: (i, j)),
        out_shape=jax.ShapeDtypeStruct((m, n), F32),
        casts=casts,
    )


def _ffn_body(x_ref, g_ref, wg_ref, wu_ref, wd_ref, o_ref, h_ref):
    @pl.when(pl.program_id(1) == 0)
    def _():
        x = x_ref[...]
        h_ref[...] = _rms_norm(x, g_ref[...]).astype(BF16)
        o_ref[...] = x

    h = h_ref[...]
    gate = _dot(h, wg_ref[...])
    up = _dot(h, wu_ref[...])
    act = (gate * jax.nn.sigmoid(gate) * up).astype(BF16)
    o_ref[...] += _dot(act, wd_ref[...])


def _ffn(x, gain, w_gate, w_up, w_down, *, tm, tf, casts):
    m, d = x.shape
    f = w_gate.shape[1]
    return _fused_call(
        _ffn_body,
        name="swiglu_ffn",
        grid=(m // tm, f // tf),
        j_outer=False,
        in_specs=[
            pl.BlockSpec((tm, d), lambda i, j: (i, 0)),
            pl.BlockSpec((1, d), lambda i, j: (0, 0)),
            pl.BlockSpec((d, tf), lambda i, j: (0, j)),
            pl.BlockSpec((d, tf), lambda i, j: (0, j)),
            pl.BlockSpec((tf, d), lambda i, j: (j, 0)),
        ],
        inputs=(x, gain, w_gate, w_up, w_down),
        out_spec=pl.BlockSpec((tm, d), lambda i, j: (i, 0)),
        out_shape=jax.ShapeDtypeStruct((m, d), F32),
        scratch=[pltpu.VMEM((tm, d), BF16)],
        casts=casts,
    )


def _ple_body(x_ref, p_ref, g_ref, wg_ref, bg_ref, wp_ref, o_ref, *, tn):
    x = x_ref[...]
    h = _rms_norm(x, g_ref[...]).astype(BF16)
    gate = jax.nn.sigmoid(_dot(h, wg_ref[...]) + bg_ref[...])
    col0 = pl.multiple_of(pl.program_id(0) * tn, tn)
    o_ref[...] = x_ref[:, pl.ds(col0, tn)] + gate * _dot(p_ref[...].astype(BF16), wp_ref[...])


def _ple(x, p, gain, w_gate, b_gate, w_proj, *, tm, tn):
    m, d = x.shape
    pd = p.shape[1]
    out, _ = _fused_call(
        functools.partial(_ple_body, tn=tn),
        name="gated_embedding",
        grid=(d // tn, m // tm),
        j_outer=True,
        in_specs=[
            pl.BlockSpec((tm, d), lambda j, i: (i, 0)),
            pl.BlockSpec((tm, pd), lambda j, i: (i, 0)),
            pl.BlockSpec((1, d), lambda j, i: (0, 0)),
            pl.BlockSpec((d, tn), lambda j, i: (0, j)),
            pl.BlockSpec((1, tn), lambda j, i: (0, j)),
            pl.BlockSpec((pd, tn), lambda j, i: (0, j)),
        ],
        inputs=(x, p, gain, w_gate, b_gate, w_proj),
        out_spec=pl.BlockSpec((tm, tn), lambda j, i: (i, j)),
        out_shape=jax.ShapeDtypeStruct((m, d), F32),
    )
    return out


def kernel(x, p, norm_mix, norm_ffn, norm_ple, conv_w_in, conv_b_in, conv_w_dw, conv_b_dw, conv_ln_g,
           conv_ln_b, conv_w_out, conv_b_out, pool_w, pool_scale, attn_w_qkv, attn_q_norm, attn_k_norm,
           attn_sinks, attn_w_o, rel_bias, ffn_w_gate, ffn_w_up, ffn_w_down, ple_w_proj, ple_w_gate,
           ple_b_gate):
    batch, seq, d = x.shape
    depth = norm_mix.shape[0]
    n_heads = d // HEAD_DIM
    n_kv = (attn_w_qkv.shape[-1] // HEAD_DIM - n_heads) // 2
    group = n_heads // n_kv
    assert n_kv % 2 == 0 and group % 2 == 0, "attention kernel pairs heads into 128-lane tiles"
    assert d % (len(POOL_WINDOWS) * V7X_LANES) == 0
    m = batch * seq
    assert batch == 1 and m % QBLOCK == 0, "row tiles assume one sequence"

    tm = _tile(m, 512)
    tn = _tile(d, 1024)
    tf = _tile(ffn_w_gate.shape[-1], 512)
    tp = _tile(m, 256)
    row = lambda v: v.reshape(1, -1)

    def mixer_weights(i):
        kind, l = i % N_MIXERS, i // N_MIXERS
        if kind == 0:
            return {("conv_in", i): conv_w_in[l], ("conv_out", i): conv_w_out[l]}
        if kind == 1:
            return {("pool", i): pool_w[l].reshape(-1, pool_w.shape[-1])}
        return {("qkv", i): attn_w_qkv[l], ("attn_o", i): attn_w_o[l]}

    def ffn_weights(i):
        return {("gate", i): ffn_w_gate[i], ("up", i): ffn_w_up[i], ("down", i): ffn_w_down[i]}

    def ple_weights(i):
        return {("ple_gate", i): ple_w_gate[i], ("ple_proj", i): ple_w_proj[i]}

    ready = {}

    def bf16(key, w):
        return ready.pop(key) if key in ready else _cast(w)

    def hosting(weights, call):
        out, copies = call(list(weights.values()))
        ready.update(zip(weights.keys(), copies))
        return out

    xs = x.reshape(m, d)
    bias = None
    for i in range(depth):
        kind, l = i % N_MIXERS, i // N_MIXERS
        gain = row(norm_mix[i])
        mine = {} if ("gate", i) in ready else {**ffn_weights(i), **ple_weights(i)}
        if kind == 0:
            w_in, w_out = bf16(("conv_in", i), conv_w_in[l]), bf16(("conv_out", i), conv_w_out[l])
            first = {k: v for k, v in mine.items() if k[0] in ("gate", "up")}
            rest = {k: v for k, v in mine.items() if k not in first}
            u = hosting(first, lambda c: _conv_in(xs, gain, w_in, row(conv_b_in[l]), tm=tm, tn=tn, casts=c))
            xs = hosting(rest, lambda c: _conv_out(
                u, xs, conv_w_dw[l], row(conv_b_dw[l]), row(conv_ln_g[l]), row(conv_ln_b[l]), w_out,
                row(conv_b_out[l]), tm=tm, tn=tn, casts=c))
        elif kind == 1:
            w_grp = bf16(("pool", i), pool_w[l].reshape(-1, pool_w.shape[-1])).reshape(pool_w.shape[1:])
            xs = _pool(xs, gain, w_grp, row(pool_scale[l]), tm=tp)
        else:
            if bias is None:
                bias = _band_bias(rel_bias)
            w_qkv, w_o = bf16(("qkv", i), attn_w_qkv[l]), bf16(("attn_o", i), attn_w_o[l])
            qkv = hosting(mine, lambda c: _norm_matmul(
                xs, gain, w_qkv, tm=tm, tn=_tile(attn_w_qkv.shape[-1], 1024), casts=c))
            o = _attention(qkv, bias, attn_sinks[l], row(jnp.tile(attn_q_norm[l], 2)),
                           row(jnp.tile(attn_k_norm[l], 2)), n_heads=n_heads, n_kv=n_kv)
            xs, _ = _matmul_residual(o, w_o, xs, tm=tm, tn=tn, casts=[])
        w_gate, w_up, w_down = (bf16(k, w) for k, w in ffn_weights(i).items())
        ahead = {}
        if i + 1 < depth:
            ahead = {**mixer_weights(i + 1), **ffn_weights(i + 1), **ple_weights(i + 1)}
        xs = hosting(ahead, lambda c: _ffn(xs, row(norm_ffn[i]), w_gate, w_up, w_down, tm=tm, tf=tf, casts=c))
        w_pg, w_pp = (bf16(k, w) for k, w in ple_weights(i).items())
        xs = _ple(xs, p[i].reshape(m, -1), row(norm_ple[i]), w_pg, row(ple_b_gate[i]), w_pp, tm=tm, tn=tn)
    return xs.reshape(batch, seq, d)
```

```python
import functools
import math

import jax
import jax.numpy as jnp
from jax import lax
from jax.experimental import pallas as pl
from jax.experimental.pallas import tpu as pltpu

N_MIXERS = 3
CHUNK = 64
CONV_WIDTH = 31
POOL_WINDOWS = (2, 4, 8, 16)
HEAD_DIM = 64
WINDOW_CHUNKS = 2
QBLOCK = 128
NUM_BUCKETS = 32
REL_MAX_DIST = 128
EPS = 1e-6
NEG_INF = -1e30

V7X_VMEM_BYTES = 64 * 1024 * 1024
V7X_LANES = 128
V7X_SUBLANES = 8
BF16_ROWS = 2 * V7X_SUBLANES
VMEM_LIMIT_BYTES = V7X_VMEM_BYTES * 7 // 8

CONV_HALO = 32
POOL_HALO = 16
CONV_ROWS = 64
CONV_STRIP = V7X_LANES
NORM_ROWS = 64

BF16 = jnp.bfloat16
F32 = jnp.float32


def _params(*semantics):
    return pltpu.CompilerParams(dimension_semantics=semantics, vmem_limit_bytes=VMEM_LIMIT_BYTES)


def _rms_norm(x, g):
    ms = jnp.mean(x * x, axis=-1, keepdims=True)
    return x * lax.rsqrt(ms + EPS) * g


def _dot(a, b):
    return jnp.dot(a, b, preferred_element_type=F32)


def _tile(n, target, multiple=V7X_LANES):
    if n <= target:
        return n
    t = target - target % multiple
    while n % t:
        t -= multiple
    return t


def _copy_tiles(src_ref, dst_ref):
    tile = dst_ref.shape[-1]
    for t in range(dst_ref.shape[0]):
        dst_ref[t] = src_ref[:, t * tile:(t + 1) * tile].astype(BF16)


def _cast(stack, layer, tile):
    _, r, c = stack.shape
    rb = _tile(r, 256, BF16_ROWS)
    return pl.pallas_call(
        _copy_tiles,
        grid=(r // rb,),
        in_specs=[pl.BlockSpec((None, rb, c), lambda i: (layer, i, 0))],
        out_specs=pl.BlockSpec((c // tile, rb, tile), lambda i: (0, i, 0)),
        out_shape=jax.ShapeDtypeStruct((c // tile, r, tile), BF16),
        compiler_params=_params("parallel"),
        name="weight_cast",
    )(stack)


def _cast_plan(sources, gi, gj, j_outer):
    in_specs, out_specs, out_shapes, row_only = [], [], [], []
    for stack, layer, tile in sources:
        _, r, c = stack.shape
        once = False
        if r % (gi * gj * BF16_ROWS) == 0:
            rows, cols = r // (gi * gj), c
            src_map = lambda i, j: (i * gj + j, 0)
            dst_map = lambda i, j: (0, i * gj + j, 0)
        elif r % (gi * BF16_ROWS) == 0 and c % gj == 0 and (c // gj) % tile == 0:
            rows, cols = r // gi, c // gj
            src_map = lambda i, j: (i, j)
            dst_map = lambda i, j: (j, i, 0)
        elif r % (gj * BF16_ROWS) == 0 and c % gi == 0 and (c // gi) % tile == 0:
            rows, cols = r // gj, c // gi
            src_map = lambda i, j: (j, i)
            dst_map = lambda i, j: (i, j, 0)
        else:
            assert r % (gi * BF16_ROWS) == 0 and not j_outer, stack.shape
            rows, cols = r // gi, c
            src_map = lambda i, j: (i, 0)
            dst_map = lambda i, j: (0, i, 0)
            once = gj > 1
        if j_outer:
            src_map = (lambda f: lambda j, i: f(i, j))(src_map)
            dst_map = (lambda f: lambda j, i: f(i, j))(dst_map)
        src_map = (lambda f, l: lambda a, b: (l, *f(a, b)))(src_map, layer)
        in_specs.append(pl.BlockSpec((None, rows, cols), src_map))
        out_specs.append(pl.BlockSpec((cols // tile, rows, tile), dst_map))
        out_shapes.append(jax.ShapeDtypeStruct((c // tile, r, tile), BF16))
        row_only.append(once)
    return in_specs, out_specs, out_shapes, row_only


def _fused_call(body, *, name, grid, j_outer, in_specs, inputs, out_spec, out_shape, scratch=(), casts=()):
    n_in, n_cast = len(inputs), len(casts)
    gi, gj = (grid[1], grid[0]) if j_outer else grid
    cast_in, cast_out, cast_shapes, row_only = _cast_plan(casts, gi, gj, j_outer)

    def kernel(*refs):
        ins, rest = refs[:n_in], refs[n_in:]
        cast_src, rest = rest[:n_cast], rest[n_cast:]
        out, rest = rest[0], rest[1:]
        cast_dst, scr = rest[:n_cast], rest[n_cast:]
        for src, dst, once in zip(cast_src, cast_dst, row_only):
            if once:
                pl.when(pl.program_id(1) == 0)(functools.partial(_copy_tiles, src, dst))
            else:
                _copy_tiles(src, dst)
        body(*ins, out, *scr)

    res = pl.pallas_call(
        kernel,
        grid=grid,
        in_specs=[*in_specs, *cast_in],
        out_specs=[out_spec, *cast_out],
        out_shape=[out_shape, *cast_shapes],
        scratch_shapes=list(scratch),
        compiler_params=_params("parallel", "parallel" if j_outer else "arbitrary"),
        name=name,
    )(*inputs, *[stack for stack, _, _ in casts])
    return res[0], list(res[1:])


def _conv_in_body(x_ref, g_ref, wa_ref, wg_ref, ba_ref, bg_ref, u_ref):
    h = _rms_norm(x_ref[...], g_ref[...]).astype(BF16)
    a = _dot(h, wa_ref[...]) + ba_ref[...]
    gate = _dot(h, wg_ref[...]) + bg_ref[...]
    u_ref[...] = a * jax.nn.sigmoid(gate)


def _conv_in(x, gain, w_in, b_in, *, tm, tn, casts):
    m, d = x.shape
    nj = d // tn
    return _fused_call(
        _conv_in_body,
        name="conv_in",
        grid=(nj, m // tm),
        j_outer=True,
        in_specs=[
            pl.BlockSpec((tm, d), lambda j, i: (i, 0)),
            pl.BlockSpec((1, d), lambda j, i: (0, 0)),
            pl.BlockSpec((None, d, tn), lambda j, i: (j, 0, 0)),
            pl.BlockSpec((None, d, tn), lambda j, i: (j + nj, 0, 0)),
            pl.BlockSpec((1, tn), lambda j, i: (0, j)),
            pl.BlockSpec((1, tn), lambda j, i: (0, j + nj)),
        ],
        inputs=(x, gain, w_in, w_in, b_in, b_in),
        out_spec=pl.BlockSpec((tm, tn), lambda j, i: (i, j)),
        out_shape=jax.ShapeDtypeStruct((m, d), F32),
        casts=casts,
    )


def _conv_out_body(u_ref, uprev_ref, wdw_ref, bdw_ref, lng_ref, lnb_ref, wout_ref, bout_ref, x_ref,
                   o_ref, ext_ref, conv_ref, v_ref, *, tm):
    i = pl.program_id(0)
    j = pl.program_id(1)
    d = u_ref.shape[1]
    first = CONV_HALO - (CONV_WIDTH - 1)

    @pl.when(j == 0)
    def _():
        for s in range(d // CONV_STRIP):
            cols = slice(s * CONV_STRIP, (s + 1) * CONV_STRIP)
            ext_ref[s, 0:CONV_HALO, :] = jnp.where(i == 0, 0.0, uprev_ref[:, cols])
            ext_ref[s, CONV_HALO:, :] = u_ref[:, cols]
            taps = [jnp.broadcast_to(wdw_ref[k:k + 1, cols], (CONV_ROWS, CONV_STRIP))
                    for k in range(CONV_WIDTH)]
            bias = jnp.broadcast_to(bdw_ref[:, cols], (CONV_ROWS, CONV_STRIP))

            def conv_rows(c, carry, s=s, cols=cols, taps=taps, bias=bias):
                r0 = pl.multiple_of(c * CONV_ROWS, CONV_ROWS)
                acc = bias
                for k in range(CONV_WIDTH):
                    acc = acc + taps[k] * ext_ref[s, pl.ds(r0 + first + k, CONV_ROWS), :]
                conv_ref[pl.ds(r0, CONV_ROWS), cols] = acc
                return carry

            lax.fori_loop(0, tm // CONV_ROWS, conv_rows, 0)

        def norm_rows(c, carry):
            r0 = pl.multiple_of(c * NORM_ROWS, NORM_ROWS)
            conv = conv_ref[pl.ds(r0, NORM_ROWS), :]
            mu = jnp.mean(conv, axis=-1, keepdims=True)
            xc = conv - mu
            y = xc * lax.rsqrt(jnp.mean(xc * xc, axis=-1, keepdims=True) + EPS)
            y = y * lng_ref[...] + lnb_ref[...]
            v_ref[pl.ds(r0, NORM_ROWS), :] = (y * jax.nn.sigmoid(y)).astype(BF16)
            return carry

        lax.fori_loop(0, tm // NORM_ROWS, norm_rows, 0)

    o_ref[...] = x_ref[...] + _dot(v_ref[...], wout_ref[...]) + bout_ref[...]


def _conv_out(u, x, w_dw, b_dw, ln_g, ln_b, w_out, b_out, *, tm, tn, casts):
    m, d = u.shape
    hb = tm // CONV_HALO
    return _fused_call(
        functools.partial(_conv_out_body, tm=tm),
        name="conv_out",
        grid=(m // tm, d // tn),
        j_outer=False,
        in_specs=[
            pl.BlockSpec((tm, d), lambda i, j: (i, 0)),
            pl.BlockSpec((CONV_HALO, d), lambda i, j: (jnp.maximum(i * hb - 1, 0), 0)),
            pl.BlockSpec((CONV_WIDTH, d), lambda i, j: (0, 0)),
            pl.BlockSpec((1, d), lambda i, j: (0, 0)),
            pl.BlockSpec((1, d), lambda i, j: (0, 0)),
            pl.BlockSpec((1, d), lambda i, j: (0, 0)),
            pl.BlockSpec((None, d, tn), lambda i, j: (j, 0, 0)),
            pl.BlockSpec((1, tn), lambda i, j: (0, j)),
            pl.BlockSpec((tm, tn), lambda i, j: (i, j)),
        ],
        inputs=(u, u, w_dw, b_dw, ln_g, ln_b, w_out, b_out, x),
        out_spec=pl.BlockSpec((tm, tn), lambda i, j: (i, j)),
        out_shape=jax.ShapeDtypeStruct((m, d), F32),
        scratch=[
            pltpu.VMEM((d // CONV_STRIP, tm + CONV_HALO, CONV_STRIP), F32),
            pltpu.VMEM((tm, d), F32),
            pltpu.VMEM((tm, d), BF16),
        ],
        casts=casts,
    )


def _pool_kernel(x_ref, xprev_ref, g_ref, w_ref, scale_ref, o_ref, ext_ref, *, tm):
    i = pl.program_id(0)
    d = x_ref.shape[1]
    gc = d // len(POOL_WINDOWS)
    x = x_ref[...]
    h = _rms_norm(x, g_ref[...])
    ext_ref[0:POOL_HALO, :] = jnp.where(i == 0, 0.0, _rms_norm(xprev_ref[...], g_ref[...]))
    ext_ref[POOL_HALO:, :] = h
    t = i * tm + lax.broadcasted_iota(jnp.int32, (tm, 1), 0)
    for g, w in enumerate(POOL_WINDOWS):
        cols = slice(g * gc, (g + 1) * gc)
        s = h[:, cols]
        for k in range(1, w):
            s = s + ext_ref[POOL_HALO - k:POOL_HALO - k + tm, cols]
        cnt = jnp.minimum(t + 1, w).astype(F32)
        mix = (s / cnt - h[:, cols]).astype(BF16)
        y = _dot(mix, w_ref[g]) * scale_ref[:, cols]
        o_ref[:, cols] = x[:, cols] + y


def _pool(x, gain, w_grp, scale, *, tm):
    m, d = x.shape
    ng, gc, _ = w_grp.shape
    hb = tm // POOL_HALO
    kernel = functools.partial(_pool_kernel, tm=tm)
    return pl.pallas_call(
        kernel,
        grid=(m // tm,),
        in_specs=[
            pl.BlockSpec((tm, d), lambda i: (i, 0)),
            pl.BlockSpec((POOL_HALO, d), lambda i: (jnp.maximum(i * hb - 1, 0), 0)),
            pl.BlockSpec((1, d), lambda i: (0, 0)),
            pl.BlockSpec((ng, gc, gc), lambda i: (0, 0, 0)),
            pl.BlockSpec((1, d), lambda i: (0, 0)),
        ],
        out_specs=pl.BlockSpec((tm, d), lambda i: (i, 0)),
        out_shape=jax.ShapeDtypeStruct((m, d), F32),
        scratch_shapes=[pltpu.VMEM((tm + POOL_HALO, d), F32)],
        compiler_params=_params("parallel"),
        name="pool",
    )(x, x, gain, w_grp, scale)


def _norm_matmul_body(x_ref, g_ref, w_ref, o_ref):
    o_ref[...] = _dot(_rms_norm(x_ref[...], g_ref[...]).astype(BF16), w_ref[...])


def _norm_matmul(x, gain, w, *, tm, tn, casts):
    m, d = x.shape
    n = w.shape[0] * tn
    return _fused_call(
        _norm_matmul_body,
        name="qkv_proj",
        grid=(n // tn, m // tm),
        j_outer=True,
        in_specs=[
            pl.BlockSpec((tm, d), lambda j, i: (i, 0)),
            pl.BlockSpec((1, d), lambda j, i: (0, 0)),
            pl.BlockSpec((None, d, tn), lambda j, i: (j, 0, 0)),
        ],
        inputs=(x, gain, w),
        out_spec=pl.BlockSpec((tm, tn), lambda j, i: (i, j)),
        out_shape=jax.ShapeDtypeStruct((m, n), F32),
        casts=casts,
    )


def _t5_bucket(rel):
    nb = NUM_BUCKETS // 2
    n = -rel
    ret = jnp.where(n < 0, nb, 0)
    n = jnp.abs(n)
    max_exact = nb // 2
    nf = jnp.maximum(n, 1).astype(F32)
    large = max_exact + (jnp.log(nf / max_exact) / math.log(REL_MAX_DIST / max_exact)
                         * (nb - max_exact)).astype(jnp.int32)
    large = jnp.minimum(large, nb - 1)
    return ret + jnp.where(n < max_exact, n, large)


def _band_bias_kernel(bucket_ref, rel_bias_ref, o_ref):
    h = pl.program_id(0)
    bucket = bucket_ref[...]
    acc = jnp.zeros(bucket.shape, F32)
    for b in range(NUM_BUCKETS):
        acc = jnp.where(bucket == b, rel_bias_ref[b, h], acc)
    o_ref[0] = acc


def _band_bias(rel_bias):
    n_heads = rel_bias.shape[1]
    i = jnp.arange(QBLOCK)[:, None]
    j = jnp.arange(2 * QBLOCK)[None, :]
    bucket = _t5_bucket(j - QBLOCK - i).astype(jnp.int32)
    return pl.pallas_call(
        _band_bias_kernel,
        grid=(n_heads,),
        in_specs=[
            pl.BlockSpec((QBLOCK, 2 * QBLOCK), lambda h: (0, 0)),
            pl.BlockSpec(memory_space=pltpu.SMEM),
        ],
        out_specs=pl.BlockSpec((1, QBLOCK, 2 * QBLOCK), lambda h: (h, 0, 0)),
        out_shape=jax.ShapeDtypeStruct((n_heads, QBLOCK, 2 * QBLOCK), F32),
        compiler_params=_params("parallel"),
        name="band_bias",
    )(bucket, rel_bias)


def _half_rms_norm(x, g, low):
    sq = x * x
    ms_lo = jnp.sum(jnp.where(low, sq, 0.0), axis=-1, keepdims=True) * (1.0 / HEAD_DIM)
    ms_hi = jnp.sum(jnp.where(low, 0.0, sq), axis=-1, keepdims=True) * (1.0 / HEAD_DIM)
    r = jnp.where(low, lax.rsqrt(ms_lo + EPS), lax.rsqrt(ms_hi + EPS))
    return x * r * g


def _attn_kernel(sinks_ref, q_ref, kc_ref, kp_ref, vc_ref, vp_ref, bias_ref, qg_ref, kg_ref, o_ref,
                 *, n_kv, group):
    n = pl.program_id(0)
    pair = 2 * HEAD_DIM
    n_heads = n_kv * group
    qi = lax.broadcasted_iota(jnp.int32, (QBLOCK, 2 * QBLOCK), 0)
    kj = lax.broadcasted_iota(jnp.int32, (QBLOCK, 2 * QBLOCK), 1)
    qc = qi // CHUNK
    kc = kj // CHUNK - QBLOCK // CHUNK
    visible = (kc <= qc) & (kc >= qc - WINDOW_CHUNKS) & ((n > 0) | (kj >= QBLOCK))
    low = lax.broadcasted_iota(jnp.int32, (1, pair), 1) < HEAD_DIM
    scale = HEAD_DIM ** -0.5

    q_all = jnp.concatenate([q_ref[:, c * pair:(c + 1) * pair] for c in range(n_heads // 2)], axis=0)
    q_all = _half_rms_norm(q_all, qg_ref[...], low) * scale
    k_all = jnp.concatenate([ref[:, c * pair:(c + 1) * pair]
                             for c in range(n_kv // 2) for ref in (kp_ref, kc_ref)], axis=0)
    k_all = _half_rms_norm(k_all, kg_ref[...], low).astype(BF16)

    logits, values = [], []
    heads_per_tile = 2 * group
    for hp in range(n_kv // 2):
        kcols = slice(hp * pair, (hp + 1) * pair)
        values.append(jnp.concatenate([vp_ref[:, kcols], vc_ref[:, kcols]], axis=0).astype(BF16))
        stacked = []
        for head in range(hp * heads_per_tile, (hp + 1) * heads_per_tile):
            qhalf, half = head % 2, (head // group) % 2
            q2 = q_all[(head // 2) * QBLOCK:(head // 2 + 1) * QBLOCK]
            qz = jnp.where(low == (qhalf == 0), q2, 0.0)
            if qhalf != half:
                qz = pltpu.roll(qz, HEAD_DIM, axis=1)
            stacked.append(qz.astype(BF16))
        k2 = k_all[hp * 2 * QBLOCK:(hp + 1) * 2 * QBLOCK]
        s = lax.dot_general(jnp.concatenate(stacked, axis=0), k2, (((1,), (1,)), ((), ())),
                            preferred_element_type=F32)
        for g in range(heads_per_tile):
            rows = s[g * QBLOCK:(g + 1) * QBLOCK]
            logits.append(jnp.where(visible, rows + bias_ref[hp * heads_per_tile + g], NEG_INF))

    s = jnp.concatenate(logits, axis=0)
    sink = jnp.concatenate([jnp.full((QBLOCK, 1), sinks_ref[h], F32) for h in range(n_heads)], axis=0)
    mx = jnp.maximum(jnp.max(s, axis=-1, keepdims=True), sink)
    e = jnp.exp(s - mx)
    denom = jnp.sum(e, axis=-1, keepdims=True) + jnp.exp(sink - mx)
    r = 1.0 / denom
    e = e.astype(BF16)

    for hp in range(n_kv // 2):
        rows = slice(hp * heads_per_tile * QBLOCK, (hp + 1) * heads_per_tile * QBLOCK)
        o = _dot(e[rows], values[hp]) * r[rows]
        for c in range(group):
            halves = []
            for qhalf in range(2):
                g = 2 * c + qhalf
                o2 = o[g * QBLOCK:(g + 1) * QBLOCK]
                if qhalf != (g // group) % 2:
                    o2 = pltpu.roll(o2, HEAD_DIM, axis=1)
                halves.append(o2)
            col = (hp * group + c) * pair
            o_ref[:, col:col + pair] = jnp.where(low, halves[0], halves[1]).astype(BF16)


def _attention(qkv, bias, sinks, q_gain, k_gain, *, n_heads, n_kv):
    m = qkv.shape[0]
    group = n_heads // n_kv
    dq = n_heads * HEAD_DIM
    dkv = n_kv * HEAD_DIM
    kblk = dq // dkv
    kernel = functools.partial(_attn_kernel, n_kv=n_kv, group=group)
    prev = lambda n: jnp.maximum(n - 1, 0)
    return pl.pallas_call(
        kernel,
        grid=(m // QBLOCK,),
        in_specs=[
            pl.BlockSpec(memory_space=pltpu.SMEM),
            pl.BlockSpec((QBLOCK, dq), lambda n: (n, 0)),
            pl.BlockSpec((QBLOCK, dkv), lambda n: (n, kblk)),
            pl.BlockSpec((QBLOCK, dkv), lambda n: (prev(n), kblk)),
            pl.BlockSpec((QBLOCK, dkv), lambda n: (n, kblk + 1)),
            pl.BlockSpec((QBLOCK, dkv), lambda n: (prev(n), kblk + 1)),
            pl.BlockSpec((n_heads, QBLOCK, 2 * QBLOCK), lambda n: (0, 0, 0)),
            pl.BlockSpec((1, 2 * HEAD_DIM), lambda n: (0, 0)),
            pl.BlockSpec((1, 2 * HEAD_DIM), lambda n: (0, 0)),
        ],
        out_specs=pl.BlockSpec((QBLOCK, dq), lambda n: (n, 0)),
        out_shape=jax.ShapeDtypeStruct((m, dq), BF16),
        compiler_params=_params("parallel"),
        name="band_attention",
    )(sinks, qkv, qkv, qkv, qkv, qkv, bias, q_gain, k_gain)


def _matmul_residual_body(a_ref, w_ref, x_ref, o_ref):
    o_ref[...] = x_ref[...] + _dot(a_ref[...], w_ref[...])


def _matmul_residual(a, w, x, *, tm, tn, casts):
    m, k = a.shape
    n = w.shape[0] * tn
    return _fused_call(
        _matmul_residual_body,
        name="attn_out_proj",
        grid=(n // tn, m // tm),
        j_outer=True,
        in_specs=[
            pl.BlockSpec((tm, k), lambda j, i: (i, 0)),
            pl.BlockSpec((None, k, tn), lambda j, i: (j, 0, 0)),
            pl.BlockSpec((tm, tn), lambda j, i: (i, j)),
        ],
        inputs=(a, w, x),
        out_spec=pl.BlockSpec((tm, tn), lambda j, i: (i, j)),
        out_shape=jax.ShapeDtypeStruct((m, n), F32),
        casts=casts,
    )


def _ffn_body(x_ref, g_ref, wg_ref, wu_ref, wd_ref, o_ref, h_ref):
    @pl.when(pl.program_id(1) == 0)
    def _():
        x = x_ref[...]
        h_ref[...] = _rms_norm(x, g_ref[...]).astype(BF16)
        o_ref[...] = x

    h = h_ref[...]
    gate = _dot(h, wg_ref[...])
    up = _dot(h, wu_ref[...])
    act = (gate * jax.nn.sigmoid(gate) * up).astype(BF16)
    o_ref[...] += _dot(act, wd_ref[...])


def _ffn(x, gain, w_gate, w_up, w_down, *, tm, tf, casts):
    m, d = x.shape
    f = w_gate.shape[0] * tf
    return _fused_call(
        _ffn_body,
        name="swiglu_ffn",
        grid=(m // tm, f // tf),
        j_outer=False,
        in_specs=[
            pl.BlockSpec((tm, d), lambda i, j: (i, 0)),
            pl.BlockSpec((1, d), lambda i, j: (0, 0)),
            pl.BlockSpec((None, d, tf), lambda i, j: (j, 0, 0)),
            pl.BlockSpec((None, d, tf), lambda i, j: (j, 0, 0)),
            pl.BlockSpec((None, tf, d), lambda i, j: (0, j, 0)),
        ],
        inputs=(x, gain, w_gate, w_up, w_down),
        out_spec=pl.BlockSpec((tm, d), lambda i, j: (i, 0)),
        out_shape=jax.ShapeDtypeStruct((m, d), F32),
        scratch=[pltpu.VMEM((tm, d), BF16)],
        casts=casts,
    )


def _ple_body(x_ref, p_ref, g_ref, wg_ref, bg_ref, wp_ref, o_ref, *, tn):
    x = x_ref[...]
    h = _rms_norm(x, g_ref[...]).astype(BF16)
    gate = jax.nn.sigmoid(_dot(h, wg_ref[...]) + bg_ref[...])
    col0 = pl.multiple_of(pl.program_id(0) * tn, tn)
    o_ref[...] = x_ref[:, pl.ds(col0, tn)] + gate * _dot(p_ref[...].astype(BF16), wp_ref[...])


def _ple(x, p, layer, gain, w_gate, b_gate, w_proj, *, tm, tn):
    m, d = x.shape
    pd = p.shape[-1]
    out, _ = _fused_call(
        functools.partial(_ple_body, tn=tn),
        name="gated_embedding",
        grid=(d // tn, m // tm),
        j_outer=True,
        in_specs=[
            pl.BlockSpec((tm, d), lambda j, i: (i, 0)),
            pl.BlockSpec((None, tm, pd), lambda j, i: (layer, i, 0)),
            pl.BlockSpec((1, d), lambda j, i: (0, 0)),
            pl.BlockSpec((None, d, tn), lambda j, i: (j, 0, 0)),
            pl.BlockSpec((1, tn), lambda j, i: (0, j)),
            pl.BlockSpec((None, pd, tn), lambda j, i: (j, 0, 0)),
        ],
        inputs=(x, p, gain, w_gate, b_gate, w_proj),
        out_spec=pl.BlockSpec((tm, tn), lambda j, i: (i, j)),
        out_shape=jax.ShapeDtypeStruct((m, d), F32),
    )
    return out


def kernel(x, p, norm_mix, norm_ffn, norm_ple, conv_w_in, conv_b_in, conv_w_dw, conv_b_dw, conv_ln_g,
           conv_ln_b, conv_w_out, conv_b_out, pool_w, pool_scale, attn_w_qkv, attn_q_norm, attn_k_norm,
           attn_sinks, attn_w_o, rel_bias, ffn_w_gate, ffn_w_up, ffn_w_down, ple_w_proj, ple_w_gate,
           ple_b_gate):
    batch, seq, d = x.shape
    depth = norm_mix.shape[0]
    n_heads = d // HEAD_DIM
    n_kv = (attn_w_qkv.shape[-1] // HEAD_DIM - n_heads) // 2
    group = n_heads // n_kv
    assert n_kv % 2 == 0 and group % 2 == 0, "attention kernel pairs heads into 128-lane tiles"
    assert d % (len(POOL_WINDOWS) * V7X_LANES) == 0
    m = batch * seq
    assert batch == 1 and m % QBLOCK == 0, "row tiles assume one sequence"

    tm = _tile(m, 512)
    tn = _tile(d, 1024)
    tf = _tile(ffn_w_gate.shape[-1], 512)
    tq = _tile(attn_w_qkv.shape[-1], 1024)
    tp = _tile(m, 256)
    row = lambda v: v.reshape(1, -1)

    pool_w2 = pool_w.reshape(pool_w.shape[0], -1, pool_w.shape[-1])

    def mixer_weights(i):
        kind, l = i % N_MIXERS, i // N_MIXERS
        if kind == 0:
            return {("conv_in", i): (conv_w_in, l, tn), ("conv_out", i): (conv_w_out, l, tn)}
        if kind == 1:
            return {("pool", i): (pool_w2, l, pool_w2.shape[-1])}
        return {("qkv", i): (attn_w_qkv, l, tq), ("attn_o", i): (attn_w_o, l, tn)}

    def ffn_weights(i):
        return {("gate", i): (ffn_w_gate, i, tf), ("up", i): (ffn_w_up, i, tf), ("down", i): (ffn_w_down, i, d)}

    def ple_weights(i):
        return {("ple_gate", i): (ple_w_gate, i, tn), ("ple_proj", i): (ple_w_proj, i, tn)}

    ready = {}

    def bf16(key, source):
        return ready.pop(key) if key in ready else _cast(*source)

    def hosting(weights, call):
        out, copies = call(list(weights.values()))
        ready.update(zip(weights.keys(), copies))
        return out

    xs = x.reshape(m, d)
    bias = None
    for i in range(depth):
        kind, l = i % N_MIXERS, i // N_MIXERS
        gain = row(norm_mix[i])
        mine = {} if ("gate", i) in ready else {**ffn_weights(i), **ple_weights(i)}
        if kind == 0:
            w_in, w_out = (bf16(k, w) for k, w in mixer_weights(i).items())
            first = {k: v for k, v in mine.items() if k[0] in ("gate", "up")}
            rest = {k: v for k, v in mine.items() if k not in first}
            u = hosting(first, lambda c: _conv_in(xs, gain, w_in, row(conv_b_in[l]), tm=tm, tn=tn, casts=c))
            xs = hosting(rest, lambda c: _conv_out(
                u, xs, conv_w_dw[l], row(conv_b_dw[l]), row(conv_ln_g[l]), row(conv_ln_b[l]), w_out,
                row(conv_b_out[l]), tm=tm, tn=tn, casts=c))
        elif kind == 1:
            (w_grp,) = (bf16(k, w) for k, w in mixer_weights(i).items())
            w_grp = w_grp.reshape(pool_w.shape[1:])
            xs = _pool(xs, gain, w_grp, row(pool_scale[l]), tm=tp)
        else:
            if bias is None:
                bias = _band_bias(rel_bias)
            w_qkv, w_o = (bf16(k, w) for k, w in mixer_weights(i).items())
            qkv = hosting(mine, lambda c: _norm_matmul(xs, gain, w_qkv, tm=tm, tn=tq, casts=c))
            o = _attention(qkv, bias, attn_sinks[l], row(jnp.tile(attn_q_norm[l], 2)),
                           row(jnp.tile(attn_k_norm[l], 2)), n_heads=n_heads, n_kv=n_kv)
            xs, _ = _matmul_residual(o, w_o, xs, tm=tm, tn=tn, casts=[])
        w_gate, w_up, w_down = (bf16(k, w) for k, w in ffn_weights(i).items())
        ahead = {}
        if i + 1 < depth:
            ahead = {**mixer_weights(i + 1), **ffn_weights(i + 1), **ple_weights(i + 1)}
        xs = hosting(ahead, lambda c: _ffn(xs, row(norm_ffn[i]), w_gate, w_up, w_down, tm=tm, tf=tf, casts=c))
        w_pg, w_pp = (bf16(k, w) for k, w in ple_weights(i).items())
        xs = _ple(xs, p.reshape(depth, m, -1), i, row(norm_ple[i]), w_pg, row(ple_b_gate[i]), w_pp,
                  tm=tm, tn=tn)
    return xs.reshape(batch, seq, d)
```

```python
import functools
import math

import jax
import jax.numpy as jnp
from jax import lax
from jax.experimental import pallas as pl
from jax.experimental.pallas import tpu as pltpu

N_MIXERS = 3
CHUNK = 64
CONV_WIDTH = 31
POOL_WINDOWS = (2, 4, 8, 16)
HEAD_DIM = 64
WINDOW_CHUNKS = 2
QBLOCK = 128
NUM_BUCKETS = 32
REL_MAX_DIST = 128
EPS = 1e-6
NEG_INF = -1e30

V7X_VMEM_BYTES = 64 * 1024 * 1024
V7X_LANES = 128
V7X_SUBLANES = 8
BF16_ROWS = 2 * V7X_SUBLANES
VMEM_LIMIT_BYTES = V7X_VMEM_BYTES * 7 // 8

CONV_HALO = 32
POOL_HALO = 16
CONV_ROWS = 64
CONV_STRIP = V7X_LANES
NORM_ROWS = 64

BF16 = jnp.bfloat16
F32 = jnp.float32


def _params(*semantics):
    return pltpu.CompilerParams(dimension_semantics=semantics, vmem_limit_bytes=VMEM_LIMIT_BYTES)


def _rms_norm(x, g):
    ms = jnp.mean(x * x, axis=-1, keepdims=True)
    return x * lax.rsqrt(ms + EPS) * g


def _dot(a, b):
    return jnp.dot(a, b, preferred_element_type=F32)


def _tile(n, target, multiple=V7X_LANES):
    if n <= target:
        return n
    t = target - target % multiple
    while n % t:
        t -= multiple
    return t


def _copy_tiles(src_ref, dst_ref):
    tile = dst_ref.shape[-1]
    for t in range(dst_ref.shape[0]):
        dst_ref[t] = src_ref[:, t * tile:(t + 1) * tile].astype(BF16)


def _cast(stack, layer, tile):
    _, r, c = stack.shape
    rb = _tile(r, 256, BF16_ROWS)
    return pl.pallas_call(
        _copy_tiles,
        grid=(r // rb,),
        in_specs=[pl.BlockSpec((None, rb, c), lambda i: (layer, i, 0))],
        out_specs=pl.BlockSpec((c // tile, rb, tile), lambda i: (0, i, 0)),
        out_shape=jax.ShapeDtypeStruct((c // tile, r, tile), BF16),
        compiler_params=_params("parallel"),
        name="weight_cast",
    )(stack)


def _cast_plan(sources, gi, gj, j_outer):
    in_specs, out_specs, out_shapes, row_only = [], [], [], []
    for stack, layer, tile in sources:
        _, r, c = stack.shape
        once = False
        if r % (gi * gj * BF16_ROWS) == 0:
            rows, cols = r // (gi * gj), c
            src_map = lambda i, j: (i * gj + j, 0)
            dst_map = lambda i, j: (0, i * gj + j, 0)
        elif r % (gi * BF16_ROWS) == 0 and c % gj == 0 and (c // gj) % tile == 0:
            rows, cols = r // gi, c // gj
            src_map = lambda i, j: (i, j)
            dst_map = lambda i, j: (j, i, 0)
        elif r % (gj * BF16_ROWS) == 0 and c % gi == 0 and (c // gi) % tile == 0:
            rows, cols = r // gj, c // gi
            src_map = lambda i, j: (j, i)
            dst_map = lambda i, j: (i, j, 0)
        elif r % (gi * BF16_ROWS) == 0 and not j_outer:
            rows, cols = r // gi, c
            src_map = lambda i, j: (i, 0)
            dst_map = lambda i, j: (0, i, 0)
            once = gj > 1
        else:
            n = max(k for k in range(1, gi * gj + 1) if r % (k * BF16_ROWS) == 0)
            step = (lambda i, j: j * gi + i) if j_outer else (lambda i, j: i * gj + j)
            rows, cols = r // n, c
            src_map = lambda i, j, n=n, step=step: (jnp.minimum(step(i, j), n - 1), 0)
            dst_map = lambda i, j, n=n, step=step: (0, jnp.minimum(step(i, j), n - 1), 0)
        if j_outer:
            src_map = (lambda f: lambda j, i: f(i, j))(src_map)
            dst_map = (lambda f: lambda j, i: f(i, j))(dst_map)
        src_map = (lambda f, l: lambda a, b: (l, *f(a, b)))(src_map, layer)
        in_specs.append(pl.BlockSpec((None, rows, cols), src_map))
        out_specs.append(pl.BlockSpec((cols // tile, rows, tile), dst_map))
        out_shapes.append(jax.ShapeDtypeStruct((c // tile, r, tile), BF16))
        row_only.append(once)
    return in_specs, out_specs, out_shapes, row_only


def _fused_call(body, *, name, grid, j_outer, in_specs, inputs, out_spec, out_shape, scratch=(), casts=()):
    n_in, n_cast = len(inputs), len(casts)
    gi, gj = (grid[1], grid[0]) if j_outer else grid
    cast_in, cast_out, cast_shapes, row_only = _cast_plan(casts, gi, gj, j_outer)

    def kernel(*refs):
        ins, rest = refs[:n_in], refs[n_in:]
        cast_src, rest = rest[:n_cast], rest[n_cast:]
        out, rest = rest[0], rest[1:]
        cast_dst, scr = rest[:n_cast], rest[n_cast:]
        for src, dst, once in zip(cast_src, cast_dst, row_only):
            if once:
                pl.when(pl.program_id(1) == 0)(functools.partial(_copy_tiles, src, dst))
            else:
                _copy_tiles(src, dst)
        body(*ins, out, *scr)

    res = pl.pallas_call(
        kernel,
        grid=grid,
        in_specs=[*in_specs, *cast_in],
        out_specs=[out_spec, *cast_out],
        out_shape=[out_shape, *cast_shapes],
        scratch_shapes=list(scratch),
        compiler_params=_params("parallel", "parallel" if j_outer else "arbitrary"),
        name=name,
    )(*inputs, *[stack for stack, _, _ in casts])
    return res[0], list(res[1:])


def _conv_in_body(x_ref, g_ref, wa_ref, wg_ref, ba_ref, bg_ref, u_ref):
    h = _rms_norm(x_ref[...], g_ref[...]).astype(BF16)
    a = _dot(h, wa_ref[...]) + ba_ref[...]
    gate = _dot(h, wg_ref[...]) + bg_ref[...]
    u_ref[...] = a * jax.nn.sigmoid(gate)


def _conv_in(x, gain, w_in, b_in, *, tm, tn, casts):
    m, d = x.shape
    nj = d // tn
    return _fused_call(
        _conv_in_body,
        name="conv_in",
        grid=(nj, m // tm),
        j_outer=True,
        in_specs=[
            pl.BlockSpec((tm, d), lambda j, i: (i, 0)),
            pl.BlockSpec((1, d), lambda j, i: (0, 0)),
            pl.BlockSpec((None, d, tn), lambda j, i: (j, 0, 0)),
            pl.BlockSpec((None, d, tn), lambda j, i: (j + nj, 0, 0)),
            pl.BlockSpec((1, tn), lambda j, i: (0, j)),
            pl.BlockSpec((1, tn), lambda j, i: (0, j + nj)),
        ],
        inputs=(x, gain, w_in, w_in, b_in, b_in),
        out_spec=pl.BlockSpec((tm, tn), lambda j, i: (i, j)),
        out_shape=jax.ShapeDtypeStruct((m, d), F32),
        casts=casts,
    )


def _conv_out_body(u_ref, uprev_ref, wdw_ref, bdw_ref, lng_ref, lnb_ref, wout_ref, bout_ref, x_ref,
                   o_ref, ext_ref, conv_ref, v_ref, *, tm):
    i = pl.program_id(0)
    j = pl.program_id(1)
    d = u_ref.shape[1]
    first = CONV_HALO - (CONV_WIDTH - 1)

    @pl.when(j == 0)
    def _():
        for s in range(d // CONV_STRIP):
            cols = slice(s * CONV_STRIP, (s + 1) * CONV_STRIP)
            ext_ref[s, 0:CONV_HALO, :] = jnp.where(i == 0, 0.0, uprev_ref[:, cols])
            ext_ref[s, CONV_HALO:, :] = u_ref[:, cols]
            taps = [jnp.broadcast_to(wdw_ref[k:k + 1, cols], (CONV_ROWS, CONV_STRIP))
                    for k in range(CONV_WIDTH)]
            bias = jnp.broadcast_to(bdw_ref[:, cols], (CONV_ROWS, CONV_STRIP))

            def conv_rows(c, carry, s=s, cols=cols, taps=taps, bias=bias):
                r0 = pl.multiple_of(c * CONV_ROWS, CONV_ROWS)
                acc = bias
                for k in range(CONV_WIDTH):
                    acc = acc + taps[k] * ext_ref[s, pl.ds(r0 + first + k, CONV_ROWS), :]
                conv_ref[pl.ds(r0, CONV_ROWS), cols] = acc
                return carry

            lax.fori_loop(0, tm // CONV_ROWS, conv_rows, 0)

        def norm_rows(c, carry):
            r0 = pl.multiple_of(c * NORM_ROWS, NORM_ROWS)
            conv = conv_ref[pl.ds(r0, NORM_ROWS), :]
            mu = jnp.mean(conv, axis=-1, keepdims=True)
            xc = conv - mu
            y = xc * lax.rsqrt(jnp.mean(xc * xc, axis=-1, keepdims=True) + EPS)
            y = y * lng_ref[...] + lnb_ref[...]
            v_ref[pl.ds(r0, NORM_ROWS), :] = (y * jax.nn.sigmoid(y)).astype(BF16)
            return carry

        lax.fori_loop(0, tm // NORM_ROWS, norm_rows, 0)

    o_ref[...] = x_ref[...] + _dot(v_ref[...], wout_ref[...]) + bout_ref[...]


def _conv_out(u, x, w_dw, b_dw, ln_g, ln_b, w_out, b_out, *, tm, tn, casts):
    m, d = u.shape
    hb = tm // CONV_HALO
    return _fused_call(
        functools.partial(_conv_out_body, tm=tm),
        name="conv_out",
        grid=(m // tm, d // tn),
        j_outer=False,
        in_specs=[
            pl.BlockSpec((tm, d), lambda i, j: (i, 0)),
            pl.BlockSpec((CONV_HALO, d), lambda i, j: (jnp.maximum(i * hb - 1, 0), 0)),
            pl.BlockSpec((CONV_WIDTH, d), lambda i, j: (0, 0)),
            pl.BlockSpec((1, d), lambda i, j: (0, 0)),
            pl.BlockSpec((1, d), lambda i, j: (0, 0)),
            pl.BlockSpec((1, d), lambda i, j: (0, 0)),
            pl.BlockSpec((None, d, tn), lambda i, j: (j, 0, 0)),
            pl.BlockSpec((1, tn), lambda i, j: (0, j)),
            pl.BlockSpec((tm, tn), lambda i, j: (i, j)),
        ],
        inputs=(u, u, w_dw, b_dw, ln_g, ln_b, w_out, b_out, x),
        out_spec=pl.BlockSpec((tm, tn), lambda i, j: (i, j)),
        out_shape=jax.ShapeDtypeStruct((m, d), F32),
        scratch=[
            pltpu.VMEM((d // CONV_STRIP, tm + CONV_HALO, CONV_STRIP), F32),
            pltpu.VMEM((tm, d), F32),
            pltpu.VMEM((tm, d), BF16),
        ],
        casts=casts,
    )


def _pool_kernel(x_ref, xprev_ref, g_ref, w_ref, scale_ref, o_ref, ext_ref, *, tm):
    i = pl.program_id(0)
    d = x_ref.shape[1]
    gc = d // len(POOL_WINDOWS)
    x = x_ref[...]
    h = _rms_norm(x, g_ref[...])
    ext_ref[0:POOL_HALO, :] = jnp.where(i == 0, 0.0, _rms_norm(xprev_ref[...], g_ref[...]))
    ext_ref[POOL_HALO:, :] = h
    t = i * tm + lax.broadcasted_iota(jnp.int32, (tm, 1), 0)
    for g, w in enumerate(POOL_WINDOWS):
        cols = slice(g * gc, (g + 1) * gc)
        s = h[:, cols]
        for k in range(1, w):
            s = s + ext_ref[POOL_HALO - k:POOL_HALO - k + tm, cols]
        cnt = jnp.minimum(t + 1, w).astype(F32)
        mix = (s / cnt - h[:, cols]).astype(BF16)
        y = _dot(mix, w_ref[g]) * scale_ref[:, cols]
        o_ref[:, cols] = x[:, cols] + y


def _pool(x, gain, w_grp, scale, *, tm):
    m, d = x.shape
    ng, gc, _ = w_grp.shape
    hb = tm // POOL_HALO
    kernel = functools.partial(_pool_kernel, tm=tm)
    return pl.pallas_call(
        kernel,
        grid=(m // tm,),
        in_specs=[
            pl.BlockSpec((tm, d), lambda i: (i, 0)),
            pl.BlockSpec((POOL_HALO, d), lambda i: (jnp.maximum(i * hb - 1, 0), 0)),
            pl.BlockSpec((1, d), lambda i: (0, 0)),
            pl.BlockSpec((ng, gc, gc), lambda i: (0, 0, 0)),
            pl.BlockSpec((1, d), lambda i: (0, 0)),
        ],
        out_specs=pl.BlockSpec((tm, d), lambda i: (i, 0)),
        out_shape=jax.ShapeDtypeStruct((m, d), F32),
        scratch_shapes=[pltpu.VMEM((tm + POOL_HALO, d), F32)],
        compiler_params=_params("parallel"),
        name="pool",
    )(x, x, gain, w_grp, scale)


def _norm_matmul_body(x_ref, g_ref, w_ref, o_ref):
    o_ref[...] = _dot(_rms_norm(x_ref[...], g_ref[...]).astype(BF16), w_ref[...])


def _norm_matmul(x, gain, w, *, tm, tn, casts):
    m, d = x.shape
    n = w.shape[0] * tn
    return _fused_call(
        _norm_matmul_body,
        name="qkv_proj",
        grid=(n // tn, m // tm),
        j_outer=True,
        in_specs=[
            pl.BlockSpec((tm, d), lambda j, i: (i, 0)),
            pl.BlockSpec((1, d), lambda j, i: (0, 0)),
            pl.BlockSpec((None, d, tn), lambda j, i: (j, 0, 0)),
        ],
        inputs=(x, gain, w),
        out_spec=pl.BlockSpec((tm, tn), lambda j, i: (i, j)),
        out_shape=jax.ShapeDtypeStruct((m, n), F32),
        casts=casts,
    )


def _t5_bucket(rel):
    nb = NUM_BUCKETS // 2
    n = -rel
    ret = jnp.where(n < 0, nb, 0)
    n = jnp.abs(n)
    max_exact = nb // 2
    nf = jnp.maximum(n, 1).astype(F32)
    large = max_exact + (jnp.log(nf / max_exact) / math.log(REL_MAX_DIST / max_exact)
                         * (nb - max_exact)).astype(jnp.int32)
    large = jnp.minimum(large, nb - 1)
    return ret + jnp.where(n < max_exact, n, large)


def _band_bias_kernel(bucket_ref, rel_bias_ref, o_ref):
    h = pl.program_id(0)
    bucket = bucket_ref[...]
    acc = jnp.zeros(bucket.shape, F32)
    for b in range(NUM_BUCKETS):
        acc = jnp.where(bucket == b, rel_bias_ref[b, h], acc)
    o_ref[0] = acc


def _band_bias(rel_bias):
    n_heads = rel_bias.shape[1]
    i = jnp.arange(QBLOCK)[:, None]
    j = jnp.arange(2 * QBLOCK)[None, :]
    bucket = _t5_bucket(j - QBLOCK - i).astype(jnp.int32)
    return pl.pallas_call(
        _band_bias_kernel,
        grid=(n_heads,),
        in_specs=[
            pl.BlockSpec((QBLOCK, 2 * QBLOCK), lambda h: (0, 0)),
            pl.BlockSpec(memory_space=pltpu.SMEM),
        ],
        out_specs=pl.BlockSpec((1, QBLOCK, 2 * QBLOCK), lambda h: (h, 0, 0)),
        out_shape=jax.ShapeDtypeStruct((n_heads, QBLOCK, 2 * QBLOCK), F32),
        compiler_params=_params("parallel"),
        name="band_bias",
    )(bucket, rel_bias)


def _half_rms_norm(x, g, low):
    sq = x * x
    ms_lo = jnp.sum(jnp.where(low, sq, 0.0), axis=-1, keepdims=True) * (1.0 / HEAD_DIM)
    ms_hi = jnp.sum(jnp.where(low, 0.0, sq), axis=-1, keepdims=True) * (1.0 / HEAD_DIM)
    r = jnp.where(low, lax.rsqrt(ms_lo + EPS), lax.rsqrt(ms_hi + EPS))
    return x * r * g


def _attn_kernel(sinks_ref, q_ref, kc_ref, kp_ref, vc_ref, vp_ref, bias_ref, qg_ref, kg_ref, o_ref,
                 *, n_kv, group):
    n = pl.program_id(0)
    pair = 2 * HEAD_DIM
    n_heads = n_kv * group
    qi = lax.broadcasted_iota(jnp.int32, (QBLOCK, 2 * QBLOCK), 0)
    kj = lax.broadcasted_iota(jnp.int32, (QBLOCK, 2 * QBLOCK), 1)
    qc = qi // CHUNK
    kc = kj // CHUNK - QBLOCK // CHUNK
    visible = (kc <= qc) & (kc >= qc - WINDOW_CHUNKS) & ((n > 0) | (kj >= QBLOCK))
    low = lax.broadcasted_iota(jnp.int32, (1, pair), 1) < HEAD_DIM
    scale = HEAD_DIM ** -0.5

    q_all = jnp.concatenate([q_ref[:, c * pair:(c + 1) * pair] for c in range(n_heads // 2)], axis=0)
    q_all = _half_rms_norm(q_all, qg_ref[...], low) * scale
    k_all = jnp.concatenate([ref[:, c * pair:(c + 1) * pair]
                             for c in range(n_kv // 2) for ref in (kp_ref, kc_ref)], axis=0)
    k_all = _half_rms_norm(k_all, kg_ref[...], low).astype(BF16)

    logits, values = [], []
    heads_per_tile = 2 * group
    for hp in range(n_kv // 2):
        kcols = slice(hp * pair, (hp + 1) * pair)
        values.append(jnp.concatenate([vp_ref[:, kcols], vc_ref[:, kcols]], axis=0).astype(BF16))
        stacked = []
        for head in range(hp * heads_per_tile, (hp + 1) * heads_per_tile):
            qhalf, half = head % 2, (head // group) % 2
            q2 = q_all[(head // 2) * QBLOCK:(head // 2 + 1) * QBLOCK]
            qz = jnp.where(low == (qhalf == 0), q2, 0.0)
            if qhalf != half:
                qz = pltpu.roll(qz, HEAD_DIM, axis=1)
            stacked.append(qz.astype(BF16))
        k2 = k_all[hp * 2 * QBLOCK:(hp + 1) * 2 * QBLOCK]
        s = lax.dot_general(jnp.concatenate(stacked, axis=0), k2, (((1,), (1,)), ((), ())),
                            preferred_element_type=F32)
        for g in range(heads_per_tile):
            rows = s[g * QBLOCK:(g + 1) * QBLOCK]
            logits.append(jnp.where(visible, rows + bias_ref[hp * heads_per_tile + g], NEG_INF))

    s = jnp.concatenate(logits, axis=0)
    sink = jnp.concatenate([jnp.full((QBLOCK, 1), sinks_ref[h], F32) for h in range(n_heads)], axis=0)
    mx = jnp.maximum(jnp.max(s, axis=-1, keepdims=True), sink)
    e = jnp.exp(s - mx)
    denom = jnp.sum(e, axis=-1, keepdims=True) + jnp.exp(sink - mx)
    r = 1.0 / denom
    e = e.astype(BF16)

    for hp in range(n_kv // 2):
        rows = slice(hp * heads_per_tile * QBLOCK, (hp + 1) * heads_per_tile * QBLOCK)
        o = _dot(e[rows], values[hp]) * r[rows]
        for c in range(group):
            halves = []
            for qhalf in range(2):
                g = 2 * c + qhalf
                o2 = o[g * QBLOCK:(g + 1) * QBLOCK]
                if qhalf != (g // group) % 2:
                    o2 = pltpu.roll(o2, HEAD_DIM, axis=1)
                halves.append(o2)
            col = (hp * group + c) * pair
            o_ref[:, col:col + pair] = jnp.where(low, halves[0], halves[1]).astype(BF16)


def _attention(qkv, bias, sinks, q_gain, k_gain, *, n_heads, n_kv):
    m = qkv.shape[0]
    group = n_heads // n_kv
    dq = n_heads * HEAD_DIM
    dkv = n_kv * HEAD_DIM
    kblk = dq // dkv
    kernel = functools.partial(_attn_kernel, n_kv=n_kv, group=group)
    prev = lambda n: jnp.maximum(n - 1, 0)
    return pl.pallas_call(
        kernel,
        grid=(m // QBLOCK,),
        in_specs=[
            pl.BlockSpec(memory_space=pltpu.SMEM),
            pl.BlockSpec((QBLOCK, dq), lambda n: (n, 0)),
            pl.BlockSpec((QBLOCK, dkv), lambda n: (n, kblk)),
            pl.BlockSpec((QBLOCK, dkv), lambda n: (prev(n), kblk)),
            pl.BlockSpec((QBLOCK, dkv), lambda n: (n, kblk + 1)),
            pl.BlockSpec((QBLOCK, dkv), lambda n: (prev(n), kblk + 1)),
            pl.BlockSpec((n_heads, QBLOCK, 2 * QBLOCK), lambda n: (0, 0, 0)),
            pl.BlockSpec((1, 2 * HEAD_DIM), lambda n: (0, 0)),
            pl.BlockSpec((1, 2 * HEAD_DIM), lambda n: (0, 0)),
        ],
        out_specs=pl.BlockSpec((QBLOCK, dq), lambda n: (n, 0)),
        out_shape=jax.ShapeDtypeStruct((m, dq), BF16),
        compiler_params=_params("parallel"),
        name="band_attention",
    )(sinks, qkv, qkv, qkv, qkv, qkv, bias, q_gain, k_gain)


def _matmul_residual_body(a_ref, w_ref, x_ref, o_ref):
    o_ref[...] = x_ref[...] + _dot(a_ref[...], w_ref[...])


def _matmul_residual(a, w, x, *, tm, tn, casts):
    m, k = a.shape
    n = w.shape[0] * tn
    return _fused_call(
        _matmul_residual_body,
        name="attn_out_proj",
        grid=(n // tn, m // tm),
        j_outer=True,
        in_specs=[
            pl.BlockSpec((tm, k), lambda j, i: (i, 0)),
            pl.BlockSpec((None, k, tn), lambda j, i: (j, 0, 0)),
            pl.BlockSpec((tm, tn), lambda j, i: (i, j)),
        ],
        inputs=(a, w, x),
        out_spec=pl.BlockSpec((tm, tn), lambda j, i: (i, j)),
        out_shape=jax.ShapeDtypeStruct((m, n), F32),
        casts=casts,
    )


def _ffn_body(x_ref, g_ref, wg_ref, wu_ref, wd_ref, p_ref, pg_ref, wpg_ref, bpg_ref, wpp_ref, o_ref, h_ref):
    j = pl.program_id(1)

    @pl.when(j == 0)
    def _():
        x = x_ref[...]
        h_ref[...] = _rms_norm(x, g_ref[...]).astype(BF16)
        o_ref[...] = x

    h = h_ref[...]
    gate = _dot(h, wg_ref[...])
    up = _dot(h, wu_ref[...])
    act = (gate * jax.nn.sigmoid(gate) * up).astype(BF16)
    o_ref[...] += _dot(act, wd_ref[...])

    @pl.when(j == pl.num_programs(1) - 1)
    def _():
        x1 = o_ref[...]
        h1 = _rms_norm(x1, pg_ref[...]).astype(BF16)
        pb = p_ref[...].astype(BF16)
        tn = wpg_ref.shape[-1]
        for t in range(wpg_ref.shape[0]):
            cols = slice(t * tn, (t + 1) * tn)
            pgate = jax.nn.sigmoid(_dot(h1, wpg_ref[t]) + bpg_ref[:, cols])
            o_ref[:, cols] = x1[:, cols] + pgate * _dot(pb, wpp_ref[t])


def _ffn(x, gain, w_gate, w_up, w_down, p, layer, ple_gain, w_pgate, b_pgate, w_pproj, *, tm, tf, casts):
    m, d = x.shape
    f = w_gate.shape[0] * tf
    pd = p.shape[-1]
    resident = lambda a: pl.BlockSpec(a.shape, lambda i, j: (0,) * a.ndim, pipeline_mode=pl.Buffered(1))
    return _fused_call(
        _ffn_body,
        name="swiglu_ffn_gated_embedding",
        grid=(m // tm, f // tf),
        j_outer=False,
        in_specs=[
            pl.BlockSpec((tm, d), lambda i, j: (i, 0)),
            pl.BlockSpec((1, d), lambda i, j: (0, 0)),
            pl.BlockSpec((None, d, tf), lambda i, j: (j, 0, 0)),
            pl.BlockSpec((None, d, tf), lambda i, j: (j, 0, 0)),
            pl.BlockSpec((None, tf, d), lambda i, j: (0, j, 0)),
            pl.BlockSpec((None, tm, pd), lambda i, j: (layer, i, 0)),
            pl.BlockSpec((1, d), lambda i, j: (0, 0)),
            resident(w_pgate),
            pl.BlockSpec((1, d), lambda i, j: (0, 0)),
            resident(w_pproj),
        ],
        inputs=(x, gain, w_gate, w_up, w_down, p, ple_gain, w_pgate, b_pgate, w_pproj),
        out_spec=pl.BlockSpec((tm, d), lambda i, j: (i, 0)),
        out_shape=jax.ShapeDtypeStruct((m, d), F32),
        scratch=[pltpu.VMEM((tm, d), BF16)],
        casts=casts,
    )


def kernel(x, p, norm_mix, norm_ffn, norm_ple, conv_w_in, conv_b_in, conv_w_dw, conv_b_dw, conv_ln_g,
           conv_ln_b, conv_w_out, conv_b_out, pool_w, pool_scale, attn_w_qkv, attn_q_norm, attn_k_norm,
           attn_sinks, attn_w_o, rel_bias, ffn_w_gate, ffn_w_up, ffn_w_down, ple_w_proj, ple_w_gate,
           ple_b_gate):
    batch, seq, d = x.shape
    depth = norm_mix.shape[0]
    n_heads = d // HEAD_DIM
    n_kv = (attn_w_qkv.shape[-1] // HEAD_DIM - n_heads) // 2
    group = n_heads // n_kv
    assert n_kv % 2 == 0 and group % 2 == 0, "attention kernel pairs heads into 128-lane tiles"
    assert d % (len(POOL_WINDOWS) * V7X_LANES) == 0
    m = batch * seq
    assert batch == 1 and m % QBLOCK == 0, "row tiles assume one sequence"

    tm = _tile(m, 512)
    tn = _tile(d, 1024)
    tf = _tile(ffn_w_gate.shape[-1], 512)
    tq = _tile(attn_w_qkv.shape[-1], 1024)
    tp = _tile(m, 256)
    row = lambda v: v.reshape(1, -1)

    pool_w2 = pool_w.reshape(pool_w.shape[0], -1, pool_w.shape[-1])

    def mixer_weights(i):
        kind, l = i % N_MIXERS, i // N_MIXERS
        if kind == 0:
            return {("conv_in", i): (conv_w_in, l, tn), ("conv_out", i): (conv_w_out, l, tn)}
        if kind == 1:
            return {("pool", i): (pool_w2, l, pool_w2.shape[-1])}
        return {("qkv", i): (attn_w_qkv, l, tq), ("attn_o", i): (attn_w_o, l, tn)}

    def ffn_weights(i):
        return {("gate", i): (ffn_w_gate, i, tf), ("up", i): (ffn_w_up, i, tf), ("down", i): (ffn_w_down, i, d)}

    def ple_weights(i):
        return {("ple_gate", i): (ple_w_gate, i, tn), ("ple_proj", i): (ple_w_proj, i, tn)}

    ready = {}

    def bf16(key, source):
        return ready.pop(key) if key in ready else _cast(*source)

    def hosting(weights, call):
        out, copies = call(list(weights.values()))
        ready.update(zip(weights.keys(), copies))
        return out

    xs = x.reshape(m, d)
    bias = None
    for i in range(depth):
        kind, l = i % N_MIXERS, i // N_MIXERS
        gain = row(norm_mix[i])
        mine = {} if ("gate", i) in ready else {**ffn_weights(i), **ple_weights(i)}
        first = {k: v for k, v in mine.items() if k[0] in ("gate", "up")}
        rest = {k: v for k, v in mine.items() if k not in first}
        if kind == 0:
            w_in, w_out = (bf16(k, w) for k, w in mixer_weights(i).items())
            u = hosting(first, lambda c: _conv_in(xs, gain, w_in, row(conv_b_in[l]), tm=tm, tn=tn, casts=c))
            xs = hosting(rest, lambda c: _conv_out(
                u, xs, conv_w_dw[l], row(conv_b_dw[l]), row(conv_ln_g[l]), row(conv_ln_b[l]), w_out,
                row(conv_b_out[l]), tm=tm, tn=tn, casts=c))
        elif kind == 1:
            (w_grp,) = (bf16(k, w) for k, w in mixer_weights(i).items())
            w_grp = w_grp.reshape(pool_w.shape[1:])
            xs = _pool(xs, gain, w_grp, row(pool_scale[l]), tm=tp)
        else:
            if bias is None:
                bias = _band_bias(rel_bias)
            w_qkv, w_o = (bf16(k, w) for k, w in mixer_weights(i).items())
            qkv = hosting(first, lambda c: _norm_matmul(xs, gain, w_qkv, tm=tm, tn=tq, casts=c))
            o = _attention(qkv, bias, attn_sinks[l], row(jnp.tile(attn_q_norm[l], 2)),
                           row(jnp.tile(attn_k_norm[l], 2)), n_heads=n_heads, n_kv=n_kv)
            xs = hosting(rest, lambda c: _matmul_residual(o, w_o, xs, tm=tm, tn=tn, casts=c))
        w_gate, w_up, w_down = (bf16(k, w) for k, w in ffn_weights(i).items())
        w_pg, w_pp = (bf16(k, w) for k, w in ple_weights(i).items())
        ahead = {}
        if i + 1 < depth:
            ahead = mixer_weights(i + 1)
            if (i + 1) % N_MIXERS == 1:
                ahead = {**ahead, **ffn_weights(i + 1), **ple_weights(i + 1)}
        xs = hosting(ahead, lambda c: _ffn(
            xs, row(norm_ffn[i]), w_gate, w_up, w_down, p.reshape(depth, m, -1), i, row(norm_ple[i]),
            w_pg, row(ple_b_gate[i]), w_pp, tm=tm, tf=tf, casts=c))
    return xs.reshape(batch, seq, d)
```

```python
import functools
import math

import jax
import jax.numpy as jnp
from jax import lax
from jax.experimental import pallas as pl
from jax.experimental.pallas import tpu as pltpu

N_MIXERS = 3
CHUNK = 64
CONV_WIDTH = 31
POOL_WINDOWS = (2, 4, 8, 16)
HEAD_DIM = 64
WINDOW_CHUNKS = 2
QBLOCK = 128
NUM_BUCKETS = 32
REL_MAX_DIST = 128
EPS = 1e-6
NEG_INF = -1e30

V7X_VMEM_BYTES = 64 * 1024 * 1024
V7X_LANES = 128
V7X_SUBLANES = 8
BF16_ROWS = 2 * V7X_SUBLANES
VMEM_LIMIT_BYTES = V7X_VMEM_BYTES * 7 // 8

CONV_HALO = 32
POOL_HALO = 16
CONV_ROWS = 64
CONV_STRIP = V7X_LANES
NORM_ROWS = 64

BF16 = jnp.bfloat16
F32 = jnp.float32


def _params(*semantics):
    return pltpu.CompilerParams(dimension_semantics=semantics, vmem_limit_bytes=VMEM_LIMIT_BYTES)


def _rms_norm(x, g):
    ms = jnp.mean(x * x, axis=-1, keepdims=True)
    return x * lax.rsqrt(ms + EPS) * g


def _dot(a, b):
    return jnp.dot(a, b, preferred_element_type=F32)


def _tile(n, target, multiple=V7X_LANES):
    if n <= target:
        return n
    t = target - target % multiple
    while n % t:
        t -= multiple
    return t


def _copy_tiles(src_ref, dst_ref):
    tile = dst_ref.shape[-1]
    for t in range(dst_ref.shape[0]):
        dst_ref[t] = src_ref[:, t * tile:(t + 1) * tile].astype(BF16)


def _cast(stack, layer, tile):
    _, r, c = stack.shape
    rb = _tile(r, 256, BF16_ROWS)
    return pl.pallas_call(
        _copy_tiles,
        grid=(r // rb,),
        in_specs=[pl.BlockSpec((None, rb, c), lambda i: (layer, i, 0))],
        out_specs=pl.BlockSpec((c // tile, rb, tile), lambda i: (0, i, 0)),
        out_shape=jax.ShapeDtypeStruct((c // tile, r, tile), BF16),
        compiler_params=_params("parallel"),
        name="weight_cast",
    )(stack)


def _cast_plan(sources, gi, gj, j_outer):
    in_specs, out_specs, out_shapes, row_only = [], [], [], []
    for stack, layer, tile in sources:
        _, r, c = stack.shape
        once = False
        if r % (gi * gj * BF16_ROWS) == 0:
            rows, cols = r // (gi * gj), c
            src_map = lambda i, j: (i * gj + j, 0)
            dst_map = lambda i, j: (0, i * gj + j, 0)
        elif r % (gi * BF16_ROWS) == 0 and c % gj == 0 and (c // gj) % tile == 0:
            rows, cols = r // gi, c // gj
            src_map = lambda i, j: (i, j)
            dst_map = lambda i, j: (j, i, 0)
        elif r % (gj * BF16_ROWS) == 0 and c % gi == 0 and (c // gi) % tile == 0:
            rows, cols = r // gj, c // gi
            src_map = lambda i, j: (j, i)
            dst_map = lambda i, j: (i, j, 0)
        elif r % (gi * BF16_ROWS) == 0 and not j_outer:
            rows, cols = r // gi, c
            src_map = lambda i, j: (i, 0)
            dst_map = lambda i, j: (0, i, 0)
            once = gj > 1
        else:
            n = max(k for k in range(1, gi * gj + 1) if r % (k * BF16_ROWS) == 0)
            step = (lambda i, j: j * gi + i) if j_outer else (lambda i, j: i * gj + j)
            rows, cols = r // n, c
            src_map = lambda i, j, n=n, step=step: (jnp.minimum(step(i, j), n - 1), 0)
            dst_map = lambda i, j, n=n, step=step: (0, jnp.minimum(step(i, j), n - 1), 0)
        if j_outer:
            src_map = (lambda f: lambda j, i: f(i, j))(src_map)
            dst_map = (lambda f: lambda j, i: f(i, j))(dst_map)
        src_map = (lambda f, l: lambda a, b: (l, *f(a, b)))(src_map, layer)
        in_specs.append(pl.BlockSpec((None, rows, cols), src_map))
        out_specs.append(pl.BlockSpec((cols // tile, rows, tile), dst_map))
        out_shapes.append(jax.ShapeDtypeStruct((c // tile, r, tile), BF16))
        row_only.append(once)
    return in_specs, out_specs, out_shapes, row_only


def _fused_call(body, *, name, grid, j_outer, in_specs, inputs, out_spec, out_shape, scratch=(), casts=()):
    n_in, n_cast = len(inputs), len(casts)
    gi, gj = (grid[1], grid[0]) if j_outer else grid
    cast_in, cast_out, cast_shapes, row_only = _cast_plan(casts, gi, gj, j_outer)

    def kernel(*refs):
        ins, rest = refs[:n_in], refs[n_in:]
        cast_src, rest = rest[:n_cast], rest[n_cast:]
        out, rest = rest[0], rest[1:]
        cast_dst, scr = rest[:n_cast], rest[n_cast:]
        for src, dst, once in zip(cast_src, cast_dst, row_only):
            if once:
                pl.when(pl.program_id(1) == 0)(functools.partial(_copy_tiles, src, dst))
            else:
                _copy_tiles(src, dst)
        body(*ins, out, *scr)

    res = pl.pallas_call(
        kernel,
        grid=grid,
        in_specs=[*in_specs, *cast_in],
        out_specs=[out_spec, *cast_out],
        out_shape=[out_shape, *cast_shapes],
        scratch_shapes=list(scratch),
        compiler_params=_params("parallel", "parallel" if j_outer else "arbitrary"),
        name=name,
    )(*inputs, *[stack for stack, _, _ in casts])
    return res[0], list(res[1:])


def _conv_in_body(x_ref, g_ref, wa_ref, wg_ref, ba_ref, bg_ref, u_ref):
    h = _rms_norm(x_ref[...], g_ref[...]).astype(BF16)
    a = _dot(h, wa_ref[...]) + ba_ref[...]
    gate = _dot(h, wg_ref[...]) + bg_ref[...]
    u_ref[...] = a * jax.nn.sigmoid(gate)


def _conv_in(x, gain, w_in, b_in, *, tm, tn, casts):
    m, d = x.shape
    nj = d // tn
    return _fused_call(
        _conv_in_body,
        name="conv_in",
        grid=(nj, m // tm),
        j_outer=True,
        in_specs=[
            pl.BlockSpec((tm, d), lambda j, i: (i, 0)),
            pl.BlockSpec((1, d), lambda j, i: (0, 0)),
            pl.BlockSpec((None, d, tn), lambda j, i: (j, 0, 0)),
            pl.BlockSpec((None, d, tn), lambda j, i: (j + nj, 0, 0)),
            pl.BlockSpec((1, tn), lambda j, i: (0, j)),
            pl.BlockSpec((1, tn), lambda j, i: (0, j + nj)),
        ],
        inputs=(x, gain, w_in, w_in, b_in, b_in),
        out_spec=pl.BlockSpec((tm, tn), lambda j, i: (i, j)),
        out_shape=jax.ShapeDtypeStruct((m, d), F32),
        casts=casts,
    )


def _conv_out_kernel(u_ref, uprev_ref, wdw_ref, bdw_ref, lng_ref, lnb_ref, wout_ref, bout_ref, x_ref,
                     o_ref, ext_ref, conv_ref, v_ref, *, tm):
    i = pl.program_id(0)
    d = u_ref.shape[1]
    first = CONV_HALO - (CONV_WIDTH - 1)

    for s in range(d // CONV_STRIP):
        cols = slice(s * CONV_STRIP, (s + 1) * CONV_STRIP)
        ext_ref[s, 0:CONV_HALO, :] = jnp.where(i == 0, 0.0, uprev_ref[:, cols])
        ext_ref[s, CONV_HALO:, :] = u_ref[:, cols]
        taps = [jnp.broadcast_to(wdw_ref[k:k + 1, cols], (CONV_ROWS, CONV_STRIP))
                for k in range(CONV_WIDTH)]
        bias = jnp.broadcast_to(bdw_ref[:, cols], (CONV_ROWS, CONV_STRIP))

        def conv_rows(c, carry, s=s, cols=cols, taps=taps, bias=bias):
            r0 = pl.multiple_of(c * CONV_ROWS, CONV_ROWS)
            acc = bias
            for k in range(CONV_WIDTH):
                acc = acc + taps[k] * ext_ref[s, pl.ds(r0 + first + k, CONV_ROWS), :]
            conv_ref[pl.ds(r0, CONV_ROWS), cols] = acc
            return carry

        lax.fori_loop(0, tm // CONV_ROWS, conv_rows, 0)

    def norm_rows(c, carry):
        r0 = pl.multiple_of(c * NORM_ROWS, NORM_ROWS)
        conv = conv_ref[pl.ds(r0, NORM_ROWS), :]
        mu = jnp.mean(conv, axis=-1, keepdims=True)
        xc = conv - mu
        y = xc * lax.rsqrt(jnp.mean(xc * xc, axis=-1, keepdims=True) + EPS)
        y = y * lng_ref[...] + lnb_ref[...]
        v_ref[pl.ds(r0, NORM_ROWS), :] = (y * jax.nn.sigmoid(y)).astype(BF16)
        return carry

    lax.fori_loop(0, tm // NORM_ROWS, norm_rows, 0)

    v = v_ref[...]
    tn = wout_ref.shape[-1]
    for t in range(wout_ref.shape[0]):
        cols = slice(t * tn, (t + 1) * tn)
        o_ref[:, cols] = x_ref[:, cols] + _dot(v, wout_ref[t]) + bout_ref[:, cols]


def _conv_out(u, x, w_dw, b_dw, ln_g, ln_b, w_out, b_out, *, tm):
    m, d = u.shape
    hb = tm // CONV_HALO
    const = lambda i: (0, 0)
    return pl.pallas_call(
        functools.partial(_conv_out_kernel, tm=tm),
        grid=(m // tm,),
        in_specs=[
            pl.BlockSpec((tm, d), lambda i: (i, 0)),
            pl.BlockSpec((CONV_HALO, d), lambda i: (jnp.maximum(i * hb - 1, 0), 0)),
            pl.BlockSpec((CONV_WIDTH, d), const),
            pl.BlockSpec((1, d), const),
            pl.BlockSpec((1, d), const),
            pl.BlockSpec((1, d), const),
            pl.BlockSpec(w_out.shape, lambda i: (0, 0, 0), pipeline_mode=pl.Buffered(1)),
            pl.BlockSpec((1, d), const),
            pl.BlockSpec((tm, d), lambda i: (i, 0)),
        ],
        out_specs=pl.BlockSpec((tm, d), lambda i: (i, 0)),
        out_shape=jax.ShapeDtypeStruct((m, d), F32),
        scratch_shapes=[
            pltpu.VMEM((d // CONV_STRIP, tm + CONV_HALO, CONV_STRIP), F32),
            pltpu.VMEM((tm, d), F32),
            pltpu.VMEM((tm, d), BF16),
        ],
        compiler_params=_params("parallel"),
        name="conv_out",
    )(u, u, w_dw, b_dw, ln_g, ln_b, w_out, b_out, x)


def _pool_kernel(x_ref, xprev_ref, g_ref, w_ref, scale_ref, o_ref, ext_ref, *, tm):
    i = pl.program_id(0)
    d = x_ref.shape[1]
    gc = d // len(POOL_WINDOWS)
    x = x_ref[...]
    h = _rms_norm(x, g_ref[...])
    hprev = jnp.where(i == 0, 0.0, _rms_norm(xprev_ref[...], g_ref[...]))
    t = i * tm + lax.broadcasted_iota(jnp.int32, (tm, 1), 0)
    strips = gc // V7X_LANES
    for g, w in enumerate(POOL_WINDOWS):
        cnt = jnp.minimum(t + 1, w).astype(F32)
        mixes = []
        for s in range(g * strips, (g + 1) * strips):
            lanes = slice(s * V7X_LANES, (s + 1) * V7X_LANES)
            ext_ref[s, 0:POOL_HALO, :] = hprev[:, lanes]
            ext_ref[s, POOL_HALO:, :] = h[:, lanes]
            total = h[:, lanes]
            for k in range(1, w):
                total = total + ext_ref[s, POOL_HALO - k:POOL_HALO - k + tm, :]
            mixes.append(total / cnt - h[:, lanes])
        cols = slice(g * gc, (g + 1) * gc)
        mix = jnp.concatenate(mixes, axis=1).astype(BF16)
        y = _dot(mix, w_ref[g]) * scale_ref[:, cols]
        o_ref[:, cols] = x[:, cols] + y


def _pool(x, gain, w_grp, scale, *, tm):
    m, d = x.shape
    ng, gc, _ = w_grp.shape
    hb = tm // POOL_HALO
    kernel = functools.partial(_pool_kernel, tm=tm)
    return pl.pallas_call(
        kernel,
        grid=(m // tm,),
        in_specs=[
            pl.BlockSpec((tm, d), lambda i: (i, 0)),
            pl.BlockSpec((POOL_HALO, d), lambda i: (jnp.maximum(i * hb - 1, 0), 0)),
            pl.BlockSpec((1, d), lambda i: (0, 0)),
            pl.BlockSpec((ng, gc, gc), lambda i: (0, 0, 0)),
            pl.BlockSpec((1, d), lambda i: (0, 0)),
        ],
        out_specs=pl.BlockSpec((tm, d), lambda i: (i, 0)),
        out_shape=jax.ShapeDtypeStruct((m, d), F32),
        scratch_shapes=[pltpu.VMEM((d // V7X_LANES, tm + POOL_HALO, V7X_LANES), F32)],
        compiler_params=_params("parallel"),
        name="pool",
    )(x, x, gain, w_grp, scale)


def _norm_matmul_body(x_ref, g_ref, w_ref, o_ref):
    o_ref[...] = _dot(_rms_norm(x_ref[...], g_ref[...]).astype(BF16), w_ref[...])


def _norm_matmul(x, gain, w, *, tm, tn, casts):
    m, d = x.shape
    n = w.shape[0] * tn
    return _fused_call(
        _norm_matmul_body,
        name="qkv_proj",
        grid=(n // tn, m // tm),
        j_outer=True,
        in_specs=[
            pl.BlockSpec((tm, d), lambda j, i: (i, 0)),
            pl.BlockSpec((1, d), lambda j, i: (0, 0)),
            pl.BlockSpec((None, d, tn), lambda j, i: (j, 0, 0)),
        ],
        inputs=(x, gain, w),
        out_spec=pl.BlockSpec((tm, tn), lambda j, i: (i, j)),
        out_shape=jax.ShapeDtypeStruct((m, n), F32),
        casts=casts,
    )


def _t5_bucket(rel):
    nb = NUM_BUCKETS // 2
    n = -rel
    ret = jnp.where(n < 0, nb, 0)
    n = jnp.abs(n)
    max_exact = nb // 2
    nf = jnp.maximum(n, 1).astype(F32)
    large = max_exact + (jnp.log(nf / max_exact) / math.log(REL_MAX_DIST / max_exact)
                         * (nb - max_exact)).astype(jnp.int32)
    large = jnp.minimum(large, nb - 1)
    return ret + jnp.where(n < max_exact, n, large)


def _band_bias_kernel(bucket_ref, rel_bias_ref, o_ref):
    h = pl.program_id(0)
    bucket = bucket_ref[...]
    acc = jnp.zeros(bucket.shape, F32)
    for b in range(NUM_BUCKETS):
        acc = jnp.where(bucket == b, rel_bias_ref[b, h], acc)
    o_ref[0] = acc


def _band_bias(rel_bias):
    n_heads = rel_bias.shape[1]
    i = jnp.arange(QBLOCK)[:, None]
    j = jnp.arange(2 * QBLOCK)[None, :]
    bucket = _t5_bucket(j - QBLOCK - i).astype(jnp.int32)
    return pl.pallas_call(
        _band_bias_kernel,
        grid=(n_heads,),
        in_specs=[
            pl.BlockSpec((QBLOCK, 2 * QBLOCK), lambda h: (0, 0)),
            pl.BlockSpec(memory_space=pltpu.SMEM),
        ],
        out_specs=pl.BlockSpec((1, QBLOCK, 2 * QBLOCK), lambda h: (h, 0, 0)),
        out_shape=jax.ShapeDtypeStruct((n_heads, QBLOCK, 2 * QBLOCK), F32),
        compiler_params=_params("parallel"),
        name="band_bias",
    )(bucket, rel_bias)


def _half_rms_norm(x, g, low):
    sq = x * x
    ms_lo = jnp.sum(jnp.where(low, sq, 0.0), axis=-1, keepdims=True) * (1.0 / HEAD_DIM)
    ms_hi = jnp.sum(jnp.where(low, 0.0, sq), axis=-1, keepdims=True) * (1.0 / HEAD_DIM)
    r = jnp.where(low, lax.rsqrt(ms_lo + EPS), lax.rsqrt(ms_hi + EPS))
    return x * r * g


def _attn_kernel(sinks_ref, q_ref, kc_ref, kp_ref, vc_ref, vp_ref, bias_ref, qg_ref, kg_ref, o_ref,
                 *, n_kv, group):
    n = pl.program_id(0)
    pair = 2 * HEAD_DIM
    n_heads = n_kv * group
    qi = lax.broadcasted_iota(jnp.int32, (QBLOCK, 2 * QBLOCK), 0)
    kj = lax.broadcasted_iota(jnp.int32, (QBLOCK, 2 * QBLOCK), 1)
    qc = qi // CHUNK
    kc = kj // CHUNK - QBLOCK // CHUNK
    visible = (kc <= qc) & (kc >= qc - WINDOW_CHUNKS) & ((n > 0) | (kj >= QBLOCK))
    low = lax.broadcasted_iota(jnp.int32, (1, pair), 1) < HEAD_DIM
    scale = HEAD_DIM ** -0.5

    q_all = jnp.concatenate([q_ref[:, c * pair:(c + 1) * pair] for c in range(n_heads // 2)], axis=0)
    q_all = _half_rms_norm(q_all, qg_ref[...], low) * scale
    k_all = jnp.concatenate([ref[:, c * pair:(c + 1) * pair]
                             for c in range(n_kv // 2) for ref in (kp_ref, kc_ref)], axis=0)
    k_all = _half_rms_norm(k_all, kg_ref[...], low).astype(BF16)

    logits, values = [], []
    heads_per_tile = 2 * group
    for hp in range(n_kv // 2):
        kcols = slice(hp * pair, (hp + 1) * pair)
        values.append(jnp.concatenate([vp_ref[:, kcols], vc_ref[:, kcols]], axis=0).astype(BF16))
        stacked = []
        for head in range(hp * heads_per_tile, (hp + 1) * heads_per_tile):
            qhalf, half = head % 2, (head // group) % 2
            q2 = q_all[(head // 2) * QBLOCK:(head // 2 + 1) * QBLOCK]
            qz = jnp.where(low == (qhalf == 0), q2, 0.0)
            if qhalf != half:
                qz = pltpu.roll(qz, HEAD_DIM, axis=1)
            stacked.append(qz.astype(BF16))
        k2 = k_all[hp * 2 * QBLOCK:(hp + 1) * 2 * QBLOCK]
        s = lax.dot_general(jnp.concatenate(stacked, axis=0), k2, (((1,), (1,)), ((), ())),
                            preferred_element_type=F32)
        for g in range(heads_per_tile):
            rows = s[g * QBLOCK:(g + 1) * QBLOCK]
            logits.append(jnp.where(visible, rows + bias_ref[hp * heads_per_tile + g], NEG_INF))

    s = jnp.concatenate(logits, axis=0)
    sink = jnp.concatenate([jnp.full((QBLOCK, 1), sinks_ref[h], F32) for h in range(n_heads)], axis=0)
    mx = jnp.maximum(jnp.max(s, axis=-1, keepdims=True), sink)
    e = jnp.exp(s - mx)
    denom = jnp.sum(e, axis=-1, keepdims=True) + jnp.exp(sink - mx)
    r = 1.0 / denom
    e = e.astype(BF16)

    for hp in range(n_kv // 2):
        rows = slice(hp * heads_per_tile * QBLOCK, (hp + 1) * heads_per_tile * QBLOCK)
        o = _dot(e[rows], values[hp]) * r[rows]
        for c in range(group):
            halves = []
            for qhalf in range(2):
                g = 2 * c + qhalf
                o2 = o[g * QBLOCK:(g + 1) * QBLOCK]
                if qhalf != (g // group) % 2:
                    o2 = pltpu.roll(o2, HEAD_DIM, axis=1)
                halves.append(o2)
            col = (hp * group + c) * pair
            o_ref[:, col:col + pair] = jnp.where(low, halves[0], halves[1]).astype(BF16)


def _attention(qkv, bias, sinks, q_gain, k_gain, *, n_heads, n_kv):
    m = qkv.shape[0]
    group = n_heads // n_kv
    dq = n_heads * HEAD_DIM
    dkv = n_kv * HEAD_DIM
    kblk = dq // dkv
    kernel = functools.partial(_attn_kernel, n_kv=n_kv, group=group)
    prev = lambda n: jnp.maximum(n - 1, 0)
    return pl.pallas_call(
        kernel,
        grid=(m // QBLOCK,),
        in_specs=[
            pl.BlockSpec(memory_space=pltpu.SMEM),
            pl.BlockSpec((QBLOCK, dq), lambda n: (n, 0)),
            pl.BlockSpec((QBLOCK, dkv), lambda n: (n, kblk)),
            pl.BlockSpec((QBLOCK, dkv), lambda n: (prev(n), kblk)),
            pl.BlockSpec((QBLOCK, dkv), lambda n: (n, kblk + 1)),
            pl.BlockSpec((QBLOCK, dkv), lambda n: (prev(n), kblk + 1)),
            pl.BlockSpec((n_heads, QBLOCK, 2 * QBLOCK), lambda n: (0, 0, 0)),
            pl.BlockSpec((1, 2 * HEAD_DIM), lambda n: (0, 0)),
            pl.BlockSpec((1, 2 * HEAD_DIM), lambda n: (0, 0)),
        ],
        out_specs=pl.BlockSpec((QBLOCK, dq), lambda n: (n, 0)),
        out_shape=jax.ShapeDtypeStruct((m, dq), BF16),
        compiler_params=_params("parallel"),
        name="band_attention",
    )(sinks, qkv, qkv, qkv, qkv, qkv, bias, q_gain, k_gain)


def _matmul_residual_body(a_ref, w_ref, x_ref, o_ref):
    o_ref[...] = x_ref[...] + _dot(a_ref[...], w_ref[...])


def _matmul_residual(a, w, x, *, tm, tn, casts):
    m, k = a.shape
    n = w.shape[0] * tn
    return _fused_call(
        _matmul_residual_body,
        name="attn_out_proj",
        grid=(n // tn, m // tm),
        j_outer=True,
        in_specs=[
            pl.BlockSpec((tm, k), lambda j, i: (i, 0)),
            pl.BlockSpec((None, k, tn), lambda j, i: (j, 0, 0)),
            pl.BlockSpec((tm, tn), lambda j, i: (i, j)),
        ],
        inputs=(a, w, x),
        out_spec=pl.BlockSpec((tm, tn), lambda j, i: (i, j)),
        out_shape=jax.ShapeDtypeStruct((m, n), F32),
        casts=casts,
    )


def _ffn_body(x_ref, g_ref, wg_ref, wu_ref, wd_ref, p_ref, pg_ref, wpg_ref, bpg_ref, wpp_ref, o_ref, h_ref):
    j = pl.program_id(1)

    @pl.when(j == 0)
    def _():
        x = x_ref[...]
        h_ref[...] = _rms_norm(x, g_ref[...]).astype(BF16)
        o_ref[...] = x

    h = h_ref[...]
    gate = _dot(h, wg_ref[...])
    up = _dot(h, wu_ref[...])
    act = (gate * jax.nn.sigmoid(gate) * up).astype(BF16)
    o_ref[...] += _dot(act, wd_ref[...])

    @pl.when(j == pl.num_programs(1) - 1)
    def _():
        x1 = o_ref[...]
        h1 = _rms_norm(x1, pg_ref[...]).astype(BF16)
        pb = p_ref[...].astype(BF16)
        tn = wpg_ref.shape[-1]
        for t in range(wpg_ref.shape[0]):
            cols = slice(t * tn, (t + 1) * tn)
            pgate = jax.nn.sigmoid(_dot(h1, wpg_ref[t]) + bpg_ref[:, cols])
            o_ref[:, cols] = x1[:, cols] + pgate * _dot(pb, wpp_ref[t])


def _ffn(x, gain, w_gate, w_up, w_down, p, layer, ple_gain, w_pgate, b_pgate, w_pproj, *, tm, tf, casts):
    m, d = x.shape
    f = w_gate.shape[0] * tf
    pd = p.shape[-1]
    resident = lambda a: pl.BlockSpec(a.shape, lambda i, j: (0,) * a.ndim, pipeline_mode=pl.Buffered(1))
    return _fused_call(
        _ffn_body,
        name="swiglu_ffn_gated_embedding",
        grid=(m // tm, f // tf),
        j_outer=False,
        in_specs=[
            pl.BlockSpec((tm, d), lambda i, j: (i, 0)),
            pl.BlockSpec((1, d), lambda i, j: (0, 0)),
            pl.BlockSpec((None, d, tf), lambda i, j: (j, 0, 0)),
            pl.BlockSpec((None, d, tf), lambda i, j: (j, 0, 0)),
            pl.BlockSpec((None, tf, d), lambda i, j: (0, j, 0)),
            pl.BlockSpec((None, tm, pd), lambda i, j: (layer, i, 0)),
            pl.BlockSpec((1, d), lambda i, j: (0, 0)),
            resident(w_pgate),
            pl.BlockSpec((1, d), lambda i, j: (0, 0)),
            resident(w_pproj),
        ],
        inputs=(x, gain, w_gate, w_up, w_down, p, ple_gain, w_pgate, b_pgate, w_pproj),
        out_spec=pl.BlockSpec((tm, d), lambda i, j: (i, 0)),
        out_shape=jax.ShapeDtypeStruct((m, d), F32),
        scratch=[pltpu.VMEM((tm, d), BF16)],
        casts=casts,
    )


def kernel(x, p, norm_mix, norm_ffn, norm_ple, conv_w_in, conv_b_in, conv_w_dw, conv_b_dw, conv_ln_g,
           conv_ln_b, conv_w_out, conv_b_out, pool_w, pool_scale, attn_w_qkv, attn_q_norm, attn_k_norm,
           attn_sinks, attn_w_o, rel_bias, ffn_w_gate, ffn_w_up, ffn_w_down, ple_w_proj, ple_w_gate,
           ple_b_gate):
    batch, seq, d = x.shape
    depth = norm_mix.shape[0]
    n_heads = d // HEAD_DIM
    n_kv = (attn_w_qkv.shape[-1] // HEAD_DIM - n_heads) // 2
    group = n_heads // n_kv
    assert n_kv % 2 == 0 and group % 2 == 0, "attention kernel pairs heads into 128-lane tiles"
    assert d % (len(POOL_WINDOWS) * V7X_LANES) == 0
    m = batch * seq
    assert batch == 1 and m % QBLOCK == 0, "row tiles assume one sequence"

    tm = _tile(m, 512)
    tn = _tile(d, 1024)
    tf = _tile(ffn_w_gate.shape[-1], 512)
    tq = _tile(attn_w_qkv.shape[-1], 1024)
    tp = _tile(m, 256)
    row = lambda v: v.reshape(1, -1)

    pool_w2 = pool_w.reshape(pool_w.shape[0], -1, pool_w.shape[-1])

    def mixer_weights(i):
        kind, l = i % N_MIXERS, i // N_MIXERS
        if kind == 0:
            return {("conv_in", i): (conv_w_in, l, tn), ("conv_out", i): (conv_w_out, l, tn)}
        if kind == 1:
            return {("pool", i): (pool_w2, l, pool_w2.shape[-1])}
        return {("qkv", i): (attn_w_qkv, l, tq), ("attn_o", i): (attn_w_o, l, tn)}

    def ffn_weights(i):
        return {("gate", i): (ffn_w_gate, i, tf), ("up", i): (ffn_w_up, i, tf), ("down", i): (ffn_w_down, i, d)}

    def ple_weights(i):
        return {("ple_gate", i): (ple_w_gate, i, tn), ("ple_proj", i): (ple_w_proj, i, tn)}

    ready = {}

    def bf16(key, source):
        return ready.pop(key) if key in ready else _cast(*source)

    def hosting(weights, call):
        out, copies = call(list(weights.values()))
        ready.update(zip(weights.keys(), copies))
        return out

    xs = x.reshape(m, d)
    bias = None
    for i in range(depth):
        kind, l = i % N_MIXERS, i // N_MIXERS
        gain = row(norm_mix[i])
        mine = {} if ("gate", i) in ready else {**ffn_weights(i), **ple_weights(i)}
        first = {k: v for k, v in mine.items() if k[0] in ("gate", "up")}
        rest = {k: v for k, v in mine.items() if k not in first}
        if kind == 0:
            w_in, w_out = (bf16(k, w) for k, w in mixer_weights(i).items())
            u = hosting(mine, lambda c: _conv_in(xs, gain, w_in, row(conv_b_in[l]), tm=tm, tn=tn, casts=c))
            xs = _conv_out(u, xs, conv_w_dw[l], row(conv_b_dw[l]), row(conv_ln_g[l]), row(conv_ln_b[l]),
                           w_out, row(conv_b_out[l]), tm=tm)
        elif kind == 1:
            (w_grp,) = (bf16(k, w) for k, w in mixer_weights(i).items())
            w_grp = w_grp.reshape(pool_w.shape[1:])
            xs = _pool(xs, gain, w_grp, row(pool_scale[l]), tm=tp)
        else:
            if bias is None:
                bias = _band_bias(rel_bias)
            w_qkv, w_o = (bf16(k, w) for k, w in mixer_weights(i).items())
            qkv = hosting(first, lambda c: _norm_matmul(xs, gain, w_qkv, tm=tm, tn=tq, casts=c))
            o = _attention(qkv, bias, attn_sinks[l], row(jnp.tile(attn_q_norm[l], 2)),
                           row(jnp.tile(attn_k_norm[l], 2)), n_heads=n_heads, n_kv=n_kv)
            xs = hosting(rest, lambda c: _matmul_residual(o, w_o, xs, tm=tm, tn=tn, casts=c))
        w_gate, w_up, w_down = (bf16(k, w) for k, w in ffn_weights(i).items())
        w_pg, w_pp = (bf16(k, w) for k, w in ple_weights(i).items())
        ahead = {}
        if i + 1 < depth:
            ahead = mixer_weights(i + 1)
            if (i + 1) % N_MIXERS == 1:
                ahead = {**ahead, **ffn_weights(i + 1), **ple_weights(i + 1)}
        xs = hosting(ahead, lambda c: _ffn(
            xs, row(norm_ffn[i]), w_gate, w_up, w_down, p.reshape(depth, m, -1), i, row(norm_ple[i]),
            w_pg, row(ple_b_gate[i]), w_pp, tm=tm, tf=tf, casts=c))
    return xs.reshape(batch, seq, d)
```

```python
import functools
import math

import jax
import jax.numpy as jnp
from jax import lax
from jax.experimental import pallas as pl
from jax.experimental.pallas import tpu as pltpu

N_MIXERS = 3
CHUNK = 64
CONV_WIDTH = 31
POOL_WINDOWS = (2, 4, 8, 16)
HEAD_DIM = 64
WINDOW_CHUNKS = 2
QBLOCK = 128
NUM_BUCKETS = 32
REL_MAX_DIST = 128
EPS = 1e-6
NEG_INF = -1e30

V7X_VMEM_BYTES = 64 * 1024 * 1024
V7X_LANES = 128
V7X_SUBLANES = 8
BF16_ROWS = 2 * V7X_SUBLANES
VMEM_LIMIT_BYTES = V7X_VMEM_BYTES * 7 // 8

CONV_HALO = 32
POOL_HALO = 16
CONV_STRIP = V7X_LANES
NORM_ROWS = 64

BF16 = jnp.bfloat16
F32 = jnp.float32


def _params(*semantics):
    return pltpu.CompilerParams(dimension_semantics=semantics, vmem_limit_bytes=VMEM_LIMIT_BYTES)


def _rms_norm(x, g):
    ms = jnp.mean(x * x, axis=-1, keepdims=True)
    return x * lax.rsqrt(ms + EPS) * g


def _dot(a, b):
    return jnp.dot(a, b, preferred_element_type=F32)


def _tile(n, target, multiple=V7X_LANES):
    if n <= target:
        return n
    t = target - target % multiple
    while n % t:
        t -= multiple
    return t


def _copy_tiles(src_ref, dst_ref):
    tile = dst_ref.shape[-1]
    for t in range(dst_ref.shape[0]):
        dst_ref[t] = src_ref[:, t * tile:(t + 1) * tile].astype(BF16)


def _cast(stack, layer, tile):
    _, r, c = stack.shape
    rb = _tile(r, 256, BF16_ROWS)
    return pl.pallas_call(
        _copy_tiles,
        grid=(r // rb,),
        in_specs=[pl.BlockSpec((None, rb, c), lambda i: (layer, i, 0))],
        out_specs=pl.BlockSpec((c // tile, rb, tile), lambda i: (0, i, 0)),
        out_shape=jax.ShapeDtypeStruct((c // tile, r, tile), BF16),
        compiler_params=_params("parallel"),
        name="weight_cast",
    )(stack)


def _cast_plan(sources, gi, gj, j_outer):
    in_specs, out_specs, out_shapes, row_only = [], [], [], []
    for stack, layer, tile in sources:
        _, r, c = stack.shape
        once = False
        if r % (gi * gj * BF16_ROWS) == 0:
            rows, cols = r // (gi * gj), c
            src_map = lambda i, j: (i * gj + j, 0)
            dst_map = lambda i, j: (0, i * gj + j, 0)
        elif r % (gi * BF16_ROWS) == 0 and c % gj == 0 and (c // gj) % tile == 0:
            rows, cols = r // gi, c // gj
            src_map = lambda i, j: (i, j)
            dst_map = lambda i, j: (j, i, 0)
        elif r % (gj * BF16_ROWS) == 0 and c % gi == 0 and (c // gi) % tile == 0:
            rows, cols = r // gj, c // gi
            src_map = lambda i, j: (j, i)
            dst_map = lambda i, j: (i, j, 0)
        elif r % (gi * BF16_ROWS) == 0 and not j_outer:
            rows, cols = r // gi, c
            src_map = lambda i, j: (i, 0)
            dst_map = lambda i, j: (0, i, 0)
            once = gj > 1
        else:
            n = max(k for k in range(1, gi * gj + 1) if r % (k * BF16_ROWS) == 0)
            step = (lambda i, j: j * gi + i) if j_outer else (lambda i, j: i * gj + j)
            rows, cols = r // n, c
            src_map = lambda i, j, n=n, step=step: (jnp.minimum(step(i, j), n - 1), 0)
            dst_map = lambda i, j, n=n, step=step: (0, jnp.minimum(step(i, j), n - 1), 0)
        if j_outer:
            src_map = (lambda f: lambda j, i: f(i, j))(src_map)
            dst_map = (lambda f: lambda j, i: f(i, j))(dst_map)
        src_map = (lambda f, l: lambda a, b: (l, *f(a, b)))(src_map, layer)
        in_specs.append(pl.BlockSpec((None, rows, cols), src_map))
        out_specs.append(pl.BlockSpec((cols // tile, rows, tile), dst_map))
        out_shapes.append(jax.ShapeDtypeStruct((c // tile, r, tile), BF16))
        row_only.append(once)
    return in_specs, out_specs, out_shapes, row_only


def _fused_call(body, *, name, grid, j_outer, in_specs, inputs, out_spec, out_shape, scratch=(), casts=()):
    n_in, n_cast = len(inputs), len(casts)
    gi, gj = (grid[1], grid[0]) if j_outer else grid
    cast_in, cast_out, cast_shapes, row_only = _cast_plan(casts, gi, gj, j_outer)

    def kernel(*refs):
        ins, rest = refs[:n_in], refs[n_in:]
        cast_src, rest = rest[:n_cast], rest[n_cast:]
        out, rest = rest[0], rest[1:]
        cast_dst, scr = rest[:n_cast], rest[n_cast:]
        for src, dst, once in zip(cast_src, cast_dst, row_only):
            if once:
                pl.when(pl.program_id(1) == 0)(functools.partial(_copy_tiles, src, dst))
            else:
                _copy_tiles(src, dst)
        body(*ins, out, *scr)

    res = pl.pallas_call(
        kernel,
        grid=grid,
        in_specs=[*in_specs, *cast_in],
        out_specs=[out_spec, *cast_out],
        out_shape=[out_shape, *cast_shapes],
        scratch_shapes=list(scratch),
        compiler_params=_params("parallel", "parallel" if j_outer else "arbitrary"),
        name=name,
    )(*inputs, *[stack for stack, _, _ in casts])
    return res[0], list(res[1:])


def _conv_in_body(x_ref, g_ref, wa_ref, wg_ref, ba_ref, bg_ref, u_ref):
    h = _rms_norm(x_ref[...], g_ref[...]).astype(BF16)
    a = _dot(h, wa_ref[...]) + ba_ref[...]
    gate = _dot(h, wg_ref[...]) + bg_ref[...]
    u_ref[...] = a * jax.nn.sigmoid(gate)


def _conv_in(x, gain, w_in, b_in, *, tm, tn, casts):
    m, d = x.shape
    nj = d // tn
    return _fused_call(
        _conv_in_body,
        name="conv_in",
        grid=(nj, m // tm),
        j_outer=True,
        in_specs=[
            pl.BlockSpec((tm, d), lambda j, i: (i, 0)),
            pl.BlockSpec((1, d), lambda j, i: (0, 0)),
            pl.BlockSpec((None, d, tn), lambda j, i: (j, 0, 0)),
            pl.BlockSpec((None, d, tn), lambda j, i: (j + nj, 0, 0)),
            pl.BlockSpec((1, tn), lambda j, i: (0, j)),
            pl.BlockSpec((1, tn), lambda j, i: (0, j + nj)),
        ],
        inputs=(x, gain, w_in, w_in, b_in, b_in),
        out_spec=pl.BlockSpec((tm, tn), lambda j, i: (i, j)),
        out_shape=jax.ShapeDtypeStruct((m, d), F32),
        casts=casts,
    )


def _conv_out_kernel(u_ref, uprev_ref, wdw_ref, bdw_ref, lng_ref, lnb_ref, wout_ref, bout_ref, x_ref,
                     o_ref, ext_ref, conv_ref, v_ref, *, tm):
    i = pl.program_id(0)
    d = u_ref.shape[1]
    first = CONV_HALO - (CONV_WIDTH - 1)

    for s in range(d // CONV_STRIP):
        cols = slice(s * CONV_STRIP, (s + 1) * CONV_STRIP)
        ext_ref[s, 0:CONV_HALO, :] = jnp.where(i == 0, 0.0, uprev_ref[:, cols])
        ext_ref[s, CONV_HALO:, :] = u_ref[:, cols]
        acc = jnp.broadcast_to(bdw_ref[:, cols], (tm, CONV_STRIP))
        for k in range(CONV_WIDTH):
            acc = acc + wdw_ref[k:k + 1, cols] * ext_ref[s, first + k:first + k + tm, :]
        conv_ref[:, cols] = acc

    def norm_rows(c, carry):
        r0 = pl.multiple_of(c * NORM_ROWS, NORM_ROWS)
        conv = conv_ref[pl.ds(r0, NORM_ROWS), :]
        mu = jnp.mean(conv, axis=-1, keepdims=True)
        xc = conv - mu
        y = xc * lax.rsqrt(jnp.mean(xc * xc, axis=-1, keepdims=True) + EPS)
        y = y * lng_ref[...] + lnb_ref[...]
        v_ref[pl.ds(r0, NORM_ROWS), :] = (y * jax.nn.sigmoid(y)).astype(BF16)
        return carry

    lax.fori_loop(0, tm // NORM_ROWS, norm_rows, 0)

    v = v_ref[...]
    tn = wout_ref.shape[-1]
    for t in range(wout_ref.shape[0]):
        cols = slice(t * tn, (t + 1) * tn)
        o_ref[:, cols] = x_ref[:, cols] + _dot(v, wout_ref[t]) + bout_ref[:, cols]


def _conv_out(u, x, w_dw, b_dw, ln_g, ln_b, w_out, b_out, *, tm):
    m, d = u.shape
    hb = tm // CONV_HALO
    const = lambda i: (0, 0)
    return pl.pallas_call(
        functools.partial(_conv_out_kernel, tm=tm),
        grid=(m // tm,),
        in_specs=[
            pl.BlockSpec((tm, d), lambda i: (i, 0)),
            pl.BlockSpec((CONV_HALO, d), lambda i: (jnp.maximum(i * hb - 1, 0), 0)),
            pl.BlockSpec((CONV_WIDTH, d), const),
            pl.BlockSpec((1, d), const),
            pl.BlockSpec((1, d), const),
            pl.BlockSpec((1, d), const),
            pl.BlockSpec(w_out.shape, lambda i: (0, 0, 0), pipeline_mode=pl.Buffered(1)),
            pl.BlockSpec((1, d), const),
            pl.BlockSpec((tm, d), lambda i: (i, 0)),
        ],
        out_specs=pl.BlockSpec((tm, d), lambda i: (i, 0)),
        out_shape=jax.ShapeDtypeStruct((m, d), F32),
        scratch_shapes=[
            pltpu.VMEM((d // CONV_STRIP, tm + CONV_HALO, CONV_STRIP), F32),
            pltpu.VMEM((tm, d), F32),
            pltpu.VMEM((tm, d), BF16),
        ],
        compiler_params=_params("parallel"),
        name="conv_out",
    )(u, u, w_dw, b_dw, ln_g, ln_b, w_out, b_out, x)


def _pool_kernel(x_ref, xprev_ref, g_ref, w_ref, scale_ref, o_ref, ext_ref, *, tm):
    i = pl.program_id(0)
    d = x_ref.shape[1]
    gc = d // len(POOL_WINDOWS)
    x = x_ref[...]
    h = _rms_norm(x, g_ref[...])
    hprev = jnp.where(i == 0, 0.0, _rms_norm(xprev_ref[...], g_ref[...]))
    t = i * tm + lax.broadcasted_iota(jnp.int32, (tm, 1), 0)
    strips = gc // V7X_LANES
    for g, w in enumerate(POOL_WINDOWS):
        cnt = jnp.minimum(t + 1, w).astype(F32)
        mixes = []
        for s in range(g * strips, (g + 1) * strips):
            lanes = slice(s * V7X_LANES, (s + 1) * V7X_LANES)
            ext_ref[s, 0:POOL_HALO, :] = hprev[:, lanes]
            ext_ref[s, POOL_HALO:, :] = h[:, lanes]
            total = h[:, lanes]
            for k in range(1, w):
                total = total + ext_ref[s, POOL_HALO - k:POOL_HALO - k + tm, :]
            mixes.append(total / cnt - h[:, lanes])
        cols = slice(g * gc, (g + 1) * gc)
        mix = jnp.concatenate(mixes, axis=1).astype(BF16)
        y = _dot(mix, w_ref[g]) * scale_ref[:, cols]
        o_ref[:, cols] = x[:, cols] + y


def _pool(x, gain, w_grp, scale, *, tm):
    m, d = x.shape
    ng, gc, _ = w_grp.shape
    hb = tm // POOL_HALO
    kernel = functools.partial(_pool_kernel, tm=tm)
    return pl.pallas_call(
        kernel,
        grid=(m // tm,),
        in_specs=[
            pl.BlockSpec((tm, d), lambda i: (i, 0)),
            pl.BlockSpec((POOL_HALO, d), lambda i: (jnp.maximum(i * hb - 1, 0), 0)),
            pl.BlockSpec((1, d), lambda i: (0, 0)),
            pl.BlockSpec((ng, gc, gc), lambda i: (0, 0, 0)),
            pl.BlockSpec((1, d), lambda i: (0, 0)),
        ],
        out_specs=pl.BlockSpec((tm, d), lambda i: (i, 0)),
        out_shape=jax.ShapeDtypeStruct((m, d), F32),
        scratch_shapes=[pltpu.VMEM((d // V7X_LANES, tm + POOL_HALO, V7X_LANES), F32)],
        compiler_params=_params("parallel"),
        name="pool",
    )(x, x, gain, w_grp, scale)


def _norm_matmul_body(x_ref, g_ref, w_ref, o_ref):
    o_ref[...] = _dot(_rms_norm(x_ref[...], g_ref[...]).astype(BF16), w_ref[...])


def _norm_matmul(x, gain, w, *, tm, tn, casts):
    m, d = x.shape
    n = w.shape[0] * tn
    return _fused_call(
        _norm_matmul_body,
        name="qkv_proj",
        grid=(n // tn, m // tm),
        j_outer=True,
        in_specs=[
            pl.BlockSpec((tm, d), lambda j, i: (i, 0)),
            pl.BlockSpec((1, d), lambda j, i: (0, 0)),
            pl.BlockSpec((None, d, tn), lambda j, i: (j, 0, 0)),
        ],
        inputs=(x, gain, w),
        out_spec=pl.BlockSpec((tm, tn), lambda j, i: (i, j)),
        out_shape=jax.ShapeDtypeStruct((m, n), F32),
        casts=casts,
    )


def _t5_bucket(rel):
    nb = NUM_BUCKETS // 2
    n = -rel
    ret = jnp.where(n < 0, nb, 0)
    n = jnp.abs(n)
    max_exact = nb // 2
    nf = jnp.maximum(n, 1).astype(F32)
    large = max_exact + (jnp.log(nf / max_exact) / math.log(REL_MAX_DIST / max_exact)
                         * (nb - max_exact)).astype(jnp.int32)
    large = jnp.minimum(large, nb - 1)
    return ret + jnp.where(n < max_exact, n, large)


def _band_bias_kernel(bucket_ref, rel_bias_ref, o_ref):
    h = pl.program_id(0)
    bucket = bucket_ref[...]
    acc = jnp.zeros(bucket.shape, F32)
    for b in range(NUM_BUCKETS):
        acc = jnp.where(bucket == b, rel_bias_ref[b, h], acc)
    o_ref[0] = acc


def _band_bias(rel_bias):
    n_heads = rel_bias.shape[1]
    i = jnp.arange(QBLOCK)[:, None]
    j = jnp.arange(2 * QBLOCK)[None, :]
    bucket = _t5_bucket(j - QBLOCK - i).astype(jnp.int32)
    return pl.pallas_call(
        _band_bias_kernel,
        grid=(n_heads,),
        in_specs=[
            pl.BlockSpec((QBLOCK, 2 * QBLOCK), lambda h: (0, 0)),
            pl.BlockSpec(memory_space=pltpu.SMEM),
        ],
        out_specs=pl.BlockSpec((1, QBLOCK, 2 * QBLOCK), lambda h: (h, 0, 0)),
        out_shape=jax.ShapeDtypeStruct((n_heads, QBLOCK, 2 * QBLOCK), F32),
        compiler_params=_params("parallel"),
        name="band_bias",
    )(bucket, rel_bias)


def _half_rms_norm(x, g, low):
    sq = x * x
    ms_lo = jnp.sum(jnp.where(low, sq, 0.0), axis=-1, keepdims=True) * (1.0 / HEAD_DIM)
    ms_hi = jnp.sum(jnp.where(low, 0.0, sq), axis=-1, keepdims=True) * (1.0 / HEAD_DIM)
    r = jnp.where(low, lax.rsqrt(ms_lo + EPS), lax.rsqrt(ms_hi + EPS))
    return x * r * g


def _attn_kernel(sinks_ref, q_ref, kc_ref, kp_ref, vc_ref, vp_ref, bias_ref, qg_ref, kg_ref, o_ref,
                 *, n_kv, group):
    n = pl.program_id(0)
    pair = 2 * HEAD_DIM
    n_heads = n_kv * group
    qi = lax.broadcasted_iota(jnp.int32, (QBLOCK, 2 * QBLOCK), 0)
    kj = lax.broadcasted_iota(jnp.int32, (QBLOCK, 2 * QBLOCK), 1)
    qc = qi // CHUNK
    kc = kj // CHUNK - QBLOCK // CHUNK
    visible = (kc <= qc) & (kc >= qc - WINDOW_CHUNKS) & ((n > 0) | (kj >= QBLOCK))
    low = lax.broadcasted_iota(jnp.int32, (1, pair), 1) < HEAD_DIM
    scale = HEAD_DIM ** -0.5

    q_all = jnp.concatenate([q_ref[:, c * pair:(c + 1) * pair] for c in range(n_heads // 2)], axis=0)
    q_all = _half_rms_norm(q_all, qg_ref[...], low) * scale
    k_all = jnp.concatenate([ref[:, c * pair:(c + 1) * pair]
                             for c in range(n_kv // 2) for ref in (kp_ref, kc_ref)], axis=0)
    k_all = _half_rms_norm(k_all, kg_ref[...], low).astype(BF16)

    logits, values = [], []
    heads_per_tile = 2 * group
    for hp in range(n_kv // 2):
        kcols = slice(hp * pair, (hp + 1) * pair)
        values.append(jnp.concatenate([vp_ref[:, kcols], vc_ref[:, kcols]], axis=0).astype(BF16))
        stacked = []
        for head in range(hp * heads_per_tile, (hp + 1) * heads_per_tile):
            qhalf, half = head % 2, (head // group) % 2
            q2 = q_all[(head // 2) * QBLOCK:(head // 2 + 1) * QBLOCK]
            qz = jnp.where(low == (qhalf == 0), q2, 0.0)
            if qhalf != half:
                qz = pltpu.roll(qz, HEAD_DIM, axis=1)
            stacked.append(qz.astype(BF16))
        k2 = k_all[hp * 2 * QBLOCK:(hp + 1) * 2 * QBLOCK]
        s = lax.dot_general(jnp.concatenate(stacked, axis=0), k2, (((1,), (1,)), ((), ())),
                            preferred_element_type=F32)
        for g in range(heads_per_tile):
            rows = s[g * QBLOCK:(g + 1) * QBLOCK]
            logits.append(jnp.where(visible, rows + bias_ref[hp * heads_per_tile + g], NEG_INF))

    s = jnp.concatenate(logits, axis=0)
    sink = jnp.concatenate([jnp.full((QBLOCK, 1), sinks_ref[h], F32) for h in range(n_heads)], axis=0)
    mx = jnp.maximum(jnp.max(s, axis=-1, keepdims=True), sink)
    e = jnp.exp(s - mx)
    denom = jnp.sum(e, axis=-1, keepdims=True) + jnp.exp(sink - mx)
    r = 1.0 / denom
    e = e.astype(BF16)

    for hp in range(n_kv // 2):
        rows = slice(hp * heads_per_tile * QBLOCK, (hp + 1) * heads_per_tile * QBLOCK)
        o = _dot(e[rows], values[hp]) * r[rows]
        for c in range(group):
            halves = []
            for qhalf in range(2):
                g = 2 * c + qhalf
                o2 = o[g * QBLOCK:(g + 1) * QBLOCK]
                if qhalf != (g // group) % 2:
                    o2 = pltpu.roll(o2, HEAD_DIM, axis=1)
                halves.append(o2)
            col = (hp * group + c) * pair
            o_ref[:, col:col + pair] = jnp.where(low, halves[0], halves[1]).astype(BF16)


def _attention(qkv, bias, sinks, q_gain, k_gain, *, n_heads, n_kv):
    m = qkv.shape[0]
    group = n_heads // n_kv
    dq = n_heads * HEAD_DIM
    dkv = n_kv * HEAD_DIM
    kblk = dq // dkv
    kernel = functools.partial(_attn_kernel, n_kv=n_kv, group=group)
    prev = lambda n: jnp.maximum(n - 1, 0)
    return pl.pallas_call(
        kernel,
        grid=(m // QBLOCK,),
        in_specs=[
            pl.BlockSpec(memory_space=pltpu.SMEM),
            pl.BlockSpec((QBLOCK, dq), lambda n: (n, 0)),
            pl.BlockSpec((QBLOCK, dkv), lambda n: (n, kblk)),
            pl.BlockSpec((QBLOCK, dkv), lambda n: (prev(n), kblk)),
            pl.BlockSpec((QBLOCK, dkv), lambda n: (n, kblk + 1)),
            pl.BlockSpec((QBLOCK, dkv), lambda n: (prev(n), kblk + 1)),
            pl.BlockSpec((n_heads, QBLOCK, 2 * QBLOCK), lambda n: (0, 0, 0)),
            pl.BlockSpec((1, 2 * HEAD_DIM), lambda n: (0, 0)),
            pl.BlockSpec((1, 2 * HEAD_DIM), lambda n: (0, 0)),
        ],
        out_specs=pl.BlockSpec((QBLOCK, dq), lambda n: (n, 0)),
        out_shape=jax.ShapeDtypeStruct((m, dq), BF16),
        compiler_params=_params("parallel"),
        name="band_attention",
    )(sinks, qkv, qkv, qkv, qkv, qkv, bias, q_gain, k_gain)


def _matmul_residual_body(a_ref, w_ref, x_ref, o_ref):
    o_ref[...] = x_ref[...] + _dot(a_ref[...], w_ref[...])


def _matmul_residual(a, w, x, *, tm, tn, casts):
    m, k = a.shape
    n = w.shape[0] * tn
    return _fused_call(
        _matmul_residual_body,
        name="attn_out_proj",
        grid=(n // tn, m // tm),
        j_outer=True,
        in_specs=[
            pl.BlockSpec((tm, k), lambda j, i: (i, 0)),
            pl.BlockSpec((None, k, tn), lambda j, i: (j, 0, 0)),
            pl.BlockSpec((tm, tn), lambda j, i: (i, j)),
        ],
        inputs=(a, w, x),
        out_spec=pl.BlockSpec((tm, tn), lambda j, i: (i, j)),
        out_shape=jax.ShapeDtypeStruct((m, n), F32),
        casts=casts,
    )


def _ffn_body(x_ref, g_ref, wg_ref, wu_ref, wd_ref, p_ref, pg_ref, wpg_ref, bpg_ref, wpp_ref, o_ref, h_ref):
    j = pl.program_id(1)

    @pl.when(j == 0)
    def _():
        x = x_ref[...]
        h_ref[...] = _rms_norm(x, g_ref[...]).astype(BF16)
        o_ref[...] = x

    h = h_ref[...]
    gate = _dot(h, wg_ref[...])
    up = _dot(h, wu_ref[...])
    act = (gate * jax.nn.sigmoid(gate) * up).astype(BF16)
    o_ref[...] += _dot(act, wd_ref[...])

    @pl.when(j == pl.num_programs(1) - 1)
    def _():
        x1 = o_ref[...]
        h1 = _rms_norm(x1, pg_ref[...]).astype(BF16)
        pb = p_ref[...].astype(BF16)
        tn = wpg_ref.shape[-1]
        for t in range(wpg_ref.shape[0]):
            cols = slice(t * tn, (t + 1) * tn)
            pgate = jax.nn.sigmoid(_dot(h1, wpg_ref[t]) + bpg_ref[:, cols])
            o_ref[:, cols] = x1[:, cols] + pgate * _dot(pb, wpp_ref[t])


def _ffn(x, gain, w_gate, w_up, w_down, p, layer, ple_gain, w_pgate, b_pgate, w_pproj, *, tm, tf, casts):
    m, d = x.shape
    f = w_gate.shape[0] * tf
    pd = p.shape[-1]
    resident = lambda a: pl.BlockSpec(a.shape, lambda i, j: (0,) * a.ndim, pipeline_mode=pl.Buffered(1))
    return _fused_call(
        _ffn_body,
        name="swiglu_ffn_gated_embedding",
        grid=(m // tm, f // tf),
        j_outer=False,
        in_specs=[
            pl.BlockSpec((tm, d), lambda i, j: (i, 0)),
            pl.BlockSpec((1, d), lambda i, j: (0, 0)),
            pl.BlockSpec((None, d, tf), lambda i, j: (j, 0, 0)),
            pl.BlockSpec((None, d, tf), lambda i, j: (j, 0, 0)),
            pl.BlockSpec((None, tf, d), lambda i, j: (0, j, 0)),
            pl.BlockSpec((None, tm, pd), lambda i, j: (layer, i, 0)),
            pl.BlockSpec((1, d), lambda i, j: (0, 0)),
            resident(w_pgate),
            pl.BlockSpec((1, d), lambda i, j: (0, 0)),
            resident(w_pproj),
        ],
        inputs=(x, gain, w_gate, w_up, w_down, p, ple_gain, w_pgate, b_pgate, w_pproj),
        out_spec=pl.BlockSpec((tm, d), lambda i, j: (i, 0)),
        out_shape=jax.ShapeDtypeStruct((m, d), F32),
        scratch=[pltpu.VMEM((tm, d), BF16)],
        casts=casts,
    )


def kernel(x, p, norm_mix, norm_ffn, norm_ple, conv_w_in, conv_b_in, conv_w_dw, conv_b_dw, conv_ln_g,
           conv_ln_b, conv_w_out, conv_b_out, pool_w, pool_scale, attn_w_qkv, attn_q_norm, attn_k_norm,
           attn_sinks, attn_w_o, rel_bias, ffn_w_gate, ffn_w_up, ffn_w_down, ple_w_proj, ple_w_gate,
           ple_b_gate):
    batch, seq, d = x.shape
    depth = norm_mix.shape[0]
    n_heads = d // HEAD_DIM
    n_kv = (attn_w_qkv.shape[-1] // HEAD_DIM - n_heads) // 2
    group = n_heads // n_kv
    assert n_kv % 2 == 0 and group % 2 == 0, "attention kernel pairs heads into 128-lane tiles"
    assert d % (len(POOL_WINDOWS) * V7X_LANES) == 0
    m = batch * seq
    assert batch == 1 and m % QBLOCK == 0, "row tiles assume one sequence"

    tm = _tile(m, 512)
    tn = _tile(d, 1024)
    tf = _tile(ffn_w_gate.shape[-1], 512)
    tq = _tile(attn_w_qkv.shape[-1], 1024)
    tp = _tile(m, 512)
    row = lambda v: v.reshape(1, -1)

    pool_w2 = pool_w.reshape(pool_w.shape[0], -1, pool_w.shape[-1])

    def mixer_weights(i):
        kind, l = i % N_MIXERS, i // N_MIXERS
        if kind == 0:
            return {("conv_in", i): (conv_w_in, l, tn), ("conv_out", i): (conv_w_out, l, tn)}
        if kind == 1:
            return {("pool", i): (pool_w2, l, pool_w2.shape[-1])}
        return {("qkv", i): (attn_w_qkv, l, tq), ("attn_o", i): (attn_w_o, l, tn)}

    def ffn_weights(i):
        return {("gate", i): (ffn_w_gate, i, tf), ("up", i): (ffn_w_up, i, tf), ("down", i): (ffn_w_down, i, d)}

    def ple_weights(i):
        return {("ple_gate", i): (ple_w_gate, i, tn), ("ple_proj", i): (ple_w_proj, i, tn)}

    ready = {}

    def bf16(key, source):
        return ready.pop(key) if key in ready else _cast(*source)

    def hosting(weights, call):
        out, copies = call(list(weights.values()))
        ready.update(zip(weights.keys(), copies))
        return out

    xs = x.reshape(m, d)
    bias = None
    for i in range(depth):
        kind, l = i % N_MIXERS, i // N_MIXERS
        gain = row(norm_mix[i])
        mine = {} if ("gate", i) in ready else {**ffn_weights(i), **ple_weights(i)}
        first = {k: v for k, v in mine.items() if k[0] in ("gate", "up")}
        rest = {k: v for k, v in mine.items() if k not in first}
        if kind == 0:
            w_in, w_out = (bf16(k, w) for k, w in mixer_weights(i).items())
            u = hosting(mine, lambda c: _conv_in(xs, gain, w_in, row(conv_b_in[l]), tm=tm, tn=tn, casts=c))
            xs = _conv_out(u, xs, conv_w_dw[l], row(conv_b_dw[l]), row(conv_ln_g[l]), row(conv_ln_b[l]),
                           w_out, row(conv_b_out[l]), tm=tm)
        elif kind == 1:
            (w_grp,) = (bf16(k, w) for k, w in mixer_weights(i).items())
            w_grp = w_grp.reshape(pool_w.shape[1:])
            xs = _pool(xs, gain, w_grp, row(pool_scale[l]), tm=tp)
        else:
            if bias is None:
                bias = _band_bias(rel_bias)
            w_qkv, w_o = (bf16(k, w) for k, w in mixer_weights(i).items())
            qkv = hosting(first, lambda c: _norm_matmul(xs, gain, w_qkv, tm=tm, tn=tq, casts=c))
            o = _attention(qkv, bias, attn_sinks[l], row(jnp.tile(attn_q_norm[l], 2)),
                           row(jnp.tile(attn_k_norm[l], 2)), n_heads=n_heads, n_kv=n_kv)
            xs = hosting(rest, lambda c: _matmul_residual(o, w_o, xs, tm=tm, tn=tn, casts=c))
        w_gate, w_up, w_down = (bf16(k, w) for k, w in ffn_weights(i).items())
        w_pg, w_pp = (bf16(k, w) for k, w in ple_weights(i).items())
        ahead = {}
        if i + 1 < depth:
            ahead = mixer_weights(i + 1)
            if (i + 1) % N_MIXERS == 1:
                ahead = {**ahead, **ffn_weights(i + 1), **ple_weights(i + 1)}
        xs = hosting(ahead, lambda c: _ffn(
            xs, row(norm_ffn[i]), w_gate, w_up, w_down, p.reshape(depth, m, -1), i, row(norm_ple[i]),
            w_pg, row(ple_b_gate[i]), w_pp, tm=tm, tf=tf, casts=c))
    return xs.reshape(batch, seq, d)
```

```python
import functools
import math

import jax
import jax.numpy as jnp
from jax import lax
from jax.experimental import pallas as pl
from jax.experimental.pallas import tpu as pltpu

N_MIXERS = 3
CHUNK = 64
CONV_WIDTH = 31
POOL_WINDOWS = (2, 4, 8, 16)
HEAD_DIM = 64
WINDOW_CHUNKS = 2
QBLOCK = 128
NUM_BUCKETS = 32
REL_MAX_DIST = 128
EPS = 1e-6
NEG_INF = -1e30

V7X_VMEM_BYTES = 64 * 1024 * 1024
V7X_LANES = 128
V7X_SUBLANES = 8
BF16_ROWS = 2 * V7X_SUBLANES
VMEM_LIMIT_BYTES = V7X_VMEM_BYTES * 7 // 8

CONV_HALO = 32
POOL_HALO = 16
CONV_STRIP = V7X_LANES
NORM_ROWS = 64

BF16 = jnp.bfloat16
F32 = jnp.float32


def _params(*semantics):
    return pltpu.CompilerParams(dimension_semantics=semantics, vmem_limit_bytes=VMEM_LIMIT_BYTES)


def _rms_norm(x, g):
    ms = jnp.mean(x * x, axis=-1, keepdims=True)
    return x * lax.rsqrt(ms + EPS) * g


def _dot(a, b):
    return jnp.dot(a, b, preferred_element_type=F32)


def _tile(n, target, multiple=V7X_LANES):
    if n <= target:
        return n
    t = target - target % multiple
    while n % t:
        t -= multiple
    return t


def _copy_tiles(src_ref, dst_ref):
    tile = dst_ref.shape[-1]
    for t in range(dst_ref.shape[0]):
        dst_ref[t] = src_ref[:, t * tile:(t + 1) * tile].astype(BF16)


def _cast(stack, layer, tile):
    _, r, c = stack.shape
    rb = _tile(r, 256, BF16_ROWS)
    return pl.pallas_call(
        _copy_tiles,
        grid=(r // rb,),
        in_specs=[pl.BlockSpec((None, rb, c), lambda i: (layer, i, 0))],
        out_specs=pl.BlockSpec((c // tile, rb, tile), lambda i: (0, i, 0)),
        out_shape=jax.ShapeDtypeStruct((c // tile, r, tile), BF16),
        compiler_params=_params("parallel"),
        name="weight_cast",
    )(stack)


def _cast_plan(sources, gi, gj, j_outer):
    in_specs, out_specs, out_shapes, row_only = [], [], [], []
    for stack, layer, tile in sources:
        _, r, c = stack.shape
        once = False
        if r % (gi * gj * BF16_ROWS) == 0:
            rows, cols = r // (gi * gj), c
            src_map = lambda i, j: (i * gj + j, 0)
            dst_map = lambda i, j: (0, i * gj + j, 0)
        elif r % (gi * BF16_ROWS) == 0 and c % gj == 0 and (c // gj) % tile == 0:
            rows, cols = r // gi, c // gj
            src_map = lambda i, j: (i, j)
            dst_map = lambda i, j: (j, i, 0)
        elif r % (gj * BF16_ROWS) == 0 and c % gi == 0 and (c // gi) % tile == 0:
            rows, cols = r // gj, c // gi
            src_map = lambda i, j: (j, i)
            dst_map = lambda i, j: (i, j, 0)
        elif r % (gi * BF16_ROWS) == 0 and not j_outer:
            rows, cols = r // gi, c
            src_map = lambda i, j: (i, 0)
            dst_map = lambda i, j: (0, i, 0)
            once = gj > 1
        else:
            n = max(k for k in range(1, gi * gj + 1) if r % (k * BF16_ROWS) == 0)
            step = (lambda i, j: j * gi + i) if j_outer else (lambda i, j: i * gj + j)
            rows, cols = r // n, c
            src_map = lambda i, j, n=n, step=step: (jnp.minimum(step(i, j), n - 1), 0)
            dst_map = lambda i, j, n=n, step=step: (0, jnp.minimum(step(i, j), n - 1), 0)
        if j_outer:
            src_map = (lambda f: lambda j, i: f(i, j))(src_map)
            dst_map = (lambda f: lambda j, i: f(i, j))(dst_map)
        src_map = (lambda f, l: lambda a, b: (l, *f(a, b)))(src_map, layer)
        in_specs.append(pl.BlockSpec((None, rows, cols), src_map))
        out_specs.append(pl.BlockSpec((cols // tile, rows, tile), dst_map))
        out_shapes.append(jax.ShapeDtypeStruct((c // tile, r, tile), BF16))
        row_only.append(once)
    return in_specs, out_specs, out_shapes, row_only


def _fused_call(body, *, name, grid, j_outer, in_specs, inputs, out_spec, out_shape, scratch=(), casts=()):
    n_in, n_cast = len(inputs), len(casts)
    gi, gj = (grid[1], grid[0]) if j_outer else grid
    cast_in, cast_out, cast_shapes, row_only = _cast_plan(casts, gi, gj, j_outer)

    def kernel(*refs):
        ins, rest = refs[:n_in], refs[n_in:]
        cast_src, rest = rest[:n_cast], rest[n_cast:]
        out, rest = rest[0], rest[1:]
        cast_dst, scr = rest[:n_cast], rest[n_cast:]
        for src, dst, once in zip(cast_src, cast_dst, row_only):
            if once:
                pl.when(pl.program_id(1) == 0)(functools.partial(_copy_tiles, src, dst))
            else:
                _copy_tiles(src, dst)
        body(*ins, out, *scr)

    res = pl.pallas_call(
        kernel,
        grid=grid,
        in_specs=[*in_specs, *cast_in],
        out_specs=[out_spec, *cast_out],
        out_shape=[out_shape, *cast_shapes],
        scratch_shapes=list(scratch),
        compiler_params=_params("parallel", "parallel" if j_outer else "arbitrary"),
        name=name,
    )(*inputs, *[stack for stack, _, _ in casts])
    return res[0], list(res[1:])


def _conv_in_body(x_ref, g_ref, wa_ref, wg_ref, ba_ref, bg_ref, u_ref):
    h = _rms_norm(x_ref[...], g_ref[...]).astype(BF16)
    a = _dot(h, wa_ref[...]) + ba_ref[...]
    gate = _dot(h, wg_ref[...]) + bg_ref[...]
    u_ref[...] = a * jax.nn.sigmoid(gate)


def _conv_in(x, gain, w_in, b_in, *, tm, tn, casts):
    m, d = x.shape
    nj = d // tn
    return _fused_call(
        _conv_in_body,
        name="conv_in",
        grid=(nj, m // tm),
        j_outer=True,
        in_specs=[
            pl.BlockSpec((tm, d), lambda j, i: (i, 0)),
            pl.BlockSpec((1, d), lambda j, i: (0, 0)),
            pl.BlockSpec((None, d, tn), lambda j, i: (j, 0, 0)),
            pl.BlockSpec((None, d, tn), lambda j, i: (j + nj, 0, 0)),
            pl.BlockSpec((1, tn), lambda j, i: (0, j)),
            pl.BlockSpec((1, tn), lambda j, i: (0, j + nj)),
        ],
        inputs=(x, gain, w_in, w_in, b_in, b_in),
        out_spec=pl.BlockSpec((tm, tn), lambda j, i: (i, j)),
        out_shape=jax.ShapeDtypeStruct((m, d), F32),
        casts=casts,
    )


def _conv_out_kernel(u_ref, uprev_ref, wdw_ref, bdw_ref, lng_ref, lnb_ref, wout_ref, bout_ref, x_ref,
                     o_ref, ext_ref, conv_ref, v_ref, *, tm):
    i = pl.program_id(0)
    d = u_ref.shape[1]
    first = CONV_HALO - (CONV_WIDTH - 1)

    for s in range(d // CONV_STRIP):
        cols = slice(s * CONV_STRIP, (s + 1) * CONV_STRIP)
        ext_ref[s, 0:CONV_HALO, :] = jnp.where(i == 0, 0.0, uprev_ref[:, cols])
        ext_ref[s, CONV_HALO:, :] = u_ref[:, cols]
        acc = jnp.broadcast_to(bdw_ref[:, cols], (tm, CONV_STRIP))
        for k in range(CONV_WIDTH):
            acc = acc + wdw_ref[k:k + 1, cols] * ext_ref[s, first + k:first + k + tm, :]
        conv_ref[:, cols] = acc

    def norm_rows(c, carry):
        r0 = pl.multiple_of(c * NORM_ROWS, NORM_ROWS)
        conv = conv_ref[pl.ds(r0, NORM_ROWS), :]
        mu = jnp.mean(conv, axis=-1, keepdims=True)
        xc = conv - mu
        y = xc * lax.rsqrt(jnp.mean(xc * xc, axis=-1, keepdims=True) + EPS)
        y = y * lng_ref[...] + lnb_ref[...]
        v_ref[pl.ds(r0, NORM_ROWS), :] = (y * jax.nn.sigmoid(y)).astype(BF16)
        return carry

    lax.fori_loop(0, tm // NORM_ROWS, norm_rows, 0)

    v = v_ref[...]
    tn = wout_ref.shape[-1]
    for t in range(wout_ref.shape[0]):
        cols = slice(t * tn, (t + 1) * tn)
        o_ref[:, cols] = x_ref[:, cols] + _dot(v, wout_ref[t]) + bout_ref[:, cols]


def _conv_out(u, x, w_dw, b_dw, ln_g, ln_b, w_out, b_out, *, tm):
    m, d = u.shape
    hb = tm // CONV_HALO
    const = lambda i: (0, 0)
    return pl.pallas_call(
        functools.partial(_conv_out_kernel, tm=tm),
        grid=(m // tm,),
        in_specs=[
            pl.BlockSpec((tm, d), lambda i: (i, 0)),
            pl.BlockSpec((CONV_HALO, d), lambda i: (jnp.maximum(i * hb - 1, 0), 0)),
            pl.BlockSpec((CONV_WIDTH, d), const),
            pl.BlockSpec((1, d), const),
            pl.BlockSpec((1, d), const),
            pl.BlockSpec((1, d), const),
            pl.BlockSpec(w_out.shape, lambda i: (0, 0, 0), pipeline_mode=pl.Buffered(1)),
            pl.BlockSpec((1, d), const),
            pl.BlockSpec((tm, d), lambda i: (i, 0)),
        ],
        out_specs=pl.BlockSpec((tm, d), lambda i: (i, 0)),
        out_shape=jax.ShapeDtypeStruct((m, d), F32),
        scratch_shapes=[
            pltpu.VMEM((d // CONV_STRIP, tm + CONV_HALO, CONV_STRIP), F32),
            pltpu.VMEM((tm, d), F32),
            pltpu.VMEM((tm, d), BF16),
        ],
        compiler_params=_params("parallel"),
        name="conv_out",
    )(u, u, w_dw, b_dw, ln_g, ln_b, w_out, b_out, x)


def _pool_kernel(x_ref, xprev_ref, g_ref, w_ref, scale_ref, o_ref, ext_ref, *, tm):
    i = pl.program_id(0)
    d = x_ref.shape[1]
    gc = d // len(POOL_WINDOWS)
    x = x_ref[...]
    h = _rms_norm(x, g_ref[...])
    hprev = jnp.where(i == 0, 0.0, _rms_norm(xprev_ref[...], g_ref[...]))
    t = i * tm + lax.broadcasted_iota(jnp.int32, (tm, 1), 0)
    strips = gc // V7X_LANES
    for g, w in enumerate(POOL_WINDOWS):
        cnt = jnp.minimum(t + 1, w).astype(F32)
        mixes = []
        for s in range(g * strips, (g + 1) * strips):
            lanes = slice(s * V7X_LANES, (s + 1) * V7X_LANES)
            ext_ref[s, 0:POOL_HALO, :] = hprev[:, lanes]
            ext_ref[s, POOL_HALO:, :] = h[:, lanes]
            total = h[:, lanes]
            for k in range(1, w):
                total = total + ext_ref[s, POOL_HALO - k:POOL_HALO - k + tm, :]
            mixes.append(total / cnt - h[:, lanes])
        cols = slice(g * gc, (g + 1) * gc)
        mix = jnp.concatenate(mixes, axis=1).astype(BF16)
        y = _dot(mix, w_ref[g]) * scale_ref[:, cols]
        o_ref[:, cols] = x[:, cols] + y


def _pool(x, gain, w_grp, scale, *, tm):
    m, d = x.shape
    ng, gc, _ = w_grp.shape
    hb = tm // POOL_HALO
    kernel = functools.partial(_pool_kernel, tm=tm)
    return pl.pallas_call(
        kernel,
        grid=(m // tm,),
        in_specs=[
            pl.BlockSpec((tm, d), lambda i: (i, 0)),
            pl.BlockSpec((POOL_HALO, d), lambda i: (jnp.maximum(i * hb - 1, 0), 0)),
            pl.BlockSpec((1, d), lambda i: (0, 0)),
            pl.BlockSpec((ng, gc, gc), lambda i: (0, 0, 0)),
            pl.BlockSpec((1, d), lambda i: (0, 0)),
        ],
        out_specs=pl.BlockSpec((tm, d), lambda i: (i, 0)),
        out_shape=jax.ShapeDtypeStruct((m, d), F32),
        scratch_shapes=[pltpu.VMEM((d // V7X_LANES, tm + POOL_HALO, V7X_LANES), F32)],
        compiler_params=_params("parallel"),
        name="pool",
    )(x, x, gain, w_grp, scale)


def _norm_matmul_body(x_ref, g_ref, w_ref, o_ref):
    o_ref[...] = _dot(_rms_norm(x_ref[...], g_ref[...]).astype(BF16), w_ref[...])


def _norm_matmul(x, gain, w, *, tm, tn, casts):
    m, d = x.shape
    n = w.shape[0] * tn
    return _fused_call(
        _norm_matmul_body,
        name="qkv_proj",
        grid=(n // tn, m // tm),
        j_outer=True,
        in_specs=[
            pl.BlockSpec((tm, d), lambda j, i: (i, 0)),
            pl.BlockSpec((1, d), lambda j, i: (0, 0)),
            pl.BlockSpec((None, d, tn), lambda j, i: (j, 0, 0)),
        ],
        inputs=(x, gain, w),
        out_spec=pl.BlockSpec((tm, tn), lambda j, i: (i, j)),
        out_shape=jax.ShapeDtypeStruct((m, n), F32),
        casts=casts,
    )


def _t5_bucket(rel):
    nb = NUM_BUCKETS // 2
    n = -rel
    ret = jnp.where(n < 0, nb, 0)
    n = jnp.abs(n)
    max_exact = nb // 2
    nf = jnp.maximum(n, 1).astype(F32)
    large = max_exact + (jnp.log(nf / max_exact) / math.log(REL_MAX_DIST / max_exact)
                         * (nb - max_exact)).astype(jnp.int32)
    large = jnp.minimum(large, nb - 1)
    return ret + jnp.where(n < max_exact, n, large)


def _band_bias_kernel(bucket_ref, rel_bias_ref, o_ref):
    h = pl.program_id(0)
    bucket = bucket_ref[...]
    acc = jnp.zeros(bucket.shape, F32)
    for b in range(NUM_BUCKETS):
        acc = jnp.where(bucket == b, rel_bias_ref[b, h], acc)
    o_ref[0] = acc


def _band_bias(rel_bias):
    n_heads = rel_bias.shape[1]
    i = jnp.arange(QBLOCK)[:, None]
    j = jnp.arange(2 * QBLOCK)[None, :]
    bucket = _t5_bucket(j - QBLOCK - i).astype(jnp.int32)
    return pl.pallas_call(
        _band_bias_kernel,
        grid=(n_heads,),
        in_specs=[
            pl.BlockSpec((QBLOCK, 2 * QBLOCK), lambda h: (0, 0)),
            pl.BlockSpec(memory_space=pltpu.SMEM),
        ],
        out_specs=pl.BlockSpec((1, QBLOCK, 2 * QBLOCK), lambda h: (h, 0, 0)),
        out_shape=jax.ShapeDtypeStruct((n_heads, QBLOCK, 2 * QBLOCK), F32),
        compiler_params=_params("parallel"),
        name="band_bias",
    )(bucket, rel_bias)


def _half_rms_norm(x, g, low):
    sq = x * x
    ms_lo = jnp.sum(jnp.where(low, sq, 0.0), axis=-1, keepdims=True) * (1.0 / HEAD_DIM)
    ms_hi = jnp.sum(jnp.where(low, 0.0, sq), axis=-1, keepdims=True) * (1.0 / HEAD_DIM)
    r = jnp.where(low, lax.rsqrt(ms_lo + EPS), lax.rsqrt(ms_hi + EPS))
    return x * r * g


def _attn_kernel(sinks_ref, q_ref, kc_ref, kp_ref, vc_ref, vp_ref, bias_ref, qg_ref, kg_ref, o_ref,
                 *, n_kv, group):
    n = pl.program_id(0)
    pair = 2 * HEAD_DIM
    n_heads = n_kv * group
    qi = lax.broadcasted_iota(jnp.int32, (QBLOCK, 2 * QBLOCK), 0)
    kj = lax.broadcasted_iota(jnp.int32, (QBLOCK, 2 * QBLOCK), 1)
    qc = qi // CHUNK
    kc = kj // CHUNK - QBLOCK // CHUNK
    visible = (kc <= qc) & (kc >= qc - WINDOW_CHUNKS) & ((n > 0) | (kj >= QBLOCK))
    low = lax.broadcasted_iota(jnp.int32, (1, pair), 1) < HEAD_DIM
    scale = HEAD_DIM ** -0.5

    q_all = jnp.concatenate([q_ref[:, c * pair:(c + 1) * pair] for c in range(n_heads // 2)], axis=0)
    q_all = _half_rms_norm(q_all, qg_ref[...], low) * scale
    k_all = jnp.concatenate([ref[:, c * pair:(c + 1) * pair]
                             for c in range(n_kv // 2) for ref in (kp_ref, kc_ref)], axis=0)
    k_all = _half_rms_norm(k_all, kg_ref[...], low).astype(BF16)

    logits, values = [], []
    heads_per_tile = 2 * group
    for hp in range(n_kv // 2):
        kcols = slice(hp * pair, (hp + 1) * pair)
        values.append(jnp.concatenate([vp_ref[:, kcols], vc_ref[:, kcols]], axis=0).astype(BF16))
        stacked = []
        for head in range(hp * heads_per_tile, (hp + 1) * heads_per_tile):
            qhalf, half = head % 2, (head // group) % 2
            q2 = q_all[(head // 2) * QBLOCK:(head // 2 + 1) * QBLOCK]
            qz = jnp.where(low == (qhalf == 0), q2, 0.0)
            if qhalf != half:
                qz = pltpu.roll(qz, HEAD_DIM, axis=1)
            stacked.append(qz.astype(BF16))
        k2 = k_all[hp * 2 * QBLOCK:(hp + 1) * 2 * QBLOCK]
        s = lax.dot_general(jnp.concatenate(stacked, axis=0), k2, (((1,), (1,)), ((), ())),
                            preferred_element_type=F32)
        for g in range(heads_per_tile):
            rows = s[g * QBLOCK:(g + 1) * QBLOCK]
            logits.append(jnp.where(visible, rows + bias_ref[hp * heads_per_tile + g], NEG_INF))

    s = jnp.concatenate(logits, axis=0)
    sink = jnp.concatenate([jnp.full((QBLOCK, 1), sinks_ref[h], F32) for h in range(n_heads)], axis=0)
    mx = jnp.maximum(jnp.max(s, axis=-1, keepdims=True), sink)
    e = jnp.exp(s - mx)
    denom = jnp.sum(e, axis=-1, keepdims=True) + jnp.exp(sink - mx)
    r = 1.0 / denom
    e = e.astype(BF16)

    for hp in range(n_kv // 2):
        rows = slice(hp * heads_per_tile * QBLOCK, (hp + 1) * heads_per_tile * QBLOCK)
        o = _dot(e[rows], values[hp]) * r[rows]
        for c in range(group):
            halves = []
            for qhalf in range(2):
                g = 2 * c + qhalf
                o2 = o[g * QBLOCK:(g + 1) * QBLOCK]
                if qhalf != (g // group) % 2:
                    o2 = pltpu.roll(o2, HEAD_DIM, axis=1)
                halves.append(o2)
            col = (hp * group + c) * pair
            o_ref[:, col:col + pair] = jnp.where(low, halves[0], halves[1]).astype(BF16)


def _attention(qkv, bias, sinks, q_gain, k_gain, *, n_heads, n_kv, casts):
    m = qkv.shape[0]
    group = n_heads // n_kv
    dq = n_heads * HEAD_DIM
    dkv = n_kv * HEAD_DIM
    kblk = dq // dkv
    prev = lambda n: jnp.maximum(n - 1, 0)
    return _fused_call(
        functools.partial(_attn_kernel, n_kv=n_kv, group=group),
        name="band_attention",
        grid=(m // QBLOCK, 1),
        j_outer=False,
        in_specs=[
            pl.BlockSpec(memory_space=pltpu.SMEM),
            pl.BlockSpec((QBLOCK, dq), lambda n, j: (n, 0)),
            pl.BlockSpec((QBLOCK, dkv), lambda n, j: (n, kblk)),
            pl.BlockSpec((QBLOCK, dkv), lambda n, j: (prev(n), kblk)),
            pl.BlockSpec((QBLOCK, dkv), lambda n, j: (n, kblk + 1)),
            pl.BlockSpec((QBLOCK, dkv), lambda n, j: (prev(n), kblk + 1)),
            pl.BlockSpec((n_heads, QBLOCK, 2 * QBLOCK), lambda n, j: (0, 0, 0)),
            pl.BlockSpec((1, 2 * HEAD_DIM), lambda n, j: (0, 0)),
            pl.BlockSpec((1, 2 * HEAD_DIM), lambda n, j: (0, 0)),
        ],
        inputs=(sinks, qkv, qkv, qkv, qkv, qkv, bias, q_gain, k_gain),
        out_spec=pl.BlockSpec((QBLOCK, dq), lambda n, j: (n, 0)),
        out_shape=jax.ShapeDtypeStruct((m, dq), BF16),
        casts=casts,
    )


def _matmul_residual_body(a_ref, w_ref, x_ref, o_ref):
    o_ref[...] = x_ref[...] + _dot(a_ref[...], w_ref[...])


def _matmul_residual(a, w, x, *, tm, tn, casts):
    m, k = a.shape
    n = w.shape[0] * tn
    return _fused_call(
        _matmul_residual_body,
        name="attn_out_proj",
        grid=(n // tn, m // tm),
        j_outer=True,
        in_specs=[
            pl.BlockSpec((tm, k), lambda j, i: (i, 0)),
            pl.BlockSpec((None, k, tn), lambda j, i: (j, 0, 0)),
            pl.BlockSpec((tm, tn), lambda j, i: (i, j)),
        ],
        inputs=(a, w, x),
        out_spec=pl.BlockSpec((tm, tn), lambda j, i: (i, j)),
        out_shape=jax.ShapeDtypeStruct((m, n), F32),
        casts=casts,
    )


def _ffn_body(x_ref, g_ref, wg_ref, wu_ref, wd_ref, p_ref, pg_ref, wpg_ref, bpg_ref, wpp_ref, o_ref, h_ref):
    j = pl.program_id(1)

    @pl.when(j == 0)
    def _():
        x = x_ref[...]
        h_ref[...] = _rms_norm(x, g_ref[...]).astype(BF16)
        o_ref[...] = x

    h = h_ref[...]
    gate = _dot(h, wg_ref[...])
    up = _dot(h, wu_ref[...])
    act = (gate * jax.nn.sigmoid(gate) * up).astype(BF16)
    o_ref[...] += _dot(act, wd_ref[...])

    @pl.when(j == pl.num_programs(1) - 1)
    def _():
        x1 = o_ref[...]
        h1 = _rms_norm(x1, pg_ref[...]).astype(BF16)
        pb = p_ref[...].astype(BF16)
        tn = wpg_ref.shape[-1]
        for t in range(wpg_ref.shape[0]):
            cols = slice(t * tn, (t + 1) * tn)
            pgate = jax.nn.sigmoid(_dot(h1, wpg_ref[t]) + bpg_ref[:, cols])
            o_ref[:, cols] = x1[:, cols] + pgate * _dot(pb, wpp_ref[t])


def _ffn(x, gain, w_gate, w_up, w_down, p, layer, ple_gain, w_pgate, b_pgate, w_pproj, *, tm, tf, casts):
    m, d = x.shape
    f = w_gate.shape[0] * tf
    pd = p.shape[-1]
    resident = lambda a: pl.BlockSpec(a.shape, lambda i, j: (0,) * a.ndim, pipeline_mode=pl.Buffered(1))
    return _fused_call(
        _ffn_body,
        name="swiglu_ffn_gated_embedding",
        grid=(m // tm, f // tf),
        j_outer=False,
        in_specs=[
            pl.BlockSpec((tm, d), lambda i, j: (i, 0)),
            pl.BlockSpec((1, d), lambda i, j: (0, 0)),
            pl.BlockSpec((None, d, tf), lambda i, j: (j, 0, 0)),
            pl.BlockSpec((None, d, tf), lambda i, j: (j, 0, 0)),
            pl.BlockSpec((None, tf, d), lambda i, j: (0, j, 0)),
            pl.BlockSpec((None, tm, pd), lambda i, j: (layer, i, 0)),
            pl.BlockSpec((1, d), lambda i, j: (0, 0)),
            resident(w_pgate),
            pl.BlockSpec((1, d), lambda i, j: (0, 0)),
            resident(w_pproj),
        ],
        inputs=(x, gain, w_gate, w_up, w_down, p, ple_gain, w_pgate, b_pgate, w_pproj),
        out_spec=pl.BlockSpec((tm, d), lambda i, j: (i, 0)),
        out_shape=jax.ShapeDtypeStruct((m, d), F32),
        scratch=[pltpu.VMEM((tm, d), BF16)],
        casts=casts,
    )


def kernel(x, p, norm_mix, norm_ffn, norm_ple, conv_w_in, conv_b_in, conv_w_dw, conv_b_dw, conv_ln_g,
           conv_ln_b, conv_w_out, conv_b_out, pool_w, pool_scale, attn_w_qkv, attn_q_norm, attn_k_norm,
           attn_sinks, attn_w_o, rel_bias, ffn_w_gate, ffn_w_up, ffn_w_down, ple_w_proj, ple_w_gate,
           ple_b_gate):
    batch, seq, d = x.shape
    depth = norm_mix.shape[0]
    n_heads = d // HEAD_DIM
    n_kv = (attn_w_qkv.shape[-1] // HEAD_DIM - n_heads) // 2
    group = n_heads // n_kv
    assert n_kv % 2 == 0 and group % 2 == 0, "attention kernel pairs heads into 128-lane tiles"
    assert d % (len(POOL_WINDOWS) * V7X_LANES) == 0
    m = batch * seq
    assert batch == 1 and m % QBLOCK == 0, "row tiles assume one sequence"

    tm = _tile(m, 512)
    tn = _tile(d, 1024)
    tf = _tile(ffn_w_gate.shape[-1], 512)
    tq = _tile(attn_w_qkv.shape[-1], 1024)
    tp = _tile(m, 512)
    row = lambda v: v.reshape(1, -1)

    pool_w2 = pool_w.reshape(pool_w.shape[0], -1, pool_w.shape[-1])

    def mixer_weights(i):
        kind, l = i % N_MIXERS, i // N_MIXERS
        if kind == 0:
            return {("conv_in", i): (conv_w_in, l, tn), ("conv_out", i): (conv_w_out, l, tn)}
        if kind == 1:
            return {("pool", i): (pool_w2, l, pool_w2.shape[-1])}
        return {("qkv", i): (attn_w_qkv, l, tq), ("attn_o", i): (attn_w_o, l, tn)}

    def ffn_weights(i):
        return {("gate", i): (ffn_w_gate, i, tf), ("up", i): (ffn_w_up, i, tf), ("down", i): (ffn_w_down, i, d)}

    def ple_weights(i):
        return {("ple_gate", i): (ple_w_gate, i, tn), ("ple_proj", i): (ple_w_proj, i, tn)}

    ready = {}

    def bf16(key, source):
        return ready.pop(key) if key in ready else _cast(*source)

    def hosting(weights, call):
        out, copies = call(list(weights.values()))
        ready.update(zip(weights.keys(), copies))
        return out

    xs = x.reshape(m, d)
    bias = None
    for i in range(depth):
        kind, l = i % N_MIXERS, i // N_MIXERS
        gain = row(norm_mix[i])
        mine = {} if ("gate", i) in ready else {**ffn_weights(i), **ple_weights(i)}
        if kind == 0:
            w_in, w_out = (bf16(k, w) for k, w in mixer_weights(i).items())
            u = hosting(mine, lambda c: _conv_in(xs, gain, w_in, row(conv_b_in[l]), tm=tm, tn=tn, casts=c))
            xs = _conv_out(u, xs, conv_w_dw[l], row(conv_b_dw[l]), row(conv_ln_g[l]), row(conv_ln_b[l]),
                           w_out, row(conv_b_out[l]), tm=tm)
        elif kind == 1:
            (w_grp,) = (bf16(k, w) for k, w in mixer_weights(i).items())
            w_grp = w_grp.reshape(pool_w.shape[1:])
            xs = _pool(xs, gain, w_grp, row(pool_scale[l]), tm=tp)
        else:
            if bias is None:
                bias = _band_bias(rel_bias)
            w_qkv, w_o = (bf16(k, w) for k, w in mixer_weights(i).items())
            qkv, _ = _norm_matmul(xs, gain, w_qkv, tm=tm, tn=tq, casts=[])
            o = hosting(mine, lambda c: _attention(
                qkv, bias, attn_sinks[l], row(jnp.tile(attn_q_norm[l], 2)), row(jnp.tile(attn_k_norm[l], 2)),
                n_heads=n_heads, n_kv=n_kv, casts=c))
            xs, _ = _matmul_residual(o, w_o, xs, tm=tm, tn=tn, casts=[])
        w_gate, w_up, w_down = (bf16(k, w) for k, w in ffn_weights(i).items())
        w_pg, w_pp = (bf16(k, w) for k, w in ple_weights(i).items())
        ahead = {}
        if i + 1 < depth:
            ahead = mixer_weights(i + 1)
            if (i + 1) % N_MIXERS == 1:
                ahead = {**ahead, **ffn_weights(i + 1), **ple_weights(i + 1)}
        xs = hosting(ahead, lambda c: _ffn(
            xs, row(norm_ffn[i]), w_gate, w_up, w_down, p.reshape(depth, m, -1), i, row(norm_ple[i]),
            w_pg, row(ple_b_gate[i]), w_pp, tm=tm, tf=tf, casts=c))
    return xs.reshape(batch, seq, d)
```

```python
import functools
import math

import jax
import jax.numpy as jnp
from jax import lax
from jax.experimental import pallas as pl
from jax.experimental.pallas import tpu as pltpu

N_MIXERS = 3
CHUNK = 64
CONV_WIDTH = 31
POOL_WINDOWS = (2, 4, 8, 16)
HEAD_DIM = 64
WINDOW_CHUNKS = 2
QBLOCK = 128
NUM_BUCKETS = 32
REL_MAX_DIST = 128
EPS = 1e-6
NEG_INF = -1e30

V7X_VMEM_BYTES = 64 * 1024 * 1024
V7X_LANES = 128
V7X_SUBLANES = 8
BF16_ROWS = 2 * V7X_SUBLANES
VMEM_LIMIT_BYTES = V7X_VMEM_BYTES * 7 // 8

CONV_HALO = 32
POOL_HALO = 16
CONV_STRIP = V7X_LANES
NORM_ROWS = 64

BF16 = jnp.bfloat16
F32 = jnp.float32


def _params(*semantics):
    return pltpu.CompilerParams(dimension_semantics=semantics, vmem_limit_bytes=VMEM_LIMIT_BYTES)


def _rms_norm(x, g):
    ms = jnp.mean(x * x, axis=-1, keepdims=True)
    return x * lax.rsqrt(ms + EPS) * g


def _dot(a, b):
    return jnp.dot(a, b, preferred_element_type=F32)


def _tile(n, target, multiple=V7X_LANES):
    if n <= target:
        return n
    t = target - target % multiple
    while n % t:
        t -= multiple
    return t


def _copy_tiles(src_ref, dst_ref):
    tile = dst_ref.shape[-1]
    for t in range(dst_ref.shape[0]):
        dst_ref[t] = src_ref[:, t * tile:(t + 1) * tile].astype(BF16)


def _cast(stack, layer, tile):
    _, r, c = stack.shape
    rb = _tile(r, 256, BF16_ROWS)
    return pl.pallas_call(
        _copy_tiles,
        grid=(r // rb,),
        in_specs=[pl.BlockSpec((None, rb, c), lambda i: (layer, i, 0))],
        out_specs=pl.BlockSpec((c // tile, rb, tile), lambda i: (0, i, 0)),
        out_shape=jax.ShapeDtypeStruct((c // tile, r, tile), BF16),
        compiler_params=_params("parallel"),
        name="weight_cast",
    )(stack)


def _cast_plan(sources, gi, gj, j_outer):
    in_specs, out_specs, out_shapes, row_only = [], [], [], []
    for stack, layer, tile in sources:
        _, r, c = stack.shape
        once = False
        if r % (gi * gj * BF16_ROWS) == 0:
            rows, cols = r // (gi * gj), c
            src_map = lambda i, j: (i * gj + j, 0)
            dst_map = lambda i, j: (0, i * gj + j, 0)
        elif r % (gi * BF16_ROWS) == 0 and c % gj == 0 and (c // gj) % tile == 0:
            rows, cols = r // gi, c // gj
            src_map = lambda i, j: (i, j)
            dst_map = lambda i, j: (j, i, 0)
        elif r % (gj * BF16_ROWS) == 0 and c % gi == 0 and (c // gi) % tile == 0:
            rows, cols = r // gj, c // gi
            src_map = lambda i, j: (j, i)
            dst_map = lambda i, j: (i, j, 0)
        elif r % (gi * BF16_ROWS) == 0 and not j_outer:
            rows, cols = r // gi, c
            src_map = lambda i, j: (i, 0)
            dst_map = lambda i, j: (0, i, 0)
            once = gj > 1
        else:
            n = max(k for k in range(1, gi * gj + 1) if r % (k * BF16_ROWS) == 0)
            step = (lambda i, j: j * gi + i) if j_outer else (lambda i, j: i * gj + j)
            rows, cols = r // n, c
            src_map = lambda i, j, n=n, step=step: (jnp.minimum(step(i, j), n - 1), 0)
            dst_map = lambda i, j, n=n, step=step: (0, jnp.minimum(step(i, j), n - 1), 0)
        if j_outer:
            src_map = (lambda f: lambda j, i: f(i, j))(src_map)
            dst_map = (lambda f: lambda j, i: f(i, j))(dst_map)
        src_map = (lambda f, l: lambda a, b: (l, *f(a, b)))(src_map, layer)
        in_specs.append(pl.BlockSpec((None, rows, cols), src_map))
        out_specs.append(pl.BlockSpec((cols // tile, rows, tile), dst_map))
        out_shapes.append(jax.ShapeDtypeStruct((c // tile, r, tile), BF16))
        row_only.append(once)
    return in_specs, out_specs, out_shapes, row_only


def _fused_call(body, *, name, grid, j_outer, in_specs, inputs, out_spec, out_shape, scratch=(), casts=()):
    n_in, n_cast = len(inputs), len(casts)
    gi, gj = (grid[1], grid[0]) if j_outer else grid
    cast_in, cast_out, cast_shapes, row_only = _cast_plan(casts, gi, gj, j_outer)

    def kernel(*refs):
        ins, rest = refs[:n_in], refs[n_in:]
        cast_src, rest = rest[:n_cast], rest[n_cast:]
        out, rest = rest[0], rest[1:]
        cast_dst, scr = rest[:n_cast], rest[n_cast:]
        for src, dst, once in zip(cast_src, cast_dst, row_only):
            if once:
                pl.when(pl.program_id(1) == 0)(functools.partial(_copy_tiles, src, dst))
            else:
                _copy_tiles(src, dst)
        body(*ins, out, *scr)

    res = pl.pallas_call(
        kernel,
        grid=grid,
        in_specs=[*in_specs, *cast_in],
        out_specs=[out_spec, *cast_out],
        out_shape=[out_shape, *cast_shapes],
        scratch_shapes=list(scratch),
        compiler_params=_params("parallel", "parallel" if j_outer else "arbitrary"),
        name=name,
    )(*inputs, *[stack for stack, _, _ in casts])
    return res[0], list(res[1:])


def _conv_in_body(x_ref, g_ref, wa_ref, wg_ref, ba_ref, bg_ref, u_ref):
    h = _rms_norm(x_ref[...], g_ref[...]).astype(BF16)
    a = _dot(h, wa_ref[...]) + ba_ref[...]
    gate = _dot(h, wg_ref[...]) + bg_ref[...]
    u_ref[...] = a * jax.nn.sigmoid(gate)


def _conv_in(x, gain, w_in, b_in, *, tm, tn, casts):
    m, d = x.shape
    nj = d // tn
    return _fused_call(
        _conv_in_body,
        name="conv_in",
        grid=(nj, m // tm),
        j_outer=True,
        in_specs=[
            pl.BlockSpec((tm, d), lambda j, i: (i, 0)),
            pl.BlockSpec((1, d), lambda j, i: (0, 0)),
            pl.BlockSpec((None, d, tn), lambda j, i: (j, 0, 0)),
            pl.BlockSpec((None, d, tn), lambda j, i: (j + nj, 0, 0)),
            pl.BlockSpec((1, tn), lambda j, i: (0, j)),
            pl.BlockSpec((1, tn), lambda j, i: (0, j + nj)),
        ],
        inputs=(x, gain, w_in, w_in, b_in, b_in),
        out_spec=pl.BlockSpec((tm, tn), lambda j, i: (i, j)),
        out_shape=jax.ShapeDtypeStruct((m, d), F32),
        casts=casts,
    )


def _conv_out_kernel(u_ref, uprev_ref, wdw_ref, bdw_ref, lng_ref, lnb_ref, wout_ref, bout_ref, x_ref,
                     o_ref, ext_ref, conv_ref, v_ref, *, tm):
    i = pl.program_id(0)
    d = u_ref.shape[1]
    first = CONV_HALO - (CONV_WIDTH - 1)

    for s in range(d // CONV_STRIP):
        cols = slice(s * CONV_STRIP, (s + 1) * CONV_STRIP)
        ext_ref[s, 0:CONV_HALO, :] = jnp.where(i == 0, 0.0, uprev_ref[:, cols])
        ext_ref[s, CONV_HALO:, :] = u_ref[:, cols]
        acc = jnp.broadcast_to(bdw_ref[:, cols], (tm, CONV_STRIP))
        for k in range(CONV_WIDTH):
            acc = acc + wdw_ref[k:k + 1, cols] * ext_ref[s, first + k:first + k + tm, :]
        conv_ref[:, cols] = acc

    def norm_rows(c, carry):
        r0 = pl.multiple_of(c * NORM_ROWS, NORM_ROWS)
        conv = conv_ref[pl.ds(r0, NORM_ROWS), :]
        mu = jnp.mean(conv, axis=-1, keepdims=True)
        xc = conv - mu
        y = xc * lax.rsqrt(jnp.mean(xc * xc, axis=-1, keepdims=True) + EPS)
        y = y * lng_ref[...] + lnb_ref[...]
        v_ref[pl.ds(r0, NORM_ROWS), :] = (y * jax.nn.sigmoid(y)).astype(BF16)
        return carry

    lax.fori_loop(0, tm // NORM_ROWS, norm_rows, 0)

    v = v_ref[...]
    tn = wout_ref.shape[-1]
    for t in range(wout_ref.shape[0]):
        cols = slice(t * tn, (t + 1) * tn)
        o_ref[:, cols] = x_ref[:, cols] + _dot(v, wout_ref[t]) + bout_ref[:, cols]


def _conv_out(u, x, w_dw, b_dw, ln_g, ln_b, w_out, b_out, *, tm):
    m, d = u.shape
    hb = tm // CONV_HALO
    const = lambda i: (0, 0)
    return pl.pallas_call(
        functools.partial(_conv_out_kernel, tm=tm),
        grid=(m // tm,),
        in_specs=[
            pl.BlockSpec((tm, d), lambda i: (i, 0)),
            pl.BlockSpec((CONV_HALO, d), lambda i: (jnp.maximum(i * hb - 1, 0), 0)),
            pl.BlockSpec((CONV_WIDTH, d), const),
            pl.BlockSpec((1, d), const),
            pl.BlockSpec((1, d), const),
            pl.BlockSpec((1, d), const),
            pl.BlockSpec(w_out.shape, lambda i: (0, 0, 0), pipeline_mode=pl.Buffered(1)),
            pl.BlockSpec((1, d), const),
            pl.BlockSpec((tm, d), lambda i: (i, 0)),
        ],
        out_specs=pl.BlockSpec((tm, d), lambda i: (i, 0)),
        out_shape=jax.ShapeDtypeStruct((m, d), F32),
        scratch_shapes=[
            pltpu.VMEM((d // CONV_STRIP, tm + CONV_HALO, CONV_STRIP), F32),
            pltpu.VMEM((tm, d), F32),
            pltpu.VMEM((tm, d), BF16),
        ],
        compiler_params=_params("parallel"),
        name="conv_out",
    )(u, u, w_dw, b_dw, ln_g, ln_b, w_out, b_out, x)


def _pool_kernel(x_ref, xprev_ref, g_ref, w_ref, scale_ref, o_ref, ext_ref, *, tm):
    i = pl.program_id(0)
    d = x_ref.shape[1]
    gc = d // len(POOL_WINDOWS)
    x = x_ref[...]
    h = _rms_norm(x, g_ref[...])
    hprev = jnp.where(i == 0, 0.0, _rms_norm(xprev_ref[...], g_ref[...]))
    t = i * tm + lax.broadcasted_iota(jnp.int32, (tm, 1), 0)
    strips = gc // V7X_LANES
    for g, w in enumerate(POOL_WINDOWS):
        cnt = jnp.minimum(t + 1, w).astype(F32)
        mixes = []
        for s in range(g * strips, (g + 1) * strips):
            lanes = slice(s * V7X_LANES, (s + 1) * V7X_LANES)
            ext_ref[s, 0:POOL_HALO, :] = hprev[:, lanes]
            ext_ref[s, POOL_HALO:, :] = h[:, lanes]
            total = h[:, lanes]
            for k in range(1, w):
                total = total + ext_ref[s, POOL_HALO - k:POOL_HALO - k + tm, :]
            mixes.append(total / cnt - h[:, lanes])
        cols = slice(g * gc, (g + 1) * gc)
        mix = jnp.concatenate(mixes, axis=1).astype(BF16)
        y = _dot(mix, w_ref[g]) * scale_ref[:, cols]
        o_ref[:, cols] = x[:, cols] + y


def _pool(x, gain, w_grp, scale, *, tm):
    m, d = x.shape
    ng, gc, _ = w_grp.shape
    hb = tm // POOL_HALO
    kernel = functools.partial(_pool_kernel, tm=tm)
    return pl.pallas_call(
        kernel,
        grid=(m // tm,),
        in_specs=[
            pl.BlockSpec((tm, d), lambda i: (i, 0)),
            pl.BlockSpec((POOL_HALO, d), lambda i: (jnp.maximum(i * hb - 1, 0), 0)),
            pl.BlockSpec((1, d), lambda i: (0, 0)),
            pl.BlockSpec((ng, gc, gc), lambda i: (0, 0, 0)),
            pl.BlockSpec((1, d), lambda i: (0, 0)),
        ],
        out_specs=pl.BlockSpec((tm, d), lambda i: (i, 0)),
        out_shape=jax.ShapeDtypeStruct((m, d), F32),
        scratch_shapes=[pltpu.VMEM((d // V7X_LANES, tm + POOL_HALO, V7X_LANES), F32)],
        compiler_params=_params("parallel"),
        name="pool",
    )(x, x, gain, w_grp, scale)


def _norm_matmul_body(x_ref, g_ref, w_ref, o_ref):
    o_ref[...] = _dot(_rms_norm(x_ref[...], g_ref[...]).astype(BF16), w_ref[...])


def _norm_matmul(x, gain, w, *, tm, tn, casts):
    m, d = x.shape
    n = w.shape[0] * tn
    return _fused_call(
        _norm_matmul_body,
        name="qkv_proj",
        grid=(n // tn, m // tm),
        j_outer=True,
        in_specs=[
            pl.BlockSpec((tm, d), lambda j, i: (i, 0)),
            pl.BlockSpec((1, d), lambda j, i: (0, 0)),
            pl.BlockSpec((None, d, tn), lambda j, i: (j, 0, 0)),
        ],
        inputs=(x, gain, w),
        out_spec=pl.BlockSpec((tm, tn), lambda j, i: (i, j)),
        out_shape=jax.ShapeDtypeStruct((m, n), F32),
        casts=casts,
    )


def _t5_bucket(rel):
    nb = NUM_BUCKETS // 2
    n = -rel
    ret = jnp.where(n < 0, nb, 0)
    n = jnp.abs(n)
    max_exact = nb // 2
    nf = jnp.maximum(n, 1).astype(F32)
    large = max_exact + (jnp.log(nf / max_exact) / math.log(REL_MAX_DIST / max_exact)
                         * (nb - max_exact)).astype(jnp.int32)
    large = jnp.minimum(large, nb - 1)
    return ret + jnp.where(n < max_exact, n, large)


def _band_bias_kernel(bucket_ref, rel_bias_ref, o_ref):
    h = pl.program_id(0)
    bucket = bucket_ref[...]
    acc = jnp.zeros(bucket.shape, F32)
    for b in range(NUM_BUCKETS):
        acc = jnp.where(bucket == b, rel_bias_ref[b, h], acc)
    o_ref[0] = acc


def _band_bias(rel_bias):
    n_heads = rel_bias.shape[1]
    i = jnp.arange(QBLOCK)[:, None]
    j = jnp.arange(2 * QBLOCK)[None, :]
    bucket = _t5_bucket(j - QBLOCK - i).astype(jnp.int32)
    return pl.pallas_call(
        _band_bias_kernel,
        grid=(n_heads,),
        in_specs=[
            pl.BlockSpec((QBLOCK, 2 * QBLOCK), lambda h: (0, 0)),
            pl.BlockSpec(memory_space=pltpu.SMEM),
        ],
        out_specs=pl.BlockSpec((1, QBLOCK, 2 * QBLOCK), lambda h: (h, 0, 0)),
        out_shape=jax.ShapeDtypeStruct((n_heads, QBLOCK, 2 * QBLOCK), F32),
        compiler_params=_params("parallel"),
        name="band_bias",
    )(bucket, rel_bias)


def _half_rms_norm(x, g, low):
    sq = x * x
    ms_lo = jnp.sum(jnp.where(low, sq, 0.0), axis=-1, keepdims=True) * (1.0 / HEAD_DIM)
    ms_hi = jnp.sum(jnp.where(low, 0.0, sq), axis=-1, keepdims=True) * (1.0 / HEAD_DIM)
    r = jnp.where(low, lax.rsqrt(ms_lo + EPS), lax.rsqrt(ms_hi + EPS))
    return x * r * g


def _attn_kernel(sinks_ref, q_ref, kc_ref, kp_ref, vc_ref, vp_ref, bias_ref, qg_ref, kg_ref, o_ref,
                 *, n_kv, group):
    n = pl.program_id(0)
    pair = 2 * HEAD_DIM
    n_heads = n_kv * group
    qi = lax.broadcasted_iota(jnp.int32, (QBLOCK, 2 * QBLOCK), 0)
    kj = lax.broadcasted_iota(jnp.int32, (QBLOCK, 2 * QBLOCK), 1)
    qc = qi // CHUNK
    kc = kj // CHUNK - QBLOCK // CHUNK
    visible = (kc <= qc) & (kc >= qc - WINDOW_CHUNKS) & ((n > 0) | (kj >= QBLOCK))
    low = lax.broadcasted_iota(jnp.int32, (1, pair), 1) < HEAD_DIM
    scale = HEAD_DIM ** -0.5

    q_all = jnp.concatenate([q_ref[:, c * pair:(c + 1) * pair] for c in range(n_heads // 2)], axis=0)
    q_all = _half_rms_norm(q_all, qg_ref[...], low) * scale
    k_all = jnp.concatenate([ref[:, c * pair:(c + 1) * pair]
                             for c in range(n_kv // 2) for ref in (kp_ref, kc_ref)], axis=0)
    k_all = _half_rms_norm(k_all, kg_ref[...], low).astype(BF16)

    logits, values = [], []
    heads_per_tile = 2 * group
    for hp in range(n_kv // 2):
        kcols = slice(hp * pair, (hp + 1) * pair)
        values.append(jnp.concatenate([vp_ref[:, kcols], vc_ref[:, kcols]], axis=0).astype(BF16))
        stacked = []
        for head in range(hp * heads_per_tile, (hp + 1) * heads_per_tile):
            qhalf, half = head % 2, (head // group) % 2
            q2 = q_all[(head // 2) * QBLOCK:(head // 2 + 1) * QBLOCK]
            qz = jnp.where(low == (qhalf == 0), q2, 0.0)
            if qhalf != half:
                qz = pltpu.roll(qz, HEAD_DIM, axis=1)
            stacked.append(qz.astype(BF16))
        k2 = k_all[hp * 2 * QBLOCK:(hp + 1) * 2 * QBLOCK]
        s = lax.dot_general(jnp.concatenate(stacked, axis=0), k2, (((1,), (1,)), ((), ())),
                            preferred_element_type=F32)
        for g in range(heads_per_tile):
            rows = s[g * QBLOCK:(g + 1) * QBLOCK]
            logits.append(jnp.where(visible, rows + bias_ref[hp * heads_per_tile + g], NEG_INF))

    s = jnp.concatenate(logits, axis=0)
    sink = jnp.concatenate([jnp.full((QBLOCK, 1), sinks_ref[h], F32) for h in range(n_heads)], axis=0)
    mx = jnp.maximum(jnp.max(s, axis=-1, keepdims=True), sink)
    e = jnp.exp(s - mx)
    denom = jnp.sum(e, axis=-1, keepdims=True) + jnp.exp(sink - mx)
    r = 1.0 / denom
    e = e.astype(BF16)

    for hp in range(n_kv // 2):
        rows = slice(hp * heads_per_tile * QBLOCK, (hp + 1) * heads_per_tile * QBLOCK)
        o = _dot(e[rows], values[hp]) * r[rows]
        for c in range(group):
            halves = []
            for qhalf in range(2):
                g = 2 * c + qhalf
                o2 = o[g * QBLOCK:(g + 1) * QBLOCK]
                if qhalf != (g // group) % 2:
                    o2 = pltpu.roll(o2, HEAD_DIM, axis=1)
                halves.append(o2)
            col = (hp * group + c) * pair
            o_ref[:, col:col + pair] = jnp.where(low, halves[0], halves[1]).astype(BF16)


def _attention(qkv, bias, sinks, q_gain, k_gain, *, n_heads, n_kv, casts):
    m = qkv.shape[0]
    group = n_heads // n_kv
    dq = n_heads * HEAD_DIM
    dkv = n_kv * HEAD_DIM
    kblk = dq // dkv
    prev = lambda n: jnp.maximum(n - 1, 0)
    return _fused_call(
        functools.partial(_attn_kernel, n_kv=n_kv, group=group),
        name="band_attention",
        grid=(m // QBLOCK, 1),
        j_outer=False,
        in_specs=[
            pl.BlockSpec(memory_space=pltpu.SMEM),
            pl.BlockSpec((QBLOCK, dq), lambda n, j: (n, 0)),
            pl.BlockSpec((QBLOCK, dkv), lambda n, j: (n, kblk)),
            pl.BlockSpec((QBLOCK, dkv), lambda n, j: (prev(n), kblk)),
            pl.BlockSpec((QBLOCK, dkv), lambda n, j: (n, kblk + 1)),
            pl.BlockSpec((QBLOCK, dkv), lambda n, j: (prev(n), kblk + 1)),
            pl.BlockSpec((n_heads, QBLOCK, 2 * QBLOCK), lambda n, j: (0, 0, 0)),
            pl.BlockSpec((1, 2 * HEAD_DIM), lambda n, j: (0, 0)),
            pl.BlockSpec((1, 2 * HEAD_DIM), lambda n, j: (0, 0)),
        ],
        inputs=(sinks, qkv, qkv, qkv, qkv, qkv, bias, q_gain, k_gain),
        out_spec=pl.BlockSpec((QBLOCK, dq), lambda n, j: (n, 0)),
        out_shape=jax.ShapeDtypeStruct((m, dq), BF16),
        casts=casts,
    )


def _matmul_residual_body(a_ref, w_ref, x_ref, o_ref):
    o_ref[...] = x_ref[...] + _dot(a_ref[...], w_ref[...])


def _matmul_residual(a, w, x, *, tm, tn, casts):
    m, k = a.shape
    n = w.shape[0] * tn
    return _fused_call(
        _matmul_residual_body,
        name="attn_out_proj",
        grid=(n // tn, m // tm),
        j_outer=True,
        in_specs=[
            pl.BlockSpec((tm, k), lambda j, i: (i, 0)),
            pl.BlockSpec((None, k, tn), lambda j, i: (j, 0, 0)),
            pl.BlockSpec((tm, tn), lambda j, i: (i, j)),
        ],
        inputs=(a, w, x),
        out_spec=pl.BlockSpec((tm, tn), lambda j, i: (i, j)),
        out_shape=jax.ShapeDtypeStruct((m, n), F32),
        casts=casts,
    )


def _ffn_body(x_ref, g_ref, wg_ref, wu_ref, wd_ref, p_ref, pg_ref, wpg_ref, bpg_ref, wpp_ref, o_ref, h_ref):
    j = pl.program_id(1)

    @pl.when(j == 0)
    def _():
        x = x_ref[...]
        h_ref[...] = _rms_norm(x, g_ref[...]).astype(BF16)
        o_ref[...] = x

    h = h_ref[...]
    gate = _dot(h, wg_ref[...])
    up = _dot(h, wu_ref[...])
    act = (gate * jax.nn.sigmoid(gate) * up).astype(BF16)
    o_ref[...] += _dot(act, wd_ref[...])

    @pl.when(j == pl.num_programs(1) - 1)
    def _():
        x1 = o_ref[...]
        h1 = _rms_norm(x1, pg_ref[...]).astype(BF16)
        pb = p_ref[...].astype(BF16)
        tn = wpg_ref.shape[-1]
        for t in range(wpg_ref.shape[0]):
            cols = slice(t * tn, (t + 1) * tn)
            pgate = jax.nn.sigmoid(_dot(h1, wpg_ref[t]) + bpg_ref[:, cols])
            o_ref[:, cols] = x1[:, cols] + pgate * _dot(pb, wpp_ref[t])


def _ffn(x, gain, w_gate, w_up, w_down, p, layer, ple_gain, w_pgate, b_pgate, w_pproj, *, tm, tf, casts):
    m, d = x.shape
    f = w_gate.shape[0] * tf
    pd = p.shape[-1]
    resident = lambda a: pl.BlockSpec(a.shape, lambda i, j: (0,) * a.ndim, pipeline_mode=pl.Buffered(1))
    return _fused_call(
        _ffn_body,
        name="swiglu_ffn_gated_embedding",
        grid=(m // tm, f // tf),
        j_outer=False,
        in_specs=[
            pl.BlockSpec((tm, d), lambda i, j: (i, 0)),
            pl.BlockSpec((1, d), lambda i, j: (0, 0)),
            pl.BlockSpec((None, d, tf), lambda i, j: (j, 0, 0)),
            pl.BlockSpec((None, d, tf), lambda i, j: (j, 0, 0)),
            pl.BlockSpec((None, tf, d), lambda i, j: (0, j, 0)),
            pl.BlockSpec((None, tm, pd), lambda i, j: (layer, i, 0)),
            pl.BlockSpec((1, d), lambda i, j: (0, 0)),
            resident(w_pgate),
            pl.BlockSpec((1, d), lambda i, j: (0, 0)),
            resident(w_pproj),
        ],
        inputs=(x, gain, w_gate, w_up, w_down, p, ple_gain, w_pgate, b_pgate, w_pproj),
        out_spec=pl.BlockSpec((tm, d), lambda i, j: (i, 0)),
        out_shape=jax.ShapeDtypeStruct((m, d), F32),
        scratch=[pltpu.VMEM((tm, d), BF16)],
        casts=casts,
    )


def kernel(x, p, norm_mix, norm_ffn, norm_ple, conv_w_in, conv_b_in, conv_w_dw, conv_b_dw, conv_ln_g,
           conv_ln_b, conv_w_out, conv_b_out, pool_w, pool_scale, attn_w_qkv, attn_q_norm, attn_k_norm,
           attn_sinks, attn_w_o, rel_bias, ffn_w_gate, ffn_w_up, ffn_w_down, ple_w_proj, ple_w_gate,
           ple_b_gate):
    batch, seq, d = x.shape
    depth = norm_mix.shape[0]
    n_heads = d // HEAD_DIM
    n_kv = (attn_w_qkv.shape[-1] // HEAD_DIM - n_heads) // 2
    group = n_heads // n_kv
    assert n_kv % 2 == 0 and group % 2 == 0, "attention kernel pairs heads into 128-lane tiles"
    assert d % (len(POOL_WINDOWS) * V7X_LANES) == 0
    m = batch * seq
    assert batch == 1 and m % QBLOCK == 0, "row tiles assume one sequence"

    tm = _tile(m, 512)
    tm2 = _tile(m, 1024)
    tn = _tile(d, 1024)
    tf = _tile(ffn_w_gate.shape[-1], 512)
    tq = _tile(attn_w_qkv.shape[-1], 1024)
    tp = _tile(m, 512)
    row = lambda v: v.reshape(1, -1)

    pool_w2 = pool_w.reshape(pool_w.shape[0], -1, pool_w.shape[-1])

    def mixer_weights(i):
        kind, l = i % N_MIXERS, i // N_MIXERS
        if kind == 0:
            return {("conv_in", i): (conv_w_in, l, tn), ("conv_out", i): (conv_w_out, l, tn)}
        if kind == 1:
            return {("pool", i): (pool_w2, l, pool_w2.shape[-1])}
        return {("qkv", i): (attn_w_qkv, l, tq), ("attn_o", i): (attn_w_o, l, tn)}

    def ffn_weights(i):
        return {("gate", i): (ffn_w_gate, i, tf), ("up", i): (ffn_w_up, i, tf), ("down", i): (ffn_w_down, i, d)}

    def ple_weights(i):
        return {("ple_gate", i): (ple_w_gate, i, tn), ("ple_proj", i): (ple_w_proj, i, tn)}

    ready = {}

    def bf16(key, source):
        return ready.pop(key) if key in ready else _cast(*source)

    def hosting(weights, call):
        out, copies = call(list(weights.values()))
        ready.update(zip(weights.keys(), copies))
        return out

    xs = x.reshape(m, d)
    bias = None
    for i in range(depth):
        kind, l = i % N_MIXERS, i // N_MIXERS
        gain = row(norm_mix[i])
        mine = {} if ("gate", i) in ready else {**ffn_weights(i), **ple_weights(i)}
        if kind == 0:
            w_in, w_out = (bf16(k, w) for k, w in mixer_weights(i).items())
            u = hosting(mine, lambda c: _conv_in(xs, gain, w_in, row(conv_b_in[l]), tm=tm, tn=tn, casts=c))
            xs = _conv_out(u, xs, conv_w_dw[l], row(conv_b_dw[l]), row(conv_ln_g[l]), row(conv_ln_b[l]),
                           w_out, row(conv_b_out[l]), tm=tm)
        elif kind == 1:
            (w_grp,) = (bf16(k, w) for k, w in mixer_weights(i).items())
            w_grp = w_grp.reshape(pool_w.shape[1:])
            xs = _pool(xs, gain, w_grp, row(pool_scale[l]), tm=tp)
        else:
            if bias is None:
                bias = _band_bias(rel_bias)
            w_qkv, w_o = (bf16(k, w) for k, w in mixer_weights(i).items())
            qkv, _ = _norm_matmul(xs, gain, w_qkv, tm=tm2, tn=tq, casts=[])
            o = hosting({**mine, **(mixer_weights(i + 1) if i + 1 < depth else {})}, lambda c: _attention(
                qkv, bias, attn_sinks[l], row(jnp.tile(attn_q_norm[l], 2)), row(jnp.tile(attn_k_norm[l], 2)),
                n_heads=n_heads, n_kv=n_kv, casts=c))
            xs, _ = _matmul_residual(o, w_o, xs, tm=tm2, tn=tn, casts=[])
        w_gate, w_up, w_down = (bf16(k, w) for k, w in ffn_weights(i).items())
        w_pg, w_pp = (bf16(k, w) for k, w in ple_weights(i).items())
        ahead = {}
        if i + 1 < depth:
            ahead = mixer_weights(i + 1)
            if (i + 1) % N_MIXERS == 1:
                ahead = {**ahead, **ffn_weights(i + 1), **ple_weights(i + 1)}
            ahead = {k: v for k, v in ahead.items() if k not in ready}
        xs = hosting(ahead, lambda c: _ffn(
            xs, row(norm_ffn[i]), w_gate, w_up, w_down, p.reshape(depth, m, -1), i, row(norm_ple[i]),
            w_pg, row(ple_b_gate[i]), w_pp, tm=tm, tf=tf, casts=c))
    return xs.reshape(batch, seq, d)
```

```python
import functools
import math

import jax
import jax.numpy as jnp
from jax import lax
from jax.experimental import pallas as pl
from jax.experimental.pallas import tpu as pltpu

N_MIXERS = 3
CHUNK = 64
CONV_WIDTH = 31
POOL_WINDOWS = (2, 4, 8, 16)
HEAD_DIM = 64
WINDOW_CHUNKS = 2
QBLOCK = 128
NUM_BUCKETS = 32
REL_MAX_DIST = 128
EPS = 1e-6
NEG_INF = -1e30

V7X_VMEM_BYTES = 64 * 1024 * 1024
V7X_LANES = 128
V7X_SUBLANES = 8
BF16_ROWS = 2 * V7X_SUBLANES
VMEM_LIMIT_BYTES = V7X_VMEM_BYTES * 7 // 8

CONV_HALO = 32
POOL_HALO = 16
CONV_STRIP = V7X_LANES

ROW_TILE = 512
WIDE_ROW_TILE = 1024
COL_TILE = 1024
FF_TILE = 512
CAST_ROWS = 256

BF16 = jnp.bfloat16
F32 = jnp.float32


def _params(*semantics):
    return pltpu.CompilerParams(dimension_semantics=semantics, vmem_limit_bytes=VMEM_LIMIT_BYTES)


def _rms_norm(x, g):
    ms = jnp.mean(x * x, axis=-1, keepdims=True)
    return x * lax.rsqrt(ms + EPS) * g


def _dot(a, b):
    return jnp.dot(a, b, preferred_element_type=F32)


def _tile(n, target, multiple=V7X_LANES):
    if n <= target:
        return n
    t = target - target % multiple
    while n % t:
        t -= multiple
    return t


def _copy_tiles(src_ref, dst_ref):
    tile = dst_ref.shape[-1]
    for t in range(dst_ref.shape[0]):
        dst_ref[t] = src_ref[:, t * tile:(t + 1) * tile].astype(BF16)


def _cast(stack, layer, tile):
    _, r, c = stack.shape
    rb = _tile(r, CAST_ROWS, BF16_ROWS)
    return pl.pallas_call(
        _copy_tiles,
        grid=(r // rb,),
        in_specs=[pl.BlockSpec((None, rb, c), lambda i: (layer, i, 0))],
        out_specs=pl.BlockSpec((c // tile, rb, tile), lambda i: (0, i, 0)),
        out_shape=jax.ShapeDtypeStruct((c // tile, r, tile), BF16),
        compiler_params=_params("parallel"),
        name="weight_cast",
    )(stack)


def _cast_plan(sources, gi, gj, j_outer):
    in_specs, out_specs, out_shapes, row_only = [], [], [], []
    for stack, layer, tile in sources:
        _, r, c = stack.shape
        once = False
        if r % (gi * gj * BF16_ROWS) == 0:
            rows, cols = r // (gi * gj), c
            src_map = lambda i, j: (i * gj + j, 0)
            dst_map = lambda i, j: (0, i * gj + j, 0)
        elif r % (gi * BF16_ROWS) == 0 and c % gj == 0 and (c // gj) % tile == 0:
            rows, cols = r // gi, c // gj
            src_map = lambda i, j: (i, j)
            dst_map = lambda i, j: (j, i, 0)
        elif r % (gj * BF16_ROWS) == 0 and c % gi == 0 and (c // gi) % tile == 0:
            rows, cols = r // gj, c // gi
            src_map = lambda i, j: (j, i)
            dst_map = lambda i, j: (i, j, 0)
        elif r % (gi * BF16_ROWS) == 0 and not j_outer:
            rows, cols = r // gi, c
            src_map = lambda i, j: (i, 0)
            dst_map = lambda i, j: (0, i, 0)
            once = gj > 1
        else:
            n = max(k for k in range(1, gi * gj + 1) if r % (k * BF16_ROWS) == 0)
            step = (lambda i, j: j * gi + i) if j_outer else (lambda i, j: i * gj + j)
            rows, cols = r // n, c
            src_map = lambda i, j, n=n, step=step: (jnp.minimum(step(i, j), n - 1), 0)
            dst_map = lambda i, j, n=n, step=step: (0, jnp.minimum(step(i, j), n - 1), 0)
        if j_outer:
            src_map = (lambda f: lambda j, i: f(i, j))(src_map)
            dst_map = (lambda f: lambda j, i: f(i, j))(dst_map)
        src_map = (lambda f, l: lambda a, b: (l, *f(a, b)))(src_map, layer)
        in_specs.append(pl.BlockSpec((None, rows, cols), src_map))
        out_specs.append(pl.BlockSpec((cols // tile, rows, tile), dst_map))
        out_shapes.append(jax.ShapeDtypeStruct((c // tile, r, tile), BF16))
        row_only.append(once)
    return in_specs, out_specs, out_shapes, row_only


def _fused_call(body, *, name, grid, j_outer, in_specs, inputs, out_spec, out_shape, scratch=(), casts=()):
    n_in, n_cast = len(inputs), len(casts)
    gi, gj = (grid[1], grid[0]) if j_outer else grid
    cast_in, cast_out, cast_shapes, row_only = _cast_plan(casts, gi, gj, j_outer)

    def kernel(*refs):
        ins, rest = refs[:n_in], refs[n_in:]
        cast_src, rest = rest[:n_cast], rest[n_cast:]
        out, rest = rest[0], rest[1:]
        cast_dst, scr = rest[:n_cast], rest[n_cast:]
        for src, dst, once in zip(cast_src, cast_dst, row_only):
            if once:
                pl.when(pl.program_id(1) == 0)(functools.partial(_copy_tiles, src, dst))
            else:
                _copy_tiles(src, dst)
        body(*ins, out, *scr)

    res = pl.pallas_call(
        kernel,
        grid=grid,
        in_specs=[*in_specs, *cast_in],
        out_specs=[out_spec, *cast_out],
        out_shape=[out_shape, *cast_shapes],
        scratch_shapes=list(scratch),
        compiler_params=_params("parallel", "parallel" if j_outer else "arbitrary"),
        name=name,
    )(*inputs, *[stack for stack, _, _ in casts])
    return res[0], list(res[1:])


def _conv_in_body(x_ref, g_ref, wa_ref, wg_ref, ba_ref, bg_ref, u_ref):
    h = _rms_norm(x_ref[...], g_ref[...]).astype(BF16)
    a = _dot(h, wa_ref[...]) + ba_ref[...]
    gate = _dot(h, wg_ref[...]) + bg_ref[...]
    u_ref[...] = a * jax.nn.sigmoid(gate)


def _conv_in(x, gain, w_in, b_in, *, tm, tn, casts):
    m, d = x.shape
    nj = d // tn
    return _fused_call(
        _conv_in_body,
        name="conv_in",
        grid=(nj, m // tm),
        j_outer=True,
        in_specs=[
            pl.BlockSpec((tm, d), lambda j, i: (i, 0)),
            pl.BlockSpec((1, d), lambda j, i: (0, 0)),
            pl.BlockSpec((None, d, tn), lambda j, i: (j, 0, 0)),
            pl.BlockSpec((None, d, tn), lambda j, i: (j + nj, 0, 0)),
            pl.BlockSpec((1, tn), lambda j, i: (0, j)),
            pl.BlockSpec((1, tn), lambda j, i: (0, j + nj)),
        ],
        inputs=(x, gain, w_in, w_in, b_in, b_in),
        out_spec=pl.BlockSpec((tm, tn), lambda j, i: (i, j)),
        out_shape=jax.ShapeDtypeStruct((m, d), F32),
        casts=casts,
    )


def _conv_out_kernel(u_ref, uprev_ref, wdw_ref, bdw_ref, lng_ref, lnb_ref, wout_ref, bout_ref, x_ref,
                     o_ref, ext_ref, conv_ref, *, tm):
    i = pl.program_id(0)
    d = u_ref.shape[1]
    first = CONV_HALO - (CONV_WIDTH - 1)

    for s in range(d // CONV_STRIP):
        cols = slice(s * CONV_STRIP, (s + 1) * CONV_STRIP)
        ext_ref[s, 0:CONV_HALO, :] = jnp.where(i == 0, 0.0, uprev_ref[:, cols])
        ext_ref[s, CONV_HALO:, :] = u_ref[:, cols]
        acc = jnp.broadcast_to(bdw_ref[:, cols], (tm, CONV_STRIP))
        for k in range(CONV_WIDTH):
            acc = acc + wdw_ref[k:k + 1, cols] * ext_ref[s, first + k:first + k + tm, :]
        conv_ref[:, cols] = acc

    conv = conv_ref[...]
    mu = jnp.mean(conv, axis=-1, keepdims=True)
    xc = conv - mu
    y = xc * lax.rsqrt(jnp.mean(xc * xc, axis=-1, keepdims=True) + EPS)
    y = y * lng_ref[...] + lnb_ref[...]
    v = (y * jax.nn.sigmoid(y)).astype(BF16)
    tn = wout_ref.shape[-1]
    for t in range(wout_ref.shape[0]):
        cols = slice(t * tn, (t + 1) * tn)
        o_ref[:, cols] = x_ref[:, cols] + _dot(v, wout_ref[t]) + bout_ref[:, cols]


def _conv_out(u, x, w_dw, b_dw, ln_g, ln_b, w_out, b_out, *, tm):
    m, d = u.shape
    hb = tm // CONV_HALO
    const = lambda i: (0, 0)
    return pl.pallas_call(
        functools.partial(_conv_out_kernel, tm=tm),
        grid=(m // tm,),
        in_specs=[
            pl.BlockSpec((tm, d), lambda i: (i, 0)),
            pl.BlockSpec((CONV_HALO, d), lambda i: (jnp.maximum(i * hb - 1, 0), 0)),
            pl.BlockSpec((CONV_WIDTH, d), const),
            pl.BlockSpec((1, d), const),
            pl.BlockSpec((1, d), const),
            pl.BlockSpec((1, d), const),
            pl.BlockSpec(w_out.shape, lambda i: (0, 0, 0), pipeline_mode=pl.Buffered(1)),
            pl.BlockSpec((1, d), const),
            pl.BlockSpec((tm, d), lambda i: (i, 0)),
        ],
        out_specs=pl.BlockSpec((tm, d), lambda i: (i, 0)),
        out_shape=jax.ShapeDtypeStruct((m, d), F32),
        scratch_shapes=[
            pltpu.VMEM((d // CONV_STRIP, tm + CONV_HALO, CONV_STRIP), F32),
            pltpu.VMEM((tm, d), F32),
        ],
        compiler_params=_params("parallel"),
        name="conv_out",
    )(u, u, w_dw, b_dw, ln_g, ln_b, w_out, b_out, x)


def _pool_kernel(x_ref, xprev_ref, g_ref, w_ref, scale_ref, o_ref, ext_ref, *, tm):
    i = pl.program_id(0)
    d = x_ref.shape[1]
    gc = d // len(POOL_WINDOWS)
    x = x_ref[...]
    h = _rms_norm(x, g_ref[...])
    hprev = jnp.where(i == 0, 0.0, _rms_norm(xprev_ref[...], g_ref[...]))
    t = i * tm + lax.broadcasted_iota(jnp.int32, (tm, 1), 0)
    strips = gc // V7X_LANES
    for g, w in enumerate(POOL_WINDOWS):
        cnt = jnp.minimum(t + 1, w).astype(F32)
        mixes = []
        for s in range(g * strips, (g + 1) * strips):
            lanes = slice(s * V7X_LANES, (s + 1) * V7X_LANES)
            ext_ref[s, 0:POOL_HALO, :] = hprev[:, lanes]
            ext_ref[s, POOL_HALO:, :] = h[:, lanes]
            total = h[:, lanes]
            for k in range(1, w):
                total = total + ext_ref[s, POOL_HALO - k:POOL_HALO - k + tm, :]
            mixes.append(total / cnt - h[:, lanes])
        cols = slice(g * gc, (g + 1) * gc)
        mix = jnp.concatenate(mixes, axis=1).astype(BF16)
        y = _dot(mix, w_ref[g]) * scale_ref[:, cols]
        o_ref[:, cols] = x[:, cols] + y


def _pool(x, gain, w_grp, scale, *, tm):
    m, d = x.shape
    ng, gc, _ = w_grp.shape
    hb = tm // POOL_HALO
    kernel = functools.partial(_pool_kernel, tm=tm)
    return pl.pallas_call(
        kernel,
        grid=(m // tm,),
        in_specs=[
            pl.BlockSpec((tm, d), lambda i: (i, 0)),
            pl.BlockSpec((POOL_HALO, d), lambda i: (jnp.maximum(i * hb - 1, 0), 0)),
            pl.BlockSpec((1, d), lambda i: (0, 0)),
            pl.BlockSpec((ng, gc, gc), lambda i: (0, 0, 0)),
            pl.BlockSpec((1, d), lambda i: (0, 0)),
        ],
        out_specs=pl.BlockSpec((tm, d), lambda i: (i, 0)),
        out_shape=jax.ShapeDtypeStruct((m, d), F32),
        scratch_shapes=[pltpu.VMEM((d // V7X_LANES, tm + POOL_HALO, V7X_LANES), F32)],
        compiler_params=_params("parallel"),
        name="pool",
    )(x, x, gain, w_grp, scale)


def _norm_matmul_body(x_ref, g_ref, w_ref, o_ref):
    o_ref[...] = _dot(_rms_norm(x_ref[...], g_ref[...]).astype(BF16), w_ref[...])


def _norm_matmul(x, gain, w, *, tm, tn, casts):
    m, d = x.shape
    n = w.shape[0] * tn
    return _fused_call(
        _norm_matmul_body,
        name="qkv_proj",
        grid=(n // tn, m // tm),
        j_outer=True,
        in_specs=[
            pl.BlockSpec((tm, d), lambda j, i: (i, 0)),
            pl.BlockSpec((1, d), lambda j, i: (0, 0)),
            pl.BlockSpec((None, d, tn), lambda j, i: (j, 0, 0)),
        ],
        inputs=(x, gain, w),
        out_spec=pl.BlockSpec((tm, tn), lambda j, i: (i, j)),
        out_shape=jax.ShapeDtypeStruct((m, n), F32),
        casts=casts,
    )


def _t5_bucket(rel):
    nb = NUM_BUCKETS // 2
    n = -rel
    ret = jnp.where(n < 0, nb, 0)
    n = jnp.abs(n)
    max_exact = nb // 2
    nf = jnp.maximum(n, 1).astype(F32)
    large = max_exact + (jnp.log(nf / max_exact) / math.log(REL_MAX_DIST / max_exact)
                         * (nb - max_exact)).astype(jnp.int32)
    large = jnp.minimum(large, nb - 1)
    return ret + jnp.where(n < max_exact, n, large)


def _band_bias_kernel(bucket_ref, rel_bias_ref, o_ref):
    h = pl.program_id(0)
    bucket = bucket_ref[...]
    acc = jnp.zeros(bucket.shape, F32)
    for b in range(NUM_BUCKETS):
        acc = jnp.where(bucket == b, rel_bias_ref[b, h], acc)
    o_ref[0] = acc


def _band_bias(rel_bias):
    n_heads = rel_bias.shape[1]
    i = jnp.arange(QBLOCK)[:, None]
    j = jnp.arange(2 * QBLOCK)[None, :]
    bucket = _t5_bucket(j - QBLOCK - i).astype(jnp.int32)
    return pl.pallas_call(
        _band_bias_kernel,
        grid=(n_heads,),
        in_specs=[
            pl.BlockSpec((QBLOCK, 2 * QBLOCK), lambda h: (0, 0)),
            pl.BlockSpec(memory_space=pltpu.SMEM),
        ],
        out_specs=pl.BlockSpec((1, QBLOCK, 2 * QBLOCK), lambda h: (h, 0, 0)),
        out_shape=jax.ShapeDtypeStruct((n_heads, QBLOCK, 2 * QBLOCK), F32),
        compiler_params=_params("parallel"),
        name="band_bias",
    )(bucket, rel_bias)


def _half_rms_norm(x, g, low):
    sq = x * x
    ms_lo = jnp.sum(jnp.where(low, sq, 0.0), axis=-1, keepdims=True) * (1.0 / HEAD_DIM)
    ms_hi = jnp.sum(jnp.where(low, 0.0, sq), axis=-1, keepdims=True) * (1.0 / HEAD_DIM)
    r = jnp.where(low, lax.rsqrt(ms_lo + EPS), lax.rsqrt(ms_hi + EPS))
    return x * r * g


def _attn_kernel(sinks_ref, q_ref, kc_ref, kp_ref, vc_ref, vp_ref, bias_ref, qg_ref, kg_ref, o_ref,
                 *, n_kv, group):
    n = pl.program_id(0)
    pair = 2 * HEAD_DIM
    n_heads = n_kv * group
    qi = lax.broadcasted_iota(jnp.int32, (QBLOCK, 2 * QBLOCK), 0)
    kj = lax.broadcasted_iota(jnp.int32, (QBLOCK, 2 * QBLOCK), 1)
    qc = qi // CHUNK
    kc = kj // CHUNK - QBLOCK // CHUNK
    visible = (kc <= qc) & (kc >= qc - WINDOW_CHUNKS) & ((n > 0) | (kj >= QBLOCK))
    low = lax.broadcasted_iota(jnp.int32, (1, pair), 1) < HEAD_DIM
    scale = HEAD_DIM ** -0.5

    q_all = jnp.concatenate([q_ref[:, c * pair:(c + 1) * pair] for c in range(n_heads // 2)], axis=0)
    q_all = _half_rms_norm(q_all, qg_ref[...], low) * scale
    k_all = jnp.concatenate([ref[:, c * pair:(c + 1) * pair]
                             for c in range(n_kv // 2) for ref in (kp_ref, kc_ref)], axis=0)
    k_all = _half_rms_norm(k_all, kg_ref[...], low).astype(BF16)

    logits, values = [], []
    heads_per_tile = 2 * group
    for hp in range(n_kv // 2):
        kcols = slice(hp * pair, (hp + 1) * pair)
        values.append(jnp.concatenate([vp_ref[:, kcols], vc_ref[:, kcols]], axis=0).astype(BF16))
        stacked = []
        for head in range(hp * heads_per_tile, (hp + 1) * heads_per_tile):
            qhalf, half = head % 2, (head // group) % 2
            q2 = q_all[(head // 2) * QBLOCK:(head // 2 + 1) * QBLOCK]
            qz = jnp.where(low == (qhalf == 0), q2, 0.0)
            if qhalf != half:
                qz = pltpu.roll(qz, HEAD_DIM, axis=1)
            stacked.append(qz.astype(BF16))
        k2 = k_all[hp * 2 * QBLOCK:(hp + 1) * 2 * QBLOCK]
        s = lax.dot_general(jnp.concatenate(stacked, axis=0), k2, (((1,), (1,)), ((), ())),
                            preferred_element_type=F32)
        for g in range(heads_per_tile):
            rows = s[g * QBLOCK:(g + 1) * QBLOCK]
            logits.append(jnp.where(visible, rows + bias_ref[hp * heads_per_tile + g], NEG_INF))

    s = jnp.concatenate(logits, axis=0)
    sink = jnp.concatenate([jnp.full((QBLOCK, 1), sinks_ref[h], F32) for h in range(n_heads)], axis=0)
    mx = jnp.maximum(jnp.max(s, axis=-1, keepdims=True), sink)
    e = jnp.exp(s - mx)
    denom = jnp.sum(e, axis=-1, keepdims=True) + jnp.exp(sink - mx)
    r = 1.0 / denom
    e = e.astype(BF16)

    for hp in range(n_kv // 2):
        rows = slice(hp * heads_per_tile * QBLOCK, (hp + 1) * heads_per_tile * QBLOCK)
        o = _dot(e[rows], values[hp]) * r[rows]
        for c in range(group):
            halves = []
            for qhalf in range(2):
                g = 2 * c + qhalf
                o2 = o[g * QBLOCK:(g + 1) * QBLOCK]
                if qhalf != (g // group) % 2:
                    o2 = pltpu.roll(o2, HEAD_DIM, axis=1)
                halves.append(o2)
            col = (hp * group + c) * pair
            o_ref[:, col:col + pair] = jnp.where(low, halves[0], halves[1]).astype(BF16)


def _attention(qkv, bias, sinks, q_gain, k_gain, *, n_heads, n_kv, casts):
    m = qkv.shape[0]
    group = n_heads // n_kv
    dq = n_heads * HEAD_DIM
    dkv = n_kv * HEAD_DIM
    kblk = dq // dkv
    prev = lambda n: jnp.maximum(n - 1, 0)
    return _fused_call(
        functools.partial(_attn_kernel, n_kv=n_kv, group=group),
        name="band_attention",
        grid=(m // QBLOCK, 1),
        j_outer=False,
        in_specs=[
            pl.BlockSpec(memory_space=pltpu.SMEM),
            pl.BlockSpec((QBLOCK, dq), lambda n, j: (n, 0)),
            pl.BlockSpec((QBLOCK, dkv), lambda n, j: (n, kblk)),
            pl.BlockSpec((QBLOCK, dkv), lambda n, j: (prev(n), kblk)),
            pl.BlockSpec((QBLOCK, dkv), lambda n, j: (n, kblk + 1)),
            pl.BlockSpec((QBLOCK, dkv), lambda n, j: (prev(n), kblk + 1)),
            pl.BlockSpec((n_heads, QBLOCK, 2 * QBLOCK), lambda n, j: (0, 0, 0)),
            pl.BlockSpec((1, 2 * HEAD_DIM), lambda n, j: (0, 0)),
            pl.BlockSpec((1, 2 * HEAD_DIM), lambda n, j: (0, 0)),
        ],
        inputs=(sinks, qkv, qkv, qkv, qkv, qkv, bias, q_gain, k_gain),
        out_spec=pl.BlockSpec((QBLOCK, dq), lambda n, j: (n, 0)),
        out_shape=jax.ShapeDtypeStruct((m, dq), BF16),
        casts=casts,
    )


def _matmul_residual_body(a_ref, w_ref, x_ref, o_ref):
    o_ref[...] = x_ref[...] + _dot(a_ref[...], w_ref[...])


def _matmul_residual(a, w, x, *, tm, tn, casts):
    m, k = a.shape
    n = w.shape[0] * tn
    return _fused_call(
        _matmul_residual_body,
        name="attn_out_proj",
        grid=(n // tn, m // tm),
        j_outer=True,
        in_specs=[
            pl.BlockSpec((tm, k), lambda j, i: (i, 0)),
            pl.BlockSpec((None, k, tn), lambda j, i: (j, 0, 0)),
            pl.BlockSpec((tm, tn), lambda j, i: (i, j)),
        ],
        inputs=(a, w, x),
        out_spec=pl.BlockSpec((tm, tn), lambda j, i: (i, j)),
        out_shape=jax.ShapeDtypeStruct((m, n), F32),
        casts=casts,
    )


def _ffn_body(x_ref, g_ref, wg_ref, wu_ref, wd_ref, p_ref, pg_ref, wpg_ref, bpg_ref, wpp_ref, o_ref, h_ref):
    j = pl.program_id(1)

    @pl.when(j == 0)
    def _():
        x = x_ref[...]
        h_ref[...] = _rms_norm(x, g_ref[...]).astype(BF16)
        o_ref[...] = x

    h = h_ref[...]
    gate = _dot(h, wg_ref[...])
    up = _dot(h, wu_ref[...])
    act = (gate * jax.nn.sigmoid(gate) * up).astype(BF16)
    o_ref[...] += _dot(act, wd_ref[...])

    @pl.when(j == pl.num_programs(1) - 1)
    def _():
        x1 = o_ref[...]
        h1 = _rms_norm(x1, pg_ref[...]).astype(BF16)
        pb = p_ref[...].astype(BF16)
        tn = wpg_ref.shape[-1]
        for t in range(wpg_ref.shape[0]):
            cols = slice(t * tn, (t + 1) * tn)
            pgate = jax.nn.sigmoid(_dot(h1, wpg_ref[t]) + bpg_ref[:, cols])
            o_ref[:, cols] = x1[:, cols] + pgate * _dot(pb, wpp_ref[t])


def _ffn(x, gain, w_gate, w_up, w_down, p, layer, ple_gain, w_pgate, b_pgate, w_pproj, *, tm, tf, casts):
    m, d = x.shape
    f = w_gate.shape[0] * tf
    pd = p.shape[-1]
    resident = lambda a: pl.BlockSpec(a.shape, lambda i, j: (0,) * a.ndim, pipeline_mode=pl.Buffered(1))
    return _fused_call(
        _ffn_body,
        name="swiglu_ffn_gated_embedding",
        grid=(m // tm, f // tf),
        j_outer=False,
        in_specs=[
            pl.BlockSpec((tm, d), lambda i, j: (i, 0)),
            pl.BlockSpec((1, d), lambda i, j: (0, 0)),
            pl.BlockSpec((None, d, tf), lambda i, j: (j, 0, 0)),
            pl.BlockSpec((None, d, tf), lambda i, j: (j, 0, 0)),
            pl.BlockSpec((None, tf, d), lambda i, j: (0, j, 0)),
            pl.BlockSpec((None, tm, pd), lambda i, j: (layer, i, 0)),
            pl.BlockSpec((1, d), lambda i, j: (0, 0)),
            resident(w_pgate),
            pl.BlockSpec((1, d), lambda i, j: (0, 0)),
            resident(w_pproj),
        ],
        inputs=(x, gain, w_gate, w_up, w_down, p, ple_gain, w_pgate, b_pgate, w_pproj),
        out_spec=pl.BlockSpec((tm, d), lambda i, j: (i, 0)),
        out_shape=jax.ShapeDtypeStruct((m, d), F32),
        scratch=[pltpu.VMEM((tm, d), BF16)],
        casts=casts,
    )


def kernel(x, p, norm_mix, norm_ffn, norm_ple, conv_w_in, conv_b_in, conv_w_dw, conv_b_dw, conv_ln_g,
           conv_ln_b, conv_w_out, conv_b_out, pool_w, pool_scale, attn_w_qkv, attn_q_norm, attn_k_norm,
           attn_sinks, attn_w_o, rel_bias, ffn_w_gate, ffn_w_up, ffn_w_down, ple_w_proj, ple_w_gate,
           ple_b_gate):
    batch, seq, d = x.shape
    depth = norm_mix.shape[0]
    n_heads = d // HEAD_DIM
    n_kv = (attn_w_qkv.shape[-1] // HEAD_DIM - n_heads) // 2
    group = n_heads // n_kv
    assert n_kv % 2 == 0 and group % 2 == 0, "attention kernel pairs heads into 128-lane tiles"
    assert d % (len(POOL_WINDOWS) * V7X_LANES) == 0
    m = batch * seq
    assert batch == 1 and m % QBLOCK == 0, "row tiles assume one sequence"

    tm = _tile(m, ROW_TILE)
    tm2 = _tile(m, WIDE_ROW_TILE)
    tn = _tile(d, COL_TILE)
    tf = _tile(ffn_w_gate.shape[-1], FF_TILE)
    tq = _tile(attn_w_qkv.shape[-1], COL_TILE)
    row = lambda v: v.reshape(1, -1)

    pool_w2 = pool_w.reshape(pool_w.shape[0], -1, pool_w.shape[-1])

    def mixer_weights(i):
        kind, l = i % N_MIXERS, i // N_MIXERS
        if kind == 0:
            return {("conv_in", i): (conv_w_in, l, tn), ("conv_out", i): (conv_w_out, l, tn)}
        if kind == 1:
            return {("pool", i): (pool_w2, l, pool_w2.shape[-1])}
        return {("qkv", i): (attn_w_qkv, l, tq), ("attn_o", i): (attn_w_o, l, tn)}

    def ffn_weights(i):
        return {("gate", i): (ffn_w_gate, i, tf), ("up", i): (ffn_w_up, i, tf), ("down", i): (ffn_w_down, i, d)}

    def ple_weights(i):
        return {("ple_gate", i): (ple_w_gate, i, tn), ("ple_proj", i): (ple_w_proj, i, tn)}

    ready = {}

    def bf16(key, source):
        return ready.pop(key) if key in ready else _cast(*source)

    def hosting(weights, call):
        out, copies = call(list(weights.values()))
        ready.update(zip(weights.keys(), copies))
        return out

    xs = x.reshape(m, d)
    bias = None
    for i in range(depth):
        kind, l = i % N_MIXERS, i // N_MIXERS
        gain = row(norm_mix[i])
        mine = {} if ("gate", i) in ready else {**ffn_weights(i), **ple_weights(i)}
        if kind == 0:
            w_in, w_out = (bf16(k, w) for k, w in mixer_weights(i).items())
            u = hosting(mine, lambda c: _conv_in(xs, gain, w_in, row(conv_b_in[l]), tm=tm, tn=tn, casts=c))
            xs = _conv_out(u, xs, conv_w_dw[l], row(conv_b_dw[l]), row(conv_ln_g[l]), row(conv_ln_b[l]),
                           w_out, row(conv_b_out[l]), tm=tm)
        elif kind == 1:
            (w_grp,) = (bf16(k, w) for k, w in mixer_weights(i).items())
            w_grp = w_grp.reshape(pool_w.shape[1:])
            xs = _pool(xs, gain, w_grp, row(pool_scale[l]), tm=tm)
        else:
            if bias is None:
                bias = _band_bias(rel_bias)
            w_qkv, w_o = (bf16(k, w) for k, w in mixer_weights(i).items())
            qkv, _ = _norm_matmul(xs, gain, w_qkv, tm=tm2, tn=tq, casts=[])
            o = hosting({**mine, **(mixer_weights(i + 1) if i + 1 < depth else {})}, lambda c: _attention(
                qkv, bias, attn_sinks[l], row(jnp.tile(attn_q_norm[l], 2)), row(jnp.tile(attn_k_norm[l], 2)),
                n_heads=n_heads, n_kv=n_kv, casts=c))
            xs, _ = _matmul_residual(o, w_o, xs, tm=tm2, tn=tn, casts=[])
        w_gate, w_up, w_down = (bf16(k, w) for k, w in ffn_weights(i).items())
        w_pg, w_pp = (bf16(k, w) for k, w in ple_weights(i).items())
        ahead = {}
        if i + 1 < depth:
            ahead = mixer_weights(i + 1)
            if (i + 1) % N_MIXERS == 1:
                ahead = {**ahead, **ffn_weights(i + 1), **ple_weights(i + 1)}
            ahead = {k: v for k, v in ahead.items() if k not in ready}
        xs = hosting(ahead, lambda c: _ffn(
            xs, row(norm_ffn[i]), w_gate, w_up, w_down, p.reshape(depth, m, -1), i, row(norm_ple[i]),
            w_pg, row(ple_b_gate[i]), w_pp, tm=tm, tf=tf, casts=c))
    return xs.reshape(batch, seq, d)
```

```python
import functools
import math

import jax
import jax.numpy as jnp
from jax import lax
from jax.experimental import pallas as pl
from jax.experimental.pallas import tpu as pltpu

N_MIXERS = 3
CHUNK = 64
CONV_WIDTH = 31
POOL_WINDOWS = (2, 4, 8, 16)
HEAD_DIM = 64
WINDOW_CHUNKS = 2
QBLOCK = 128
NUM_BUCKETS = 32
REL_MAX_DIST = 128
EPS = 1e-6
NEG_INF = -1e30

V7X_VMEM_BYTES = 64 * 1024 * 1024
V7X_LANES = 128
V7X_SUBLANES = 8
BF16_ROWS = 2 * V7X_SUBLANES
VMEM_LIMIT_BYTES = V7X_VMEM_BYTES * 7 // 8

CONV_HALO = 32
POOL_HALO = 16
CONV_STRIP = V7X_LANES

ROW_TILE = 512
WIDE_ROW_TILE = 1024
COL_TILE = 1024
FF_TILE = 512
CAST_ROWS = 256
BIAS_HEADS = 4

BF16 = jnp.bfloat16
F32 = jnp.float32


def _params(*semantics):
    return pltpu.CompilerParams(dimension_semantics=semantics, vmem_limit_bytes=VMEM_LIMIT_BYTES)


def _rms_norm(x, g):
    ms = jnp.mean(x * x, axis=-1, keepdims=True)
    return x * lax.rsqrt(ms + EPS) * g


def _dot(a, b):
    return jnp.dot(a, b, preferred_element_type=F32)


def _tile(n, target, multiple=V7X_LANES):
    if n <= target:
        return n
    t = target - target % multiple
    while n % t:
        t -= multiple
    return t


def _copy_tiles(src_ref, dst_ref):
    tile = dst_ref.shape[-1]
    for t in range(dst_ref.shape[0]):
        dst_ref[t] = src_ref[:, t * tile:(t + 1) * tile].astype(BF16)


def _cast(stack, layer, tile):
    _, r, c = stack.shape
    rb = _tile(r, CAST_ROWS, BF16_ROWS)
    return pl.pallas_call(
        _copy_tiles,
        grid=(r // rb,),
        in_specs=[pl.BlockSpec((None, rb, c), lambda i: (layer, i, 0))],
        out_specs=pl.BlockSpec((c // tile, rb, tile), lambda i: (0, i, 0)),
        out_shape=jax.ShapeDtypeStruct((c // tile, r, tile), BF16),
        compiler_params=_params("parallel"),
        name="weight_cast",
    )(stack)


def _cast_plan(sources, gi, gj, j_outer):
    in_specs, out_specs, out_shapes, row_only = [], [], [], []
    for stack, layer, tile in sources:
        _, r, c = stack.shape
        once = False
        if r % (gi * gj * BF16_ROWS) == 0:
            rows, cols = r // (gi * gj), c
            src_map = lambda i, j: (i * gj + j, 0)
            dst_map = lambda i, j: (0, i * gj + j, 0)
        elif r % (gi * BF16_ROWS) == 0 and c % gj == 0 and (c // gj) % tile == 0:
            rows, cols = r // gi, c // gj
            src_map = lambda i, j: (i, j)
            dst_map = lambda i, j: (j, i, 0)
        elif r % (gj * BF16_ROWS) == 0 and c % gi == 0 and (c // gi) % tile == 0:
            rows, cols = r // gj, c // gi
            src_map = lambda i, j: (j, i)
            dst_map = lambda i, j: (i, j, 0)
        elif r % (gi * BF16_ROWS) == 0 and not j_outer:
            rows, cols = r // gi, c
            src_map = lambda i, j: (i, 0)
            dst_map = lambda i, j: (0, i, 0)
            once = gj > 1
        else:
            n = max(k for k in range(1, gi * gj + 1) if r % (k * BF16_ROWS) == 0)
            step = (lambda i, j: j * gi + i) if j_outer else (lambda i, j: i * gj + j)
            rows, cols = r // n, c
            src_map = lambda i, j, n=n, step=step: (jnp.minimum(step(i, j), n - 1), 0)
            dst_map = lambda i, j, n=n, step=step: (0, jnp.minimum(step(i, j), n - 1), 0)
        if j_outer:
            src_map = (lambda f: lambda j, i: f(i, j))(src_map)
            dst_map = (lambda f: lambda j, i: f(i, j))(dst_map)
        src_map = (lambda f, l: lambda a, b: (l, *f(a, b)))(src_map, layer)
        in_specs.append(pl.BlockSpec((None, rows, cols), src_map))
        out_specs.append(pl.BlockSpec((cols // tile, rows, tile), dst_map))
        out_shapes.append(jax.ShapeDtypeStruct((c // tile, r, tile), BF16))
        row_only.append(once)
    return in_specs, out_specs, out_shapes, row_only


def _fused_call(body, *, name, grid, j_outer, in_specs, inputs, out_spec, out_shape, scratch=(), casts=()):
    n_in, n_cast = len(inputs), len(casts)
    gi, gj = (grid[1], grid[0]) if j_outer else grid
    cast_in, cast_out, cast_shapes, row_only = _cast_plan(casts, gi, gj, j_outer)

    def kernel(*refs):
        ins, rest = refs[:n_in], refs[n_in:]
        cast_src, rest = rest[:n_cast], rest[n_cast:]
        out, rest = rest[0], rest[1:]
        cast_dst, scr = rest[:n_cast], rest[n_cast:]
        for src, dst, once in zip(cast_src, cast_dst, row_only):
            if once:
                pl.when(pl.program_id(1) == 0)(functools.partial(_copy_tiles, src, dst))
            else:
                _copy_tiles(src, dst)
        body(*ins, out, *scr)

    res = pl.pallas_call(
        kernel,
        grid=grid,
        in_specs=[*in_specs, *cast_in],
        out_specs=[out_spec, *cast_out],
        out_shape=[out_shape, *cast_shapes],
        scratch_shapes=list(scratch),
        compiler_params=_params("parallel", "parallel" if j_outer else "arbitrary"),
        name=name,
    )(*inputs, *[stack for stack, _, _ in casts])
    return res[0], list(res[1:])


def _conv_in_body(x_ref, g_ref, wa_ref, wg_ref, ba_ref, bg_ref, u_ref):
    h = _rms_norm(x_ref[...], g_ref[...]).astype(BF16)
    a = _dot(h, wa_ref[...]) + ba_ref[...]
    gate = _dot(h, wg_ref[...]) + bg_ref[...]
    u_ref[...] = a * jax.nn.sigmoid(gate)


def _conv_in(x, gain, w_in, b_in, *, tm, tn, casts):
    m, d = x.shape
    nj = d // tn
    return _fused_call(
        _conv_in_body,
        name="conv_in",
        grid=(nj, m // tm),
        j_outer=True,
        in_specs=[
            pl.BlockSpec((tm, d), lambda j, i: (i, 0)),
            pl.BlockSpec((1, d), lambda j, i: (0, 0)),
            pl.BlockSpec((None, d, tn), lambda j, i: (j, 0, 0)),
            pl.BlockSpec((None, d, tn), lambda j, i: (j + nj, 0, 0)),
            pl.BlockSpec((1, tn), lambda j, i: (0, j)),
            pl.BlockSpec((1, tn), lambda j, i: (0, j + nj)),
        ],
        inputs=(x, gain, w_in, w_in, b_in, b_in),
        out_spec=pl.BlockSpec((tm, tn), lambda j, i: (i, j)),
        out_shape=jax.ShapeDtypeStruct((m, d), F32),
        casts=casts,
    )


def _conv_out_kernel(u_ref, uprev_ref, wdw_ref, bdw_ref, lng_ref, lnb_ref, wout_ref, bout_ref, x_ref,
                     o_ref, ext_ref, conv_ref, *, tm):
    i = pl.program_id(0)
    d = u_ref.shape[1]
    first = CONV_HALO - (CONV_WIDTH - 1)

    for s in range(d // CONV_STRIP):
        cols = slice(s * CONV_STRIP, (s + 1) * CONV_STRIP)
        ext_ref[s, 0:CONV_HALO, :] = jnp.where(i == 0, 0.0, uprev_ref[:, cols])
        ext_ref[s, CONV_HALO:, :] = u_ref[:, cols]
        acc = jnp.broadcast_to(bdw_ref[:, cols], (tm, CONV_STRIP))
        for k in range(CONV_WIDTH):
            acc = acc + wdw_ref[k:k + 1, cols] * ext_ref[s, first + k:first + k + tm, :]
        conv_ref[:, cols] = acc

    conv = conv_ref[...]
    mu = jnp.mean(conv, axis=-1, keepdims=True)
    xc = conv - mu
    y = xc * lax.rsqrt(jnp.mean(xc * xc, axis=-1, keepdims=True) + EPS)
    y = y * lng_ref[...] + lnb_ref[...]
    v = (y * jax.nn.sigmoid(y)).astype(BF16)
    tn = wout_ref.shape[-1]
    for t in range(wout_ref.shape[0]):
        cols = slice(t * tn, (t + 1) * tn)
        o_ref[:, cols] = x_ref[:, cols] + _dot(v, wout_ref[t]) + bout_ref[:, cols]


def _conv_out(u, x, w_dw, b_dw, ln_g, ln_b, w_out, b_out, *, tm):
    m, d = u.shape
    hb = tm // CONV_HALO
    const = lambda i: (0, 0)
    return pl.pallas_call(
        functools.partial(_conv_out_kernel, tm=tm),
        grid=(m // tm,),
        in_specs=[
            pl.BlockSpec((tm, d), lambda i: (i, 0)),
            pl.BlockSpec((CONV_HALO, d), lambda i: (jnp.maximum(i * hb - 1, 0), 0)),
            pl.BlockSpec((CONV_WIDTH, d), const),
            pl.BlockSpec((1, d), const),
            pl.BlockSpec((1, d), const),
            pl.BlockSpec((1, d), const),
            pl.BlockSpec(w_out.shape, lambda i: (0, 0, 0), pipeline_mode=pl.Buffered(1)),
            pl.BlockSpec((1, d), const),
            pl.BlockSpec((tm, d), lambda i: (i, 0)),
        ],
        out_specs=pl.BlockSpec((tm, d), lambda i: (i, 0)),
        out_shape=jax.ShapeDtypeStruct((m, d), F32),
        scratch_shapes=[
            pltpu.VMEM((d // CONV_STRIP, tm + CONV_HALO, CONV_STRIP), F32),
            pltpu.VMEM((tm, d), F32),
        ],
        compiler_params=_params("parallel"),
        name="conv_out",
    )(u, u, w_dw, b_dw, ln_g, ln_b, w_out, b_out, x)


def _pool_kernel(x_ref, xprev_ref, g_ref, w_ref, scale_ref, o_ref, ext_ref, *, tm):
    i = pl.program_id(0)
    d = x_ref.shape[1]
    gc = d // len(POOL_WINDOWS)
    x = x_ref[...]
    h = _rms_norm(x, g_ref[...])
    hprev = jnp.where(i == 0, 0.0, _rms_norm(xprev_ref[...], g_ref[...]))
    t = i * tm + lax.broadcasted_iota(jnp.int32, (tm, 1), 0)
    strips = gc // V7X_LANES
    for g, w in enumerate(POOL_WINDOWS):
        cnt = jnp.minimum(t + 1, w).astype(F32)
        mixes = []
        for s in range(g * strips, (g + 1) * strips):
            lanes = slice(s * V7X_LANES, (s + 1) * V7X_LANES)
            ext_ref[s, 0:POOL_HALO, :] = hprev[:, lanes]
            ext_ref[s, POOL_HALO:, :] = h[:, lanes]
            total = h[:, lanes]
            for k in range(1, w):
                total = total + ext_ref[s, POOL_HALO - k:POOL_HALO - k + tm, :]
            mixes.append(total / cnt - h[:, lanes])
        cols = slice(g * gc, (g + 1) * gc)
        mix = jnp.concatenate(mixes, axis=1).astype(BF16)
        y = _dot(mix, w_ref[g]) * scale_ref[:, cols]
        o_ref[:, cols] = x[:, cols] + y


def _pool(x, gain, w_grp, scale, *, tm):
    m, d = x.shape
    ng, gc, _ = w_grp.shape
    hb = tm // POOL_HALO
    kernel = functools.partial(_pool_kernel, tm=tm)
    return pl.pallas_call(
        kernel,
        grid=(m // tm,),
        in_specs=[
            pl.BlockSpec((tm, d), lambda i: (i, 0)),
            pl.BlockSpec((POOL_HALO, d), lambda i: (jnp.maximum(i * hb - 1, 0), 0)),
            pl.BlockSpec((1, d), lambda i: (0, 0)),
            pl.BlockSpec((ng, gc, gc), lambda i: (0, 0, 0)),
            pl.BlockSpec((1, d), lambda i: (0, 0)),
        ],
        out_specs=pl.BlockSpec((tm, d), lambda i: (i, 0)),
        out_shape=jax.ShapeDtypeStruct((m, d), F32),
        scratch_shapes=[pltpu.VMEM((d // V7X_LANES, tm + POOL_HALO, V7X_LANES), F32)],
        compiler_params=_params("parallel"),
        name="pool",
    )(x, x, gain, w_grp, scale)


def _norm_matmul_body(x_ref, g_ref, w_ref, o_ref):
    o_ref[...] = _dot(_rms_norm(x_ref[...], g_ref[...]).astype(BF16), w_ref[...])


def _norm_matmul(x, gain, w, *, tm, tn, casts):
    m, d = x.shape
    n = w.shape[0] * tn
    return _fused_call(
        _norm_matmul_body,
        name="qkv_proj",
        grid=(n // tn, m // tm),
        j_outer=True,
        in_specs=[
            pl.BlockSpec((tm, d), lambda j, i: (i, 0)),
            pl.BlockSpec((1, d), lambda j, i: (0, 0)),
            pl.BlockSpec((None, d, tn), lambda j, i: (j, 0, 0)),
        ],
        inputs=(x, gain, w),
        out_spec=pl.BlockSpec((tm, tn), lambda j, i: (i, j)),
        out_shape=jax.ShapeDtypeStruct((m, n), F32),
        casts=casts,
    )


def _t5_bucket(rel):
    nb = NUM_BUCKETS // 2
    n = -rel
    ret = jnp.where(n < 0, nb, 0)
    n = jnp.abs(n)
    max_exact = nb // 2
    nf = jnp.maximum(n, 1).astype(F32)
    large = max_exact + (jnp.log(nf / max_exact) / math.log(REL_MAX_DIST / max_exact)
                         * (nb - max_exact)).astype(jnp.int32)
    large = jnp.minimum(large, nb - 1)
    return ret + jnp.where(n < max_exact, n, large)


def _band_bias_body(bucket_ref, rel_bias_ref, o_ref):
    heads = o_ref.shape[0]
    bucket = bucket_ref[...]
    for k in range(heads):
        h = pl.program_id(0) * heads + k
        acc = jnp.zeros(bucket.shape, F32)
        for b in range(NUM_BUCKETS):
            acc = jnp.where(bucket == b, rel_bias_ref[b, h], acc)
        o_ref[k] = acc


def _band_bias(rel_bias, *, casts):
    n_heads = rel_bias.shape[1]
    heads = _tile(n_heads, BIAS_HEADS, 1)
    i = jnp.arange(QBLOCK)[:, None]
    j = jnp.arange(2 * QBLOCK)[None, :]
    bucket = _t5_bucket(j - QBLOCK - i).astype(jnp.int32)
    return _fused_call(
        _band_bias_body,
        name="band_bias",
        grid=(n_heads // heads, 1),
        j_outer=False,
        in_specs=[
            pl.BlockSpec((QBLOCK, 2 * QBLOCK), lambda h, j: (0, 0)),
            pl.BlockSpec(memory_space=pltpu.SMEM),
        ],
        inputs=(bucket, rel_bias),
        out_spec=pl.BlockSpec((heads, QBLOCK, 2 * QBLOCK), lambda h, j: (h, 0, 0)),
        out_shape=jax.ShapeDtypeStruct((n_heads, QBLOCK, 2 * QBLOCK), F32),
        casts=casts,
    )


def _half_rms_norm(x, g, low):
    sq = x * x
    ms_lo = jnp.sum(jnp.where(low, sq, 0.0), axis=-1, keepdims=True) * (1.0 / HEAD_DIM)
    ms_hi = jnp.sum(jnp.where(low, 0.0, sq), axis=-1, keepdims=True) * (1.0 / HEAD_DIM)
    r = jnp.where(low, lax.rsqrt(ms_lo + EPS), lax.rsqrt(ms_hi + EPS))
    return x * r * g


def _attn_kernel(sinks_ref, q_ref, kc_ref, kp_ref, vc_ref, vp_ref, bias_ref, qg_ref, kg_ref, o_ref,
                 *, n_kv, group):
    n = pl.program_id(0)
    pair = 2 * HEAD_DIM
    n_heads = n_kv * group
    qi = lax.broadcasted_iota(jnp.int32, (QBLOCK, 2 * QBLOCK), 0)
    kj = lax.broadcasted_iota(jnp.int32, (QBLOCK, 2 * QBLOCK), 1)
    qc = qi // CHUNK
    kc = kj // CHUNK - QBLOCK // CHUNK
    visible = (kc <= qc) & (kc >= qc - WINDOW_CHUNKS) & ((n > 0) | (kj >= QBLOCK))
    low = lax.broadcasted_iota(jnp.int32, (1, pair), 1) < HEAD_DIM
    scale = HEAD_DIM ** -0.5

    q_all = jnp.concatenate([q_ref[:, c * pair:(c + 1) * pair] for c in range(n_heads // 2)], axis=0)
    q_all = _half_rms_norm(q_all, qg_ref[...], low) * scale
    k_all = jnp.concatenate([ref[:, c * pair:(c + 1) * pair]
                             for c in range(n_kv // 2) for ref in (kp_ref, kc_ref)], axis=0)
    k_all = _half_rms_norm(k_all, kg_ref[...], low).astype(BF16)

    logits, values = [], []
    heads_per_tile = 2 * group
    for hp in range(n_kv // 2):
        kcols = slice(hp * pair, (hp + 1) * pair)
        values.append(jnp.concatenate([vp_ref[:, kcols], vc_ref[:, kcols]], axis=0).astype(BF16))
        stacked = []
        for head in range(hp * heads_per_tile, (hp + 1) * heads_per_tile):
            qhalf, half = head % 2, (head // group) % 2
            q2 = q_all[(head // 2) * QBLOCK:(head // 2 + 1) * QBLOCK]
            qz = jnp.where(low == (qhalf == 0), q2, 0.0)
            if qhalf != half:
                qz = pltpu.roll(qz, HEAD_DIM, axis=1)
            stacked.append(qz.astype(BF16))
        k2 = k_all[hp * 2 * QBLOCK:(hp + 1) * 2 * QBLOCK]
        s = lax.dot_general(jnp.concatenate(stacked, axis=0), k2, (((1,), (1,)), ((), ())),
                            preferred_element_type=F32)
        for g in range(heads_per_tile):
            rows = s[g * QBLOCK:(g + 1) * QBLOCK]
            logits.append(jnp.where(visible, rows + bias_ref[hp * heads_per_tile + g], NEG_INF))

    s = jnp.concatenate(logits, axis=0)
    sink = jnp.concatenate([jnp.full((QBLOCK, 1), sinks_ref[h], F32) for h in range(n_heads)], axis=0)
    mx = jnp.maximum(jnp.max(s, axis=-1, keepdims=True), sink)
    e = jnp.exp(s - mx)
    denom = jnp.sum(e, axis=-1, keepdims=True) + jnp.exp(sink - mx)
    r = 1.0 / denom
    e = e.astype(BF16)

    for hp in range(n_kv // 2):
        rows = slice(hp * heads_per_tile * QBLOCK, (hp + 1) * heads_per_tile * QBLOCK)
        o = _dot(e[rows], values[hp]) * r[rows]
        for c in range(group):
            halves = []
            for qhalf in range(2):
                g = 2 * c + qhalf
                o2 = o[g * QBLOCK:(g + 1) * QBLOCK]
                if qhalf != (g // group) % 2:
                    o2 = pltpu.roll(o2, HEAD_DIM, axis=1)
                halves.append(o2)
            col = (hp * group + c) * pair
            o_ref[:, col:col + pair] = jnp.where(low, halves[0], halves[1]).astype(BF16)


def _attention(qkv, bias, sinks, q_gain, k_gain, *, n_heads, n_kv, casts):
    m = qkv.shape[0]
    group = n_heads // n_kv
    dq = n_heads * HEAD_DIM
    dkv = n_kv * HEAD_DIM
    kblk = dq // dkv
    prev = lambda n: jnp.maximum(n - 1, 0)
    return _fused_call(
        functools.partial(_attn_kernel, n_kv=n_kv, group=group),
        name="band_attention",
        grid=(m // QBLOCK, 1),
        j_outer=False,
        in_specs=[
            pl.BlockSpec(memory_space=pltpu.SMEM),
            pl.BlockSpec((QBLOCK, dq), lambda n, j: (n, 0)),
            pl.BlockSpec((QBLOCK, dkv), lambda n, j: (n, kblk)),
            pl.BlockSpec((QBLOCK, dkv), lambda n, j: (prev(n), kblk)),
            pl.BlockSpec((QBLOCK, dkv), lambda n, j: (n, kblk + 1)),
            pl.BlockSpec((QBLOCK, dkv), lambda n, j: (prev(n), kblk + 1)),
            pl.BlockSpec((n_heads, QBLOCK, 2 * QBLOCK), lambda n, j: (0, 0, 0)),
            pl.BlockSpec((1, 2 * HEAD_DIM), lambda n, j: (0, 0)),
            pl.BlockSpec((1, 2 * HEAD_DIM), lambda n, j: (0, 0)),
        ],
        inputs=(sinks, qkv, qkv, qkv, qkv, qkv, bias, q_gain, k_gain),
        out_spec=pl.BlockSpec((QBLOCK, dq), lambda n, j: (n, 0)),
        out_shape=jax.ShapeDtypeStruct((m, dq), BF16),
        casts=casts,
    )


def _matmul_residual_body(a_ref, w_ref, x_ref, o_ref):
    o_ref[...] = x_ref[...] + _dot(a_ref[...], w_ref[...])


def _matmul_residual(a, w, x, *, tm, tn, casts):
    m, k = a.shape
    n = w.shape[0] * tn
    return _fused_call(
        _matmul_residual_body,
        name="attn_out_proj",
        grid=(n // tn, m // tm),
        j_outer=True,
        in_specs=[
            pl.BlockSpec((tm, k), lambda j, i: (i, 0)),
            pl.BlockSpec((None, k, tn), lambda j, i: (j, 0, 0)),
            pl.BlockSpec((tm, tn), lambda j, i: (i, j)),
        ],
        inputs=(a, w, x),
        out_spec=pl.BlockSpec((tm, tn), lambda j, i: (i, j)),
        out_shape=jax.ShapeDtypeStruct((m, n), F32),
        casts=casts,
    )


def _ffn_body(x_ref, g_ref, wg_ref, wu_ref, wd_ref, p_ref, pg_ref, wpg_ref, bpg_ref, wpp_ref, o_ref, h_ref):
    j = pl.program_id(1)

    @pl.when(j == 0)
    def _():
        x = x_ref[...]
        h_ref[...] = _rms_norm(x, g_ref[...]).astype(BF16)
        o_ref[...] = x

    h = h_ref[...]
    gate = _dot(h, wg_ref[...])
    up = _dot(h, wu_ref[...])
    act = (gate * jax.nn.sigmoid(gate) * up).astype(BF16)
    o_ref[...] += _dot(act, wd_ref[...])

    @pl.when(j == pl.num_programs(1) - 1)
    def _():
        x1 = o_ref[...]
        h1 = _rms_norm(x1, pg_ref[...]).astype(BF16)
        pb = p_ref[...].astype(BF16)
        tn = wpg_ref.shape[-1]
        for t in range(wpg_ref.shape[0]):
            cols = slice(t * tn, (t + 1) * tn)
            pgate = jax.nn.sigmoid(_dot(h1, wpg_ref[t]) + bpg_ref[:, cols])
            o_ref[:, cols] = x1[:, cols] + pgate * _dot(pb, wpp_ref[t])


def _ffn(x, gain, w_gate, w_up, w_down, p, layer, ple_gain, w_pgate, b_pgate, w_pproj, *, tm, tf, casts):
    m, d = x.shape
    f = w_gate.shape[0] * tf
    pd = p.shape[-1]
    resident = lambda a: pl.BlockSpec(a.shape, lambda i, j: (0,) * a.ndim, pipeline_mode=pl.Buffered(1))
    return _fused_call(
        _ffn_body,
        name="swiglu_ffn_gated_embedding",
        grid=(m // tm, f // tf),
        j_outer=False,
        in_specs=[
            pl.BlockSpec((tm, d), lambda i, j: (i, 0)),
            pl.BlockSpec((1, d), lambda i, j: (0, 0)),
            pl.BlockSpec((None, d, tf), lambda i, j: (j, 0, 0)),
            pl.BlockSpec((None, d, tf), lambda i, j: (j, 0, 0)),
            pl.BlockSpec((None, tf, d), lambda i, j: (0, j, 0)),
            pl.BlockSpec((None, tm, pd), lambda i, j: (layer, i, 0)),
            pl.BlockSpec((1, d), lambda i, j: (0, 0)),
            resident(w_pgate),
            pl.BlockSpec((1, d), lambda i, j: (0, 0)),
            resident(w_pproj),
        ],
        inputs=(x, gain, w_gate, w_up, w_down, p, ple_gain, w_pgate, b_pgate, w_pproj),
        out_spec=pl.BlockSpec((tm, d), lambda i, j: (i, 0)),
        out_shape=jax.ShapeDtypeStruct((m, d), F32),
        scratch=[pltpu.VMEM((tm, d), BF16)],
        casts=casts,
    )


def kernel(x, p, norm_mix, norm_ffn, norm_ple, conv_w_in, conv_b_in, conv_w_dw, conv_b_dw, conv_ln_g,
           conv_ln_b, conv_w_out, conv_b_out, pool_w, pool_scale, attn_w_qkv, attn_q_norm, attn_k_norm,
           attn_sinks, attn_w_o, rel_bias, ffn_w_gate, ffn_w_up, ffn_w_down, ple_w_proj, ple_w_gate,
           ple_b_gate):
    batch, seq, d = x.shape
    depth = norm_mix.shape[0]
    n_heads = d // HEAD_DIM
    n_kv = (attn_w_qkv.shape[-1] // HEAD_DIM - n_heads) // 2
    group = n_heads // n_kv
    assert n_kv % 2 == 0 and group % 2 == 0, "attention kernel pairs heads into 128-lane tiles"
    assert d % (len(POOL_WINDOWS) * V7X_LANES) == 0
    m = batch * seq
    assert batch == 1 and m % QBLOCK == 0, "row tiles assume one sequence"

    tm = _tile(m, ROW_TILE)
    tm2 = _tile(m, WIDE_ROW_TILE)
    tn = _tile(d, COL_TILE)
    tf = _tile(ffn_w_gate.shape[-1], FF_TILE)
    tq = _tile(attn_w_qkv.shape[-1], COL_TILE)
    row = lambda v: v.reshape(1, -1)

    pool_w2 = pool_w.reshape(pool_w.shape[0], -1, pool_w.shape[-1])

    def mixer_weights(i):
        kind, l = i % N_MIXERS, i // N_MIXERS
        if kind == 0:
            return {("conv_in", i): (conv_w_in, l, tn), ("conv_out", i): (conv_w_out, l, tn)}
        if kind == 1:
            return {("pool", i): (pool_w2, l, pool_w2.shape[-1])}
        return {("qkv", i): (attn_w_qkv, l, tq), ("attn_o", i): (attn_w_o, l, tn)}

    def ffn_weights(i):
        return {("gate", i): (ffn_w_gate, i, tf), ("up", i): (ffn_w_up, i, tf), ("down", i): (ffn_w_down, i, d)}

    def ple_weights(i):
        return {("ple_gate", i): (ple_w_gate, i, tn), ("ple_proj", i): (ple_w_proj, i, tn)}

    ready = {}

    def bf16(key, source):
        return ready.pop(key) if key in ready else _cast(*source)

    def hosting(weights, call):
        out, copies = call(list(weights.values()))
        ready.update(zip(weights.keys(), copies))
        return out

    xs = x.reshape(m, d)
    bias = None
    if depth >= N_MIXERS:
        bias = hosting(mixer_weights(0), lambda c: _band_bias(rel_bias, casts=c))
    for i in range(depth):
        kind, l = i % N_MIXERS, i // N_MIXERS
        gain = row(norm_mix[i])
        mine = {} if ("gate", i) in ready else {**ffn_weights(i), **ple_weights(i)}
        if kind == 0:
            w_in, w_out = (bf16(k, w) for k, w in mixer_weights(i).items())
            u = hosting(mine, lambda c: _conv_in(xs, gain, w_in, row(conv_b_in[l]), tm=tm, tn=tn, casts=c))
            xs = _conv_out(u, xs, conv_w_dw[l], row(conv_b_dw[l]), row(conv_ln_g[l]), row(conv_ln_b[l]),
                           w_out, row(conv_b_out[l]), tm=tm)
        elif kind == 1:
            (w_grp,) = (bf16(k, w) for k, w in mixer_weights(i).items())
            w_grp = w_grp.reshape(pool_w.shape[1:])
            xs = _pool(xs, gain, w_grp, row(pool_scale[l]), tm=tm)
        else:
            w_qkv, w_o = (bf16(k, w) for k, w in mixer_weights(i).items())
            qkv, _ = _norm_matmul(xs, gain, w_qkv, tm=tm2, tn=tq, casts=[])
            o = hosting({**mine, **(mixer_weights(i + 1) if i + 1 < depth else {})}, lambda c: _attention(
                qkv, bias, attn_sinks[l], row(jnp.tile(attn_q_norm[l], 2)), row(jnp.tile(attn_k_norm[l], 2)),
                n_heads=n_heads, n_kv=n_kv, casts=c))
            xs, _ = _matmul_residual(o, w_o, xs, tm=tm2, tn=tn, casts=[])
        w_gate, w_up, w_down = (bf16(k, w) for k, w in ffn_weights(i).items())
        w_pg, w_pp = (bf16(k, w) for k, w in ple_weights(i).items())
        ahead = {}
        if i + 1 < depth:
            ahead = mixer_weights(i + 1)
            if (i + 1) % N_MIXERS == 1:
                ahead = {**ahead, **ffn_weights(i + 1), **ple_weights(i + 1)}
            ahead = {k: v for k, v in ahead.items() if k not in ready}
        xs = hosting(ahead, lambda c: _ffn(
            xs, row(norm_ffn[i]), w_gate, w_up, w_down, p.reshape(depth, m, -1), i, row(norm_ple[i]),
            w_pg, row(ple_b_gate[i]), w_pp, tm=tm, tf=tf, casts=c))
    return xs.reshape(batch, seq, d)
```

```python
import functools
import math

import jax
import jax.numpy as jnp
from jax import lax
from jax.experimental import pallas as pl
from jax.experimental.pallas import tpu as pltpu

N_MIXERS = 3
CHUNK = 64
CONV_WIDTH = 31
POOL_WINDOWS = (2, 4, 8, 16)
HEAD_DIM = 64
WINDOW_CHUNKS = 2
QBLOCK = 128
NUM_BUCKETS = 32
REL_MAX_DIST = 128
EPS = 1e-6
NEG_INF = -1e30

V7X_VMEM_BYTES = 64 * 1024 * 1024
V7X_LANES = 128
V7X_SUBLANES = 8
BF16_ROWS = 2 * V7X_SUBLANES
VMEM_LIMIT_BYTES = V7X_VMEM_BYTES * 7 // 8

CONV_HALO = 32
POOL_HALO = 16
CONV_STRIP = V7X_LANES

ROW_TILE = 512
WIDE_ROW_TILE = 1024
COL_TILE = 1024
FF_TILE = 512
CAST_ROWS = 256
BIAS_HEADS = 4

BF16 = jnp.bfloat16
F32 = jnp.float32


def _params(*semantics):
    return pltpu.CompilerParams(dimension_semantics=semantics, vmem_limit_bytes=VMEM_LIMIT_BYTES)


def _rms_norm(x, g):
    ms = jnp.mean(x * x, axis=-1, keepdims=True)
    return x * lax.rsqrt(ms + EPS) * g


def _dot(a, b):
    return jnp.dot(a, b, preferred_element_type=F32)


def _tile(n, target, multiple=V7X_LANES):
    if n <= target:
        return n
    t = target - target % multiple
    while n % t:
        t -= multiple
    return t


def _copy_tiles(*refs):
    *srcs, dst_ref = refs
    tile = dst_ref.shape[-1] // len(srcs)
    for t in range(dst_ref.shape[0]):
        for k, src_ref in enumerate(srcs):
            dst_ref[t, :, k * tile:(k + 1) * tile] = src_ref[:, t * tile:(t + 1) * tile].astype(BF16)


def _cast(stacks, layer, tile):
    _, r, c = stacks[0].shape
    rb = _tile(r, CAST_ROWS, BF16_ROWS)
    n = len(stacks)
    return pl.pallas_call(
        _copy_tiles,
        grid=(r // rb,),
        in_specs=[pl.BlockSpec((None, rb, c), lambda i: (layer, i, 0))] * n,
        out_specs=pl.BlockSpec((c // tile, rb, n * tile), lambda i: (0, i, 0)),
        out_shape=jax.ShapeDtypeStruct((c // tile, r, n * tile), BF16),
        compiler_params=_params("parallel"),
        name="weight_cast",
    )(*stacks)


def _cast_plan(sources, gi, gj, j_outer):
    in_specs, out_specs, out_shapes, row_only = [], [], [], []
    for stacks, layer, tile in sources:
        _, r, c = stacks[0].shape
        once = False
        if r % (gi * gj * BF16_ROWS) == 0:
            rows, cols = r // (gi * gj), c
            src_map = lambda i, j: (i * gj + j, 0)
            dst_map = lambda i, j: (0, i * gj + j, 0)
        elif r % (gi * BF16_ROWS) == 0 and c % gj == 0 and (c // gj) % tile == 0:
            rows, cols = r // gi, c // gj
            src_map = lambda i, j: (i, j)
            dst_map = lambda i, j: (j, i, 0)
        elif r % (gj * BF16_ROWS) == 0 and c % gi == 0 and (c // gi) % tile == 0:
            rows, cols = r // gj, c // gi
            src_map = lambda i, j: (j, i)
            dst_map = lambda i, j: (i, j, 0)
        elif r % (gi * BF16_ROWS) == 0 and not j_outer:
            rows, cols = r // gi, c
            src_map = lambda i, j: (i, 0)
            dst_map = lambda i, j: (0, i, 0)
            once = gj > 1
        else:
            n = max(k for k in range(1, gi * gj + 1) if r % (k * BF16_ROWS) == 0)
            step = (lambda i, j: j * gi + i) if j_outer else (lambda i, j: i * gj + j)
            rows, cols = r // n, c
            src_map = lambda i, j, n=n, step=step: (jnp.minimum(step(i, j), n - 1), 0)
            dst_map = lambda i, j, n=n, step=step: (0, jnp.minimum(step(i, j), n - 1), 0)
        if j_outer:
            src_map = (lambda f: lambda j, i: f(i, j))(src_map)
            dst_map = (lambda f: lambda j, i: f(i, j))(dst_map)
        src_map = (lambda f, l: lambda a, b: (l, *f(a, b)))(src_map, layer)
        in_specs.append([pl.BlockSpec((None, rows, cols), src_map)] * len(stacks))
        out_specs.append(pl.BlockSpec((cols // tile, rows, len(stacks) * tile), dst_map))
        out_shapes.append(jax.ShapeDtypeStruct((c // tile, r, len(stacks) * tile), BF16))
        row_only.append(once)
    return in_specs, out_specs, out_shapes, row_only


def _fused_call(body, *, name, grid, j_outer, in_specs, inputs, out_spec, out_shape, scratch=(), casts=()):
    n_in, n_cast = len(inputs), len(casts)
    gi, gj = (grid[1], grid[0]) if j_outer else grid
    cast_in, cast_out, cast_shapes, row_only = _cast_plan(casts, gi, gj, j_outer)
    cast_stacks = [stack for stacks, _, _ in casts for stack in stacks]

    def kernel(*refs):
        ins, rest = refs[:n_in], refs[n_in:]
        cast_src, rest = list(rest[:len(cast_stacks)]), rest[len(cast_stacks):]
        out, rest = rest[0], rest[1:]
        cast_dst, scr = rest[:n_cast], rest[n_cast:]
        for (stacks, _, _), dst, once in zip(casts, cast_dst, row_only):
            srcs = [cast_src.pop(0) for _ in stacks]
            if once:
                pl.when(pl.program_id(1) == 0)(functools.partial(_copy_tiles, *srcs, dst))
            else:
                _copy_tiles(*srcs, dst)
        body(*ins, out, *scr)

    res = pl.pallas_call(
        kernel,
        grid=grid,
        in_specs=[*in_specs, *[spec for specs in cast_in for spec in specs]],
        out_specs=[out_spec, *cast_out],
        out_shape=[out_shape, *cast_shapes],
        scratch_shapes=list(scratch),
        compiler_params=_params("parallel", "parallel" if j_outer else "arbitrary"),
        name=name,
    )(*inputs, *cast_stacks)
    return res[0], list(res[1:])


def _conv_in_body(x_ref, g_ref, wa_ref, wg_ref, ba_ref, bg_ref, u_ref):
    h = _rms_norm(x_ref[...], g_ref[...]).astype(BF16)
    a = _dot(h, wa_ref[...]) + ba_ref[...]
    gate = _dot(h, wg_ref[...]) + bg_ref[...]
    u_ref[...] = a * jax.nn.sigmoid(gate)


def _conv_in(x, gain, w_in, b_in, *, tm, tn, casts):
    m, d = x.shape
    nj = d // tn
    return _fused_call(
        _conv_in_body,
        name="conv_in",
        grid=(nj, m // tm),
        j_outer=True,
        in_specs=[
            pl.BlockSpec((tm, d), lambda j, i: (i, 0)),
            pl.BlockSpec((1, d), lambda j, i: (0, 0)),
            pl.BlockSpec((None, d, tn), lambda j, i: (j, 0, 0)),
            pl.BlockSpec((None, d, tn), lambda j, i: (j + nj, 0, 0)),
            pl.BlockSpec((1, tn), lambda j, i: (0, j)),
            pl.BlockSpec((1, tn), lambda j, i: (0, j + nj)),
        ],
        inputs=(x, gain, w_in, w_in, b_in, b_in),
        out_spec=pl.BlockSpec((tm, tn), lambda j, i: (i, j)),
        out_shape=jax.ShapeDtypeStruct((m, d), F32),
        casts=casts,
    )


def _conv_out_kernel(u_ref, uprev_ref, wdw_ref, bdw_ref, lng_ref, lnb_ref, wout_ref, bout_ref, x_ref,
                     o_ref, ext_ref, conv_ref, *, tm):
    i = pl.program_id(0)
    d = u_ref.shape[1]
    first = CONV_HALO - (CONV_WIDTH - 1)

    for s in range(d // CONV_STRIP):
        cols = slice(s * CONV_STRIP, (s + 1) * CONV_STRIP)
        ext_ref[s, 0:CONV_HALO, :] = jnp.where(i == 0, 0.0, uprev_ref[:, cols])
        ext_ref[s, CONV_HALO:, :] = u_ref[:, cols]
        acc = jnp.broadcast_to(bdw_ref[:, cols], (tm, CONV_STRIP))
        for k in range(CONV_WIDTH):
            acc = acc + wdw_ref[k:k + 1, cols] * ext_ref[s, first + k:first + k + tm, :]
        conv_ref[:, cols] = acc

    conv = conv_ref[...]
    mu = jnp.mean(conv, axis=-1, keepdims=True)
    xc = conv - mu
    y = xc * lax.rsqrt(jnp.mean(xc * xc, axis=-1, keepdims=True) + EPS)
    y = y * lng_ref[...] + lnb_ref[...]
    v = (y * jax.nn.sigmoid(y)).astype(BF16)
    tn = wout_ref.shape[-1]
    for t in range(wout_ref.shape[0]):
        cols = slice(t * tn, (t + 1) * tn)
        o_ref[:, cols] = x_ref[:, cols] + _dot(v, wout_ref[t]) + bout_ref[:, cols]


def _conv_out(u, x, w_dw, b_dw, ln_g, ln_b, w_out, b_out, *, tm):
    m, d = u.shape
    hb = tm // CONV_HALO
    const = lambda i: (0, 0)
    return pl.pallas_call(
        functools.partial(_conv_out_kernel, tm=tm),
        grid=(m // tm,),
        in_specs=[
            pl.BlockSpec((tm, d), lambda i: (i, 0)),
            pl.BlockSpec((CONV_HALO, d), lambda i: (jnp.maximum(i * hb - 1, 0), 0)),
            pl.BlockSpec((CONV_WIDTH, d), const),
            pl.BlockSpec((1, d), const),
            pl.BlockSpec((1, d), const),
            pl.BlockSpec((1, d), const),
            pl.BlockSpec(w_out.shape, lambda i: (0, 0, 0), pipeline_mode=pl.Buffered(1)),
            pl.BlockSpec((1, d), const),
            pl.BlockSpec((tm, d), lambda i: (i, 0)),
        ],
        out_specs=pl.BlockSpec((tm, d), lambda i: (i, 0)),
        out_shape=jax.ShapeDtypeStruct((m, d), F32),
        scratch_shapes=[
            pltpu.VMEM((d // CONV_STRIP, tm + CONV_HALO, CONV_STRIP), F32),
            pltpu.VMEM((tm, d), F32),
        ],
        compiler_params=_params("parallel"),
        name="conv_out",
    )(u, u, w_dw, b_dw, ln_g, ln_b, w_out, b_out, x)


def _pool_kernel(x_ref, xprev_ref, g_ref, w_ref, scale_ref, o_ref, ext_ref, *, tm):
    i = pl.program_id(0)
    d = x_ref.shape[1]
    gc = d // len(POOL_WINDOWS)
    x = x_ref[...]
    h = _rms_norm(x, g_ref[...])
    hprev = jnp.where(i == 0, 0.0, _rms_norm(xprev_ref[...], g_ref[...]))
    t = i * tm + lax.broadcasted_iota(jnp.int32, (tm, 1), 0)
    strips = gc // V7X_LANES
    for g, w in enumerate(POOL_WINDOWS):
        cnt = jnp.minimum(t + 1, w).astype(F32)
        mixes = []
        for s in range(g * strips, (g + 1) * strips):
            lanes = slice(s * V7X_LANES, (s + 1) * V7X_LANES)
            ext_ref[s, 0:POOL_HALO, :] = hprev[:, lanes]
            ext_ref[s, POOL_HALO:, :] = h[:, lanes]
            total = h[:, lanes]
            for k in range(1, w):
                total = total + ext_ref[s, POOL_HALO - k:POOL_HALO - k + tm, :]
            mixes.append(total / cnt - h[:, lanes])
        cols = slice(g * gc, (g + 1) * gc)
        mix = jnp.concatenate(mixes, axis=1).astype(BF16)
        y = _dot(mix, w_ref[g]) * scale_ref[:, cols]
        o_ref[:, cols] = x[:, cols] + y


def _pool(x, gain, w_grp, scale, *, tm):
    m, d = x.shape
    ng, gc, _ = w_grp.shape
    hb = tm // POOL_HALO
    kernel = functools.partial(_pool_kernel, tm=tm)
    return pl.pallas_call(
        kernel,
        grid=(m // tm,),
        in_specs=[
            pl.BlockSpec((tm, d), lambda i: (i, 0)),
            pl.BlockSpec((POOL_HALO, d), lambda i: (jnp.maximum(i * hb - 1, 0), 0)),
            pl.BlockSpec((1, d), lambda i: (0, 0)),
            pl.BlockSpec((ng, gc, gc), lambda i: (0, 0, 0)),
            pl.BlockSpec((1, d), lambda i: (0, 0)),
        ],
        out_specs=pl.BlockSpec((tm, d), lambda i: (i, 0)),
        out_shape=jax.ShapeDtypeStruct((m, d), F32),
        scratch_shapes=[pltpu.VMEM((d // V7X_LANES, tm + POOL_HALO, V7X_LANES), F32)],
        compiler_params=_params("parallel"),
        name="pool",
    )(x, x, gain, w_grp, scale)


def _norm_matmul_body(x_ref, g_ref, w_ref, o_ref):
    o_ref[...] = _dot(_rms_norm(x_ref[...], g_ref[...]).astype(BF16), w_ref[...])


def _norm_matmul(x, gain, w, *, tm, tn, casts):
    m, d = x.shape
    n = w.shape[0] * tn
    return _fused_call(
        _norm_matmul_body,
        name="qkv_proj",
        grid=(n // tn, m // tm),
        j_outer=True,
        in_specs=[
            pl.BlockSpec((tm, d), lambda j, i: (i, 0)),
            pl.BlockSpec((1, d), lambda j, i: (0, 0)),
            pl.BlockSpec((None, d, tn), lambda j, i: (j, 0, 0)),
        ],
        inputs=(x, gain, w),
        out_spec=pl.BlockSpec((tm, tn), lambda j, i: (i, j)),
        out_shape=jax.ShapeDtypeStruct((m, n), F32),
        casts=casts,
    )


def _t5_bucket(rel):
    nb = NUM_BUCKETS // 2
    n = -rel
    ret = jnp.where(n < 0, nb, 0)
    n = jnp.abs(n)
    max_exact = nb // 2
    nf = jnp.maximum(n, 1).astype(F32)
    large = max_exact + (jnp.log(nf / max_exact) / math.log(REL_MAX_DIST / max_exact)
                         * (nb - max_exact)).astype(jnp.int32)
    large = jnp.minimum(large, nb - 1)
    return ret + jnp.where(n < max_exact, n, large)


def _band_bias_body(bucket_ref, rel_bias_ref, o_ref):
    heads = o_ref.shape[0]
    bucket = bucket_ref[...]
    for k in range(heads):
        h = pl.program_id(0) * heads + k
        acc = jnp.zeros(bucket.shape, F32)
        for b in range(NUM_BUCKETS):
            acc = jnp.where(bucket == b, rel_bias_ref[b, h], acc)
        o_ref[k] = acc


def _band_bias(rel_bias, *, casts):
    n_heads = rel_bias.shape[1]
    heads = _tile(n_heads, BIAS_HEADS, 1)
    i = jnp.arange(QBLOCK)[:, None]
    j = jnp.arange(2 * QBLOCK)[None, :]
    bucket = _t5_bucket(j - QBLOCK - i).astype(jnp.int32)
    return _fused_call(
        _band_bias_body,
        name="band_bias",
        grid=(n_heads // heads, 1),
        j_outer=False,
        in_specs=[
            pl.BlockSpec((QBLOCK, 2 * QBLOCK), lambda h, j: (0, 0)),
            pl.BlockSpec(memory_space=pltpu.SMEM),
        ],
        inputs=(bucket, rel_bias),
        out_spec=pl.BlockSpec((heads, QBLOCK, 2 * QBLOCK), lambda h, j: (h, 0, 0)),
        out_shape=jax.ShapeDtypeStruct((n_heads, QBLOCK, 2 * QBLOCK), F32),
        casts=casts,
    )


def _half_rms_norm(x, g, low):
    sq = x * x
    ms_lo = jnp.sum(jnp.where(low, sq, 0.0), axis=-1, keepdims=True) * (1.0 / HEAD_DIM)
    ms_hi = jnp.sum(jnp.where(low, 0.0, sq), axis=-1, keepdims=True) * (1.0 / HEAD_DIM)
    r = jnp.where(low, lax.rsqrt(ms_lo + EPS), lax.rsqrt(ms_hi + EPS))
    return x * r * g


def _attn_kernel(sinks_ref, q_ref, kc_ref, kp_ref, vc_ref, vp_ref, bias_ref, qg_ref, kg_ref, o_ref,
                 *, n_kv, group):
    n = pl.program_id(0)
    pair = 2 * HEAD_DIM
    n_heads = n_kv * group
    qi = lax.broadcasted_iota(jnp.int32, (QBLOCK, 2 * QBLOCK), 0)
    kj = lax.broadcasted_iota(jnp.int32, (QBLOCK, 2 * QBLOCK), 1)
    qc = qi // CHUNK
    kc = kj // CHUNK - QBLOCK // CHUNK
    visible = (kc <= qc) & (kc >= qc - WINDOW_CHUNKS) & ((n > 0) | (kj >= QBLOCK))
    low = lax.broadcasted_iota(jnp.int32, (1, pair), 1) < HEAD_DIM
    scale = HEAD_DIM ** -0.5

    q_all = jnp.concatenate([q_ref[:, c * pair:(c + 1) * pair] for c in range(n_heads // 2)], axis=0)
    q_all = _half_rms_norm(q_all, qg_ref[...], low) * scale
    k_all = jnp.concatenate([ref[:, c * pair:(c + 1) * pair]
                             for c in range(n_kv // 2) for ref in (kp_ref, kc_ref)], axis=0)
    k_all = _half_rms_norm(k_all, kg_ref[...], low).astype(BF16)

    logits, values = [], []
    heads_per_tile = 2 * group
    for hp in range(n_kv // 2):
        kcols = slice(hp * pair, (hp + 1) * pair)
        values.append(jnp.concatenate([vp_ref[:, kcols], vc_ref[:, kcols]], axis=0).astype(BF16))
        stacked = []
        for head in range(hp * heads_per_tile, (hp + 1) * heads_per_tile):
            qhalf, half = head % 2, (head // group) % 2
            q2 = q_all[(head // 2) * QBLOCK:(head // 2 + 1) * QBLOCK]
            qz = jnp.where(low == (qhalf == 0), q2, 0.0)
            if qhalf != half:
                qz = pltpu.roll(qz, HEAD_DIM, axis=1)
            stacked.append(qz.astype(BF16))
        k2 = k_all[hp * 2 * QBLOCK:(hp + 1) * 2 * QBLOCK]
        s = lax.dot_general(jnp.concatenate(stacked, axis=0), k2, (((1,), (1,)), ((), ())),
                            preferred_element_type=F32)
        for g in range(heads_per_tile):
            rows = s[g * QBLOCK:(g + 1) * QBLOCK]
            logits.append(jnp.where(visible, rows + bias_ref[hp * heads_per_tile + g], NEG_INF))

    s = jnp.concatenate(logits, axis=0)
    sink = jnp.concatenate([jnp.full((QBLOCK, 1), sinks_ref[h], F32) for h in range(n_heads)], axis=0)
    mx = jnp.maximum(jnp.max(s, axis=-1, keepdims=True), sink)
    e = jnp.exp(s - mx)
    denom = jnp.sum(e, axis=-1, keepdims=True) + jnp.exp(sink - mx)
    r = 1.0 / denom
    e = e.astype(BF16)

    for hp in range(n_kv // 2):
        rows = slice(hp * heads_per_tile * QBLOCK, (hp + 1) * heads_per_tile * QBLOCK)
        o = _dot(e[rows], values[hp]) * r[rows]
        for c in range(group):
            halves = []
            for qhalf in range(2):
                g = 2 * c + qhalf
                o2 = o[g * QBLOCK:(g + 1) * QBLOCK]
                if qhalf != (g // group) % 2:
                    o2 = pltpu.roll(o2, HEAD_DIM, axis=1)
                halves.append(o2)
            col = (hp * group + c) * pair
            o_ref[:, col:col + pair] = jnp.where(low, halves[0], halves[1]).astype(BF16)


def _attention(qkv, bias, sinks, q_gain, k_gain, *, n_heads, n_kv, casts):
    m = qkv.shape[0]
    group = n_heads // n_kv
    dq = n_heads * HEAD_DIM
    dkv = n_kv * HEAD_DIM
    kblk = dq // dkv
    prev = lambda n: jnp.maximum(n - 1, 0)
    return _fused_call(
        functools.partial(_attn_kernel, n_kv=n_kv, group=group),
        name="band_attention",
        grid=(m // QBLOCK, 1),
        j_outer=False,
        in_specs=[
            pl.BlockSpec(memory_space=pltpu.SMEM),
            pl.BlockSpec((QBLOCK, dq), lambda n, j: (n, 0)),
            pl.BlockSpec((QBLOCK, dkv), lambda n, j: (n, kblk)),
            pl.BlockSpec((QBLOCK, dkv), lambda n, j: (prev(n), kblk)),
            pl.BlockSpec((QBLOCK, dkv), lambda n, j: (n, kblk + 1)),
            pl.BlockSpec((QBLOCK, dkv), lambda n, j: (prev(n), kblk + 1)),
            pl.BlockSpec((n_heads, QBLOCK, 2 * QBLOCK), lambda n, j: (0, 0, 0)),
            pl.BlockSpec((1, 2 * HEAD_DIM), lambda n, j: (0, 0)),
            pl.BlockSpec((1, 2 * HEAD_DIM), lambda n, j: (0, 0)),
        ],
        inputs=(sinks, qkv, qkv, qkv, qkv, qkv, bias, q_gain, k_gain),
        out_spec=pl.BlockSpec((QBLOCK, dq), lambda n, j: (n, 0)),
        out_shape=jax.ShapeDtypeStruct((m, dq), BF16),
        casts=casts,
    )


def _matmul_residual_body(a_ref, w_ref, x_ref, o_ref):
    o_ref[...] = x_ref[...] + _dot(a_ref[...], w_ref[...])


def _matmul_residual(a, w, x, *, tm, tn, casts):
    m, k = a.shape
    n = w.shape[0] * tn
    return _fused_call(
        _matmul_residual_body,
        name="attn_out_proj",
        grid=(n // tn, m // tm),
        j_outer=True,
        in_specs=[
            pl.BlockSpec((tm, k), lambda j, i: (i, 0)),
            pl.BlockSpec((None, k, tn), lambda j, i: (j, 0, 0)),
            pl.BlockSpec((tm, tn), lambda j, i: (i, j)),
        ],
        inputs=(a, w, x),
        out_spec=pl.BlockSpec((tm, tn), lambda j, i: (i, j)),
        out_shape=jax.ShapeDtypeStruct((m, n), F32),
        casts=casts,
    )


def _ffn_body(x_ref, g_ref, wgu_ref, wd_ref, p_ref, pg_ref, wpg_ref, bpg_ref, wpp_ref, o_ref, h_ref):
    j = pl.program_id(1)

    @pl.when(j == 0)
    def _():
        x = x_ref[...]
        h_ref[...] = _rms_norm(x, g_ref[...]).astype(BF16)
        o_ref[...] = x

    gate_up = _dot(h_ref[...], wgu_ref[...])
    tf = wd_ref.shape[0]
    gate, up = gate_up[:, :tf], gate_up[:, tf:]
    act = (gate * jax.nn.sigmoid(gate) * up).astype(BF16)
    o_ref[...] += _dot(act, wd_ref[...])

    @pl.when(j == pl.num_programs(1) - 1)
    def _():
        x1 = o_ref[...]
        h1 = _rms_norm(x1, pg_ref[...]).astype(BF16)
        pb = p_ref[...].astype(BF16)
        tn = wpg_ref.shape[-1]
        for t in range(wpg_ref.shape[0]):
            cols = slice(t * tn, (t + 1) * tn)
            pgate = jax.nn.sigmoid(_dot(h1, wpg_ref[t]) + bpg_ref[:, cols])
            o_ref[:, cols] = x1[:, cols] + pgate * _dot(pb, wpp_ref[t])


def _ffn(x, gain, w_gate_up, w_down, p, layer, ple_gain, w_pgate, b_pgate, w_pproj, *, tm, tf, casts):
    m, d = x.shape
    f = w_gate_up.shape[0] * tf
    pd = p.shape[-1]
    resident = lambda a: pl.BlockSpec(a.shape, lambda i, j: (0,) * a.ndim, pipeline_mode=pl.Buffered(1))
    return _fused_call(
        _ffn_body,
        name="swiglu_ffn_gated_embedding",
        grid=(m // tm, f // tf),
        j_outer=False,
        in_specs=[
            pl.BlockSpec((tm, d), lambda i, j: (i, 0)),
            pl.BlockSpec((1, d), lambda i, j: (0, 0)),
            pl.BlockSpec((None, d, 2 * tf), lambda i, j: (j, 0, 0)),
            pl.BlockSpec((None, tf, d), lambda i, j: (0, j, 0)),
            pl.BlockSpec((None, tm, pd), lambda i, j: (layer, i, 0)),
            pl.BlockSpec((1, d), lambda i, j: (0, 0)),
            resident(w_pgate),
            pl.BlockSpec((1, d), lambda i, j: (0, 0)),
            resident(w_pproj),
        ],
        inputs=(x, gain, w_gate_up, w_down, p, ple_gain, w_pgate, b_pgate, w_pproj),
        out_spec=pl.BlockSpec((tm, d), lambda i, j: (i, 0)),
        out_shape=jax.ShapeDtypeStruct((m, d), F32),
        scratch=[pltpu.VMEM((tm, d), BF16)],
        casts=casts,
    )


def kernel(x, p, norm_mix, norm_ffn, norm_ple, conv_w_in, conv_b_in, conv_w_dw, conv_b_dw, conv_ln_g,
           conv_ln_b, conv_w_out, conv_b_out, pool_w, pool_scale, attn_w_qkv, attn_q_norm, attn_k_norm,
           attn_sinks, attn_w_o, rel_bias, ffn_w_gate, ffn_w_up, ffn_w_down, ple_w_proj, ple_w_gate,
           ple_b_gate):
    batch, seq, d = x.shape
    depth = norm_mix.shape[0]
    n_heads = d // HEAD_DIM
    n_kv = (attn_w_qkv.shape[-1] // HEAD_DIM - n_heads) // 2
    group = n_heads // n_kv
    assert n_kv % 2 == 0 and group % 2 == 0, "attention kernel pairs heads into 128-lane tiles"
    assert d % (len(POOL_WINDOWS) * V7X_LANES) == 0
    m = batch * seq
    assert batch == 1 and m % QBLOCK == 0, "row tiles assume one sequence"

    tm = _tile(m, ROW_TILE)
    tm2 = _tile(m, WIDE_ROW_TILE)
    tn = _tile(d, COL_TILE)
    tf = _tile(ffn_w_gate.shape[-1], FF_TILE)
    tq = _tile(attn_w_qkv.shape[-1], COL_TILE)
    row = lambda v: v.reshape(1, -1)

    pool_w2 = pool_w.reshape(pool_w.shape[0], -1, pool_w.shape[-1])

    def mixer_weights(i):
        kind, l = i % N_MIXERS, i // N_MIXERS
        if kind == 0:
            return {("conv_in", i): ((conv_w_in,), l, tn), ("conv_out", i): ((conv_w_out,), l, tn)}
        if kind == 1:
            return {("pool", i): ((pool_w2,), l, pool_w2.shape[-1])}
        return {("qkv", i): ((attn_w_qkv,), l, tq), ("attn_o", i): ((attn_w_o,), l, tn)}

    def ffn_weights(i):
        return {("gate_up", i): ((ffn_w_gate, ffn_w_up), i, tf), ("down", i): ((ffn_w_down,), i, d)}

    def ple_weights(i):
        return {("ple_gate", i): ((ple_w_gate,), i, tn), ("ple_proj", i): ((ple_w_proj,), i, tn)}

    ready = {}

    def bf16(key, source):
        return ready.pop(key) if key in ready else _cast(*source)

    def hosting(weights, call):
        out, copies = call(list(weights.values()))
        ready.update(zip(weights.keys(), copies))
        return out

    xs = x.reshape(m, d)
    bias = None
    if depth >= N_MIXERS:
        bias = hosting(mixer_weights(0), lambda c: _band_bias(rel_bias, casts=c))
    for i in range(depth):
        kind, l = i % N_MIXERS, i // N_MIXERS
        gain = row(norm_mix[i])
        mine = {} if ("gate_up", i) in ready else {**ffn_weights(i), **ple_weights(i)}
        if kind == 0:
            w_in, w_out = (bf16(k, w) for k, w in mixer_weights(i).items())
            u = hosting(mine, lambda c: _conv_in(xs, gain, w_in, row(conv_b_in[l]), tm=tm, tn=tn, casts=c))
            xs = _conv_out(u, xs, conv_w_dw[l], row(conv_b_dw[l]), row(conv_ln_g[l]), row(conv_ln_b[l]),
                           w_out, row(conv_b_out[l]), tm=tm)
        elif kind == 1:
            (w_grp,) = (bf16(k, w) for k, w in mixer_weights(i).items())
            w_grp = w_grp.reshape(pool_w.shape[1:])
            xs = _pool(xs, gain, w_grp, row(pool_scale[l]), tm=tm)
        else:
            w_qkv, w_o = (bf16(k, w) for k, w in mixer_weights(i).items())
            qkv, _ = _norm_matmul(xs, gain, w_qkv, tm=tm2, tn=tq, casts=[])
            o = hosting({**mine, **(mixer_weights(i + 1) if i + 1 < depth else {})}, lambda c: _attention(
                qkv, bias, attn_sinks[l], row(jnp.tile(attn_q_norm[l], 2)), row(jnp.tile(attn_k_norm[l], 2)),
                n_heads=n_heads, n_kv=n_kv, casts=c))
            xs, _ = _matmul_residual(o, w_o, xs, tm=tm2, tn=tn, casts=[])
        w_gate_up, w_down = (bf16(k, w) for k, w in ffn_weights(i).items())
        w_pg, w_pp = (bf16(k, w) for k, w in ple_weights(i).items())
        ahead = {}
        if i + 1 < depth:
            ahead = mixer_weights(i + 1)
            if (i + 1) % N_MIXERS == 1:
                ahead = {**ahead, **ffn_weights(i + 1), **ple_weights(i + 1)}
            ahead = {k: v for k, v in ahead.items() if k not in ready}
        xs = hosting(ahead, lambda c: _ffn(
            xs, row(norm_ffn[i]), w_gate_up, w_down, p.reshape(depth, m, -1), i, row(norm_ple[i]),
            w_pg, row(ple_b_gate[i]), w_pp, tm=tm, tf=tf, casts=c))
    return xs.reshape(batch, seq, d)
```

```python
import functools
import math

import jax
import jax.numpy as jnp
from jax import lax
from jax.experimental import pallas as pl
from jax.experimental.pallas import tpu as pltpu

N_MIXERS = 3
CHUNK = 64
CONV_WIDTH = 31
POOL_WINDOWS = (2, 4, 8, 16)
HEAD_DIM = 64
WINDOW_CHUNKS = 2
QBLOCK = 128
NUM_BUCKETS = 32
REL_MAX_DIST = 128
EPS = 1e-6
NEG_INF = -1e30

V7X_VMEM_BYTES = 64 * 1024 * 1024
V7X_LANES = 128
V7X_SUBLANES = 8
BF16_ROWS = 2 * V7X_SUBLANES
VMEM_LIMIT_BYTES = V7X_VMEM_BYTES * 7 // 8

CONV_HALO = 32
POOL_HALO = 16
CONV_STRIP = V7X_LANES

ROW_TILE = 512
WIDE_ROW_TILE = 1024
COL_TILE = 1024
FF_TILE = 512
CAST_ROWS = 256
BIAS_HEADS = 4

BF16 = jnp.bfloat16
F32 = jnp.float32


def _params(*semantics):
    return pltpu.CompilerParams(dimension_semantics=semantics, vmem_limit_bytes=VMEM_LIMIT_BYTES)


def _rms_norm(x, g):
    ms = jnp.mean(x * x, axis=-1, keepdims=True)
    return x * lax.rsqrt(ms + EPS) * g


def _dot(a, b):
    return jnp.dot(a, b, preferred_element_type=F32)


def _tile(n, target, multiple=V7X_LANES):
    if n <= target:
        return n
    t = target - target % multiple
    while n % t:
        t -= multiple
    return t


def _copy_tiles(src_ref, dst_ref):
    tile = dst_ref.shape[-1]
    for t in range(dst_ref.shape[0]):
        dst_ref[t] = src_ref[:, t * tile:(t + 1) * tile].astype(BF16)


def _cast(stack, layer, tile):
    _, r, c = stack.shape
    rb = _tile(r, CAST_ROWS, BF16_ROWS)
    return pl.pallas_call(
        _copy_tiles,
        grid=(r // rb,),
        in_specs=[pl.BlockSpec((None, rb, c), lambda i: (layer, i, 0))],
        out_specs=pl.BlockSpec((c // tile, rb, tile), lambda i: (0, i, 0)),
        out_shape=jax.ShapeDtypeStruct((c // tile, r, tile), BF16),
        compiler_params=_params("parallel"),
        name="weight_cast",
    )(stack)


def _cast_plan(sources, gi, gj, j_outer):
    in_specs, out_specs, out_shapes, row_only = [], [], [], []
    for stack, layer, tile in sources:
        _, r, c = stack.shape
        once = False
        if r % (gi * gj * BF16_ROWS) == 0:
            rows, cols = r // (gi * gj), c
            src_map = lambda i, j: (i * gj + j, 0)
            dst_map = lambda i, j: (0, i * gj + j, 0)
        elif r % (gi * BF16_ROWS) == 0 and c % gj == 0 and (c // gj) % tile == 0:
            rows, cols = r // gi, c // gj
            src_map = lambda i, j: (i, j)
            dst_map = lambda i, j: (j, i, 0)
        elif r % (gj * BF16_ROWS) == 0 and c % gi == 0 and (c // gi) % tile == 0:
            rows, cols = r // gj, c // gi
            src_map = lambda i, j: (j, i)
            dst_map = lambda i, j: (i, j, 0)
        elif r % (gi * BF16_ROWS) == 0 and not j_outer:
            rows, cols = r // gi, c
            src_map = lambda i, j: (i, 0)
            dst_map = lambda i, j: (0, i, 0)
            once = gj > 1
        else:
            n = max(k for k in range(1, gi * gj + 1) if r % (k * BF16_ROWS) == 0)
            step = (lambda i, j: j * gi + i) if j_outer else (lambda i, j: i * gj + j)
            rows, cols = r // n, c
            src_map = lambda i, j, n=n, step=step: (jnp.minimum(step(i, j), n - 1), 0)
            dst_map = lambda i, j, n=n, step=step: (0, jnp.minimum(step(i, j), n - 1), 0)
        if j_outer:
            src_map = (lambda f: lambda j, i: f(i, j))(src_map)
            dst_map = (lambda f: lambda j, i: f(i, j))(dst_map)
        src_map = (lambda f, l: lambda a, b: (l, *f(a, b)))(src_map, layer)
        in_specs.append(pl.BlockSpec((None, rows, cols), src_map))
        out_specs.append(pl.BlockSpec((cols // tile, rows, tile), dst_map))
        out_shapes.append(jax.ShapeDtypeStruct((c // tile, r, tile), BF16))
        row_only.append(once)
    return in_specs, out_specs, out_shapes, row_only


def _fused_call(body, *, name, grid, j_outer, in_specs, inputs, out_spec, out_shape, scratch=(), casts=()):
    n_in, n_cast = len(inputs), len(casts)
    gi, gj = (grid[1], grid[0]) if j_outer else grid
    cast_in, cast_out, cast_shapes, row_only = _cast_plan(casts, gi, gj, j_outer)

    def kernel(*refs):
        ins, rest = refs[:n_in], refs[n_in:]
        cast_src, rest = rest[:n_cast], rest[n_cast:]
        out, rest = rest[0], rest[1:]
        cast_dst, scr = rest[:n_cast], rest[n_cast:]
        for src, dst, once in zip(cast_src, cast_dst, row_only):
            if once:
                pl.when(pl.program_id(1) == 0)(functools.partial(_copy_tiles, src, dst))
            else:
                _copy_tiles(src, dst)
        body(*ins, out, *scr)

    res = pl.pallas_call(
        kernel,
        grid=grid,
        in_specs=[*in_specs, *cast_in],
        out_specs=[out_spec, *cast_out],
        out_shape=[out_shape, *cast_shapes],
        scratch_shapes=list(scratch),
        compiler_params=_params("parallel", "parallel" if j_outer else "arbitrary"),
        name=name,
    )(*inputs, *[stack for stack, _, _ in casts])
    return res[0], list(res[1:])


def _conv_in_body(x_ref, g_ref, wa_ref, wg_ref, ba_ref, bg_ref, u_ref):
    h = _rms_norm(x_ref[...], g_ref[...]).astype(BF16)
    a = _dot(h, wa_ref[...]) + ba_ref[...]
    gate = _dot(h, wg_ref[...]) + bg_ref[...]
    u_ref[...] = a * jax.nn.sigmoid(gate)


def _conv_in(x, gain, w_in, b_in, *, tm, tn, casts):
    m, d = x.shape
    nj = d // tn
    return _fused_call(
        _conv_in_body,
        name="conv_in",
        grid=(nj, m // tm),
        j_outer=True,
        in_specs=[
            pl.BlockSpec((tm, d), lambda j, i: (i, 0)),
            pl.BlockSpec((1, d), lambda j, i: (0, 0)),
            pl.BlockSpec((None, d, tn), lambda j, i: (j, 0, 0)),
            pl.BlockSpec((None, d, tn), lambda j, i: (j + nj, 0, 0)),
            pl.BlockSpec((1, tn), lambda j, i: (0, j)),
            pl.BlockSpec((1, tn), lambda j, i: (0, j + nj)),
        ],
        inputs=(x, gain, w_in, w_in, b_in, b_in),
        out_spec=pl.BlockSpec((tm, tn), lambda j, i: (i, j)),
        out_shape=jax.ShapeDtypeStruct((m, d), F32),
        casts=casts,
    )


def _conv_out_kernel(u_ref, uprev_ref, wdw_ref, bdw_ref, lng_ref, lnb_ref, wout_ref, bout_ref, x_ref,
                     o_ref, ext_ref, conv_ref, *, tm):
    i = pl.program_id(0)
    d = u_ref.shape[1]
    first = CONV_HALO - (CONV_WIDTH - 1)

    for s in range(d // CONV_STRIP):
        cols = slice(s * CONV_STRIP, (s + 1) * CONV_STRIP)
        ext_ref[s, 0:CONV_HALO, :] = jnp.where(i == 0, 0.0, uprev_ref[:, cols])
        ext_ref[s, CONV_HALO:, :] = u_ref[:, cols]
        acc = jnp.broadcast_to(bdw_ref[:, cols], (tm, CONV_STRIP))
        for k in range(CONV_WIDTH):
            acc = acc + wdw_ref[k:k + 1, cols] * ext_ref[s, first + k:first + k + tm, :]
        conv_ref[:, cols] = acc

    conv = conv_ref[...]
    mu = jnp.mean(conv, axis=-1, keepdims=True)
    xc = conv - mu
    y = xc * lax.rsqrt(jnp.mean(xc * xc, axis=-1, keepdims=True) + EPS)
    y = y * lng_ref[...] + lnb_ref[...]
    v = (y * jax.nn.sigmoid(y)).astype(BF16)
    tn = wout_ref.shape[-1]
    for t in range(wout_ref.shape[0]):
        cols = slice(t * tn, (t + 1) * tn)
        o_ref[:, cols] = x_ref[:, cols] + _dot(v, wout_ref[t]) + bout_ref[:, cols]


def _conv_out(u, x, w_dw, b_dw, ln_g, ln_b, w_out, b_out, *, tm):
    m, d = u.shape
    hb = tm // CONV_HALO
    const = lambda i: (0, 0)
    return pl.pallas_call(
        functools.partial(_conv_out_kernel, tm=tm),
        grid=(m // tm,),
        in_specs=[
            pl.BlockSpec((tm, d), lambda i: (i, 0)),
            pl.BlockSpec((CONV_HALO, d), lambda i: (jnp.maximum(i * hb - 1, 0), 0)),
            pl.BlockSpec((CONV_WIDTH, d), const),
            pl.BlockSpec((1, d), const),
            pl.BlockSpec((1, d), const),
            pl.BlockSpec((1, d), const),
            pl.BlockSpec(w_out.shape, lambda i: (0, 0, 0), pipeline_mode=pl.Buffered(1)),
            pl.BlockSpec((1, d), const),
            pl.BlockSpec((tm, d), lambda i: (i, 0)),
        ],
        out_specs=pl.BlockSpec((tm, d), lambda i: (i, 0)),
        out_shape=jax.ShapeDtypeStruct((m, d), F32),
        scratch_shapes=[
            pltpu.VMEM((d // CONV_STRIP, tm + CONV_HALO, CONV_STRIP), F32),
            pltpu.VMEM((tm, d), F32),
        ],
        compiler_params=_params("parallel"),
        name="conv_out",
    )(u, u, w_dw, b_dw, ln_g, ln_b, w_out, b_out, x)


def _pool_kernel(x_ref, xprev_ref, g_ref, w_ref, scale_ref, o_ref, ext_ref, *, tm):
    i = pl.program_id(0)
    d = x_ref.shape[1]
    gc = d // len(POOL_WINDOWS)
    x = x_ref[...]
    h = _rms_norm(x, g_ref[...])
    hprev = jnp.where(i == 0, 0.0, _rms_norm(xprev_ref[...], g_ref[...]))
    t = i * tm + lax.broadcasted_iota(jnp.int32, (tm, 1), 0)
    strips = gc // V7X_LANES
    for g, w in enumerate(POOL_WINDOWS):
        cnt = jnp.minimum(t + 1, w).astype(F32)
        mixes = []
        for s in range(g * strips, (g + 1) * strips):
            lanes = slice(s * V7X_LANES, (s + 1) * V7X_LANES)
            ext_ref[s, 0:POOL_HALO, :] = hprev[:, lanes]
            ext_ref[s, POOL_HALO:, :] = h[:, lanes]
            total = h[:, lanes]
            for k in range(1, w):
                total = total + ext_ref[s, POOL_HALO - k:POOL_HALO - k + tm, :]
            mixes.append(total / cnt - h[:, lanes])
        cols = slice(g * gc, (g + 1) * gc)
        mix = jnp.concatenate(mixes, axis=1).astype(BF16)
        y = _dot(mix, w_ref[g]) * scale_ref[:, cols]
        o_ref[:, cols] = x[:, cols] + y


def _pool(x, gain, w_grp, scale, *, tm):
    m, d = x.shape
    ng, gc, _ = w_grp.shape
    hb = tm // POOL_HALO
    kernel = functools.partial(_pool_kernel, tm=tm)
    return pl.pallas_call(
        kernel,
        grid=(m // tm,),
        in_specs=[
            pl.BlockSpec((tm, d), lambda i: (i, 0)),
            pl.BlockSpec((POOL_HALO, d), lambda i: (jnp.maximum(i * hb - 1, 0), 0)),
            pl.BlockSpec((1, d), lambda i: (0, 0)),
            pl.BlockSpec((ng, gc, gc), lambda i: (0, 0, 0)),
            pl.BlockSpec((1, d), lambda i: (0, 0)),
        ],
        out_specs=pl.BlockSpec((tm, d), lambda i: (i, 0)),
        out_shape=jax.ShapeDtypeStruct((m, d), F32),
        scratch_shapes=[pltpu.VMEM((d // V7X_LANES, tm + POOL_HALO, V7X_LANES), F32)],
        compiler_params=_params("parallel"),
        name="pool",
    )(x, x, gain, w_grp, scale)


def _norm_matmul_body(x_ref, g_ref, w_ref, o_ref):
    o_ref[...] = _dot(_rms_norm(x_ref[...], g_ref[...]).astype(BF16), w_ref[...])


def _norm_matmul(x, gain, w, *, tm, tn, casts):
    m, d = x.shape
    n = w.shape[0] * tn
    return _fused_call(
        _norm_matmul_body,
        name="qkv_proj",
        grid=(n // tn, m // tm),
        j_outer=True,
        in_specs=[
            pl.BlockSpec((tm, d), lambda j, i: (i, 0)),
            pl.BlockSpec((1, d), lambda j, i: (0, 0)),
            pl.BlockSpec((None, d, tn), lambda j, i: (j, 0, 0)),
        ],
        inputs=(x, gain, w),
        out_spec=pl.BlockSpec((tm, tn), lambda j, i: (i, j)),
        out_shape=jax.ShapeDtypeStruct((m, n), F32),
        casts=casts,
    )


def _t5_bucket(rel):
    nb = NUM_BUCKETS // 2
    n = -rel
    ret = jnp.where(n < 0, nb, 0)
    n = jnp.abs(n)
    max_exact = nb // 2
    nf = jnp.maximum(n, 1).astype(F32)
    large = max_exact + (jnp.log(nf / max_exact) / math.log(REL_MAX_DIST / max_exact)
                         * (nb - max_exact)).astype(jnp.int32)
    large = jnp.minimum(large, nb - 1)
    return ret + jnp.where(n < max_exact, n, large)


def _band_bias_body(bucket_ref, rel_bias_ref, o_ref):
    heads = o_ref.shape[0]
    bucket = bucket_ref[...]
    for k in range(heads):
        h = pl.program_id(0) * heads + k
        acc = jnp.zeros(bucket.shape, F32)
        for b in range(NUM_BUCKETS):
            acc = jnp.where(bucket == b, rel_bias_ref[b, h], acc)
        o_ref[k] = acc


def _band_bias(rel_bias, *, casts):
    n_heads = rel_bias.shape[1]
    heads = _tile(n_heads, BIAS_HEADS, 1)
    i = jnp.arange(QBLOCK)[:, None]
    j = jnp.arange(2 * QBLOCK)[None, :]
    bucket = _t5_bucket(j - QBLOCK - i).astype(jnp.int32).T
    return _fused_call(
        _band_bias_body,
        name="band_bias",
        grid=(n_heads // heads, 1),
        j_outer=False,
        in_specs=[
            pl.BlockSpec((2 * QBLOCK, QBLOCK), lambda h, j: (0, 0)),
            pl.BlockSpec(memory_space=pltpu.SMEM),
        ],
        inputs=(bucket, rel_bias),
        out_spec=pl.BlockSpec((heads, 2 * QBLOCK, QBLOCK), lambda h, j: (h, 0, 0)),
        out_shape=jax.ShapeDtypeStruct((n_heads, 2 * QBLOCK, QBLOCK), F32),
        casts=casts,
    )


def _half_rms_norm(x, g, low):
    sq = x * x
    ms_lo = jnp.sum(jnp.where(low, sq, 0.0), axis=-1, keepdims=True) * (1.0 / HEAD_DIM)
    ms_hi = jnp.sum(jnp.where(low, 0.0, sq), axis=-1, keepdims=True) * (1.0 / HEAD_DIM)
    r = jnp.where(low, lax.rsqrt(ms_lo + EPS), lax.rsqrt(ms_hi + EPS))
    return x * r * g


def _attn_kernel(sinks_ref, q_ref, kc_ref, kp_ref, vc_ref, vp_ref, bias_ref, qg_ref, kg_ref, o_ref,
                 *, n_kv, group):
    n = pl.program_id(0)
    pair = 2 * HEAD_DIM
    n_heads = n_kv * group
    kj = lax.broadcasted_iota(jnp.int32, (2 * QBLOCK, QBLOCK), 0)
    qi = lax.broadcasted_iota(jnp.int32, (2 * QBLOCK, QBLOCK), 1)
    qc = qi // CHUNK
    kc = kj // CHUNK - QBLOCK // CHUNK
    visible = (kc <= qc) & (kc >= qc - WINDOW_CHUNKS) & ((n > 0) | (kj >= QBLOCK))
    low = lax.broadcasted_iota(jnp.int32, (1, pair), 1) < HEAD_DIM
    scale = HEAD_DIM ** -0.5

    q_all = jnp.concatenate([q_ref[:, c * pair:(c + 1) * pair] for c in range(n_heads // 2)], axis=0)
    q_all = _half_rms_norm(q_all, qg_ref[...], low) * scale
    k_all = jnp.concatenate([ref[:, c * pair:(c + 1) * pair]
                             for c in range(n_kv // 2) for ref in (kp_ref, kc_ref)], axis=0)
    k_all = _half_rms_norm(k_all, kg_ref[...], low).astype(BF16)

    logits, values = [], []
    heads_per_tile = 2 * group
    for hp in range(n_kv // 2):
        kcols = slice(hp * pair, (hp + 1) * pair)
        values.append(jnp.concatenate([vp_ref[:, kcols].T, vc_ref[:, kcols].T], axis=1).astype(BF16))
        stacked = []
        for head in range(hp * heads_per_tile, (hp + 1) * heads_per_tile):
            qhalf, half = head % 2, (head // group) % 2
            q2 = q_all[(head // 2) * QBLOCK:(head // 2 + 1) * QBLOCK]
            qz = jnp.where(low == (qhalf == 0), q2, 0.0)
            if qhalf != half:
                qz = pltpu.roll(qz, HEAD_DIM, axis=1)
            stacked.append(qz.astype(BF16))
        k2 = k_all[hp * 2 * QBLOCK:(hp + 1) * 2 * QBLOCK]
        st = lax.dot_general(k2, jnp.concatenate(stacked, axis=0), (((1,), (1,)), ((), ())),
                             preferred_element_type=F32)
        for g in range(heads_per_tile):
            cols = st[:, g * QBLOCK:(g + 1) * QBLOCK]
            logits.append(jnp.where(visible, cols + bias_ref[hp * heads_per_tile + g], NEG_INF))

    s = jnp.concatenate(logits, axis=1)
    sink = jnp.concatenate([jnp.full((1, QBLOCK), sinks_ref[h], F32) for h in range(n_heads)], axis=1)
    mx = jnp.maximum(jnp.max(s, axis=0, keepdims=True), sink)
    e = jnp.exp(s - mx)
    denom = jnp.sum(e, axis=0, keepdims=True) + jnp.exp(sink - mx)
    r = 1.0 / denom
    e = e.astype(BF16)

    for hp in range(n_kv // 2):
        cols = slice(hp * heads_per_tile * QBLOCK, (hp + 1) * heads_per_tile * QBLOCK)
        ot = _dot(values[hp], e[:, cols]) * r[:, cols]
        for c in range(group):
            parts = []
            for qhalf in range(2):
                g = 2 * c + qhalf
                half = (g // group) % 2
                parts.append(ot[half * HEAD_DIM:(half + 1) * HEAD_DIM, g * QBLOCK:(g + 1) * QBLOCK])
            col = (hp * group + c) * pair
            o_ref[:, col:col + pair] = jnp.concatenate(parts, axis=0).T.astype(BF16)


def _attention(qkv, bias, sinks, q_gain, k_gain, *, n_heads, n_kv, casts):
    m = qkv.shape[0]
    group = n_heads // n_kv
    dq = n_heads * HEAD_DIM
    dkv = n_kv * HEAD_DIM
    kblk = dq // dkv
    prev = lambda n: jnp.maximum(n - 1, 0)
    return _fused_call(
        functools.partial(_attn_kernel, n_kv=n_kv, group=group),
        name="band_attention",
        grid=(m // QBLOCK, 1),
        j_outer=False,
        in_specs=[
            pl.BlockSpec(memory_space=pltpu.SMEM),
            pl.BlockSpec((QBLOCK, dq), lambda n, j: (n, 0)),
            pl.BlockSpec((QBLOCK, dkv), lambda n, j: (n, kblk)),
            pl.BlockSpec((QBLOCK, dkv), lambda n, j: (prev(n), kblk)),
            pl.BlockSpec((QBLOCK, dkv), lambda n, j: (n, kblk + 1)),
            pl.BlockSpec((QBLOCK, dkv), lambda n, j: (prev(n), kblk + 1)),
            pl.BlockSpec((n_heads, 2 * QBLOCK, QBLOCK), lambda n, j: (0, 0, 0)),
            pl.BlockSpec((1, 2 * HEAD_DIM), lambda n, j: (0, 0)),
            pl.BlockSpec((1, 2 * HEAD_DIM), lambda n, j: (0, 0)),
        ],
        inputs=(sinks, qkv, qkv, qkv, qkv, qkv, bias, q_gain, k_gain),
        out_spec=pl.BlockSpec((QBLOCK, dq), lambda n, j: (n, 0)),
        out_shape=jax.ShapeDtypeStruct((m, dq), BF16),
        casts=casts,
    )


def _matmul_residual_body(a_ref, w_ref, x_ref, o_ref):
    o_ref[...] = x_ref[...] + _dot(a_ref[...], w_ref[...])


def _matmul_residual(a, w, x, *, tm, tn, casts):
    m, k = a.shape
    n = w.shape[0] * tn
    return _fused_call(
        _matmul_residual_body,
        name="attn_out_proj",
        grid=(n // tn, m // tm),
        j_outer=True,
        in_specs=[
            pl.BlockSpec((tm, k), lambda j, i: (i, 0)),
            pl.BlockSpec((None, k, tn), lambda j, i: (j, 0, 0)),
            pl.BlockSpec((tm, tn), lambda j, i: (i, j)),
        ],
        inputs=(a, w, x),
        out_spec=pl.BlockSpec((tm, tn), lambda j, i: (i, j)),
        out_shape=jax.ShapeDtypeStruct((m, n), F32),
        casts=casts,
    )


def _ffn_body(x_ref, g_ref, wg_ref, wu_ref, wd_ref, p_ref, pg_ref, wpg_ref, bpg_ref, wpp_ref, o_ref, h_ref):
    j = pl.program_id(1)

    @pl.when(j == 0)
    def _():
        x = x_ref[...]
        h_ref[...] = _rms_norm(x, g_ref[...]).astype(BF16)
        o_ref[...] = x

    h = h_ref[...]
    gate = _dot(h, wg_ref[...])
    up = _dot(h, wu_ref[...])
    act = (gate * jax.nn.sigmoid(gate) * up).astype(BF16)
    o_ref[...] += _dot(act, wd_ref[...])

    @pl.when(j == pl.num_programs(1) - 1)
    def _():
        x1 = o_ref[...]
        h1 = _rms_norm(x1, pg_ref[...]).astype(BF16)
        pb = p_ref[...].astype(BF16)
        tn = wpg_ref.shape[-1]
        for t in range(wpg_ref.shape[0]):
            cols = slice(t * tn, (t + 1) * tn)
            pgate = jax.nn.sigmoid(_dot(h1, wpg_ref[t]) + bpg_ref[:, cols])
            o_ref[:, cols] = x1[:, cols] + pgate * _dot(pb, wpp_ref[t])


def _ffn(x, gain, w_gate, w_up, w_down, p, layer, ple_gain, w_pgate, b_pgate, w_pproj, *, tm, tf, casts):
    m, d = x.shape
    f = w_gate.shape[0] * tf
    pd = p.shape[-1]
    resident = lambda a: pl.BlockSpec(a.shape, lambda i, j: (0,) * a.ndim, pipeline_mode=pl.Buffered(1))
    return _fused_call(
        _ffn_body,
        name="swiglu_ffn_gated_embedding",
        grid=(m // tm, f // tf),
        j_outer=False,
        in_specs=[
            pl.BlockSpec((tm, d), lambda i, j: (i, 0)),
            pl.BlockSpec((1, d), lambda i, j: (0, 0)),
            pl.BlockSpec((None, d, tf), lambda i, j: (j, 0, 0)),
            pl.BlockSpec((None, d, tf), lambda i, j: (j, 0, 0)),
            pl.BlockSpec((None, tf, d), lambda i, j: (0, j, 0)),
            pl.BlockSpec((None, tm, pd), lambda i, j: (layer, i, 0)),
            pl.BlockSpec((1, d), lambda i, j: (0, 0)),
            resident(w_pgate),
            pl.BlockSpec((1, d), lambda i, j: (0, 0)),
            resident(w_pproj),
        ],
        inputs=(x, gain, w_gate, w_up, w_down, p, ple_gain, w_pgate, b_pgate, w_pproj),
        out_spec=pl.BlockSpec((tm, d), lambda i, j: (i, 0)),
        out_shape=jax.ShapeDtypeStruct((m, d), F32),
        scratch=[pltpu.VMEM((tm, d), BF16)],
        casts=casts,
    )


def kernel(x, p, norm_mix, norm_ffn, norm_ple, conv_w_in, conv_b_in, conv_w_dw, conv_b_dw, conv_ln_g,
           conv_ln_b, conv_w_out, conv_b_out, pool_w, pool_scale, attn_w_qkv, attn_q_norm, attn_k_norm,
           attn_sinks, attn_w_o, rel_bias, ffn_w_gate, ffn_w_up, ffn_w_down, ple_w_proj, ple_w_gate,
           ple_b_gate):
    batch, seq, d = x.shape
    depth = norm_mix.shape[0]
    n_heads = d // HEAD_DIM
    n_kv = (attn_w_qkv.shape[-1] // HEAD_DIM - n_heads) // 2
    group = n_heads // n_kv
    assert n_kv % 2 == 0 and group % 2 == 0, "attention kernel pairs heads into 128-lane tiles"
    assert d % (len(POOL_WINDOWS) * V7X_LANES) == 0
    m = batch * seq
    assert batch == 1 and m % QBLOCK == 0, "row tiles assume one sequence"

    tm = _tile(m, ROW_TILE)
    tm2 = _tile(m, WIDE_ROW_TILE)
    tn = _tile(d, COL_TILE)
    tf = _tile(ffn_w_gate.shape[-1], FF_TILE)
    tq = _tile(attn_w_qkv.shape[-1], COL_TILE)
    row = lambda v: v.reshape(1, -1)

    pool_w2 = pool_w.reshape(pool_w.shape[0], -1, pool_w.shape[-1])

    def mixer_weights(i):
        kind, l = i % N_MIXERS, i // N_MIXERS
        if kind == 0:
            return {("conv_in", i): (conv_w_in, l, tn), ("conv_out", i): (conv_w_out, l, tn)}
        if kind == 1:
            return {("pool", i): (pool_w2, l, pool_w2.shape[-1])}
        return {("qkv", i): (attn_w_qkv, l, tq), ("attn_o", i): (attn_w_o, l, tn)}

    def ffn_weights(i):
        return {("gate", i): (ffn_w_gate, i, tf), ("up", i): (ffn_w_up, i, tf), ("down", i): (ffn_w_down, i, d)}

    def ple_weights(i):
        return {("ple_gate", i): (ple_w_gate, i, tn), ("ple_proj", i): (ple_w_proj, i, tn)}

    ready = {}

    def bf16(key, source):
        return ready.pop(key) if key in ready else _cast(*source)

    def hosting(weights, call):
        out, copies = call(list(weights.values()))
        ready.update(zip(weights.keys(), copies))
        return out

    xs = x.reshape(m, d)
    bias = None
    if depth >= N_MIXERS:
        bias = hosting(mixer_weights(0), lambda c: _band_bias(rel_bias, casts=c))
    for i in range(depth):
        kind, l = i % N_MIXERS, i // N_MIXERS
        gain = row(norm_mix[i])
        mine = {} if ("gate", i) in ready else {**ffn_weights(i), **ple_weights(i)}
        if kind == 0:
            w_in, w_out = (bf16(k, w) for k, w in mixer_weights(i).items())
            u = hosting(mine, lambda c: _conv_in(xs, gain, w_in, row(conv_b_in[l]), tm=tm, tn=tn, casts=c))
            xs = _conv_out(u, xs, conv_w_dw[l], row(conv_b_dw[l]), row(conv_ln_g[l]), row(conv_ln_b[l]),
                           w_out, row(conv_b_out[l]), tm=tm)
        elif kind == 1:
            (w_grp,) = (bf16(k, w) for k, w in mixer_weights(i).items())
            w_grp = w_grp.reshape(pool_w.shape[1:])
            xs = _pool(xs, gain, w_grp, row(pool_scale[l]), tm=tm)
        else:
            w_qkv, w_o = (bf16(k, w) for k, w in mixer_weights(i).items())
            qkv, _ = _norm_matmul(xs, gain, w_qkv, tm=tm2, tn=tq, casts=[])
            o = hosting({**mine, **(mixer_weights(i + 1) if i + 1 < depth else {})}, lambda c: _attention(
                qkv, bias, attn_sinks[l], row(jnp.tile(attn_q_norm[l], 2)), row(jnp.tile(attn_k_norm[l], 2)),
                n_heads=n_heads, n_kv=n_kv, casts=c))
            xs, _ = _matmul_residual(o, w_o, xs, tm=tm2, tn=tn, casts=[])
        w_gate, w_up, w_down = (bf16(k, w) for k, w in ffn_weights(i).items())
        w_pg, w_pp = (bf16(k, w) for k, w in ple_weights(i).items())
        ahead = {}
        if i + 1 < depth:
            ahead = mixer_weights(i + 1)
            if (i + 1) % N_MIXERS == 1:
                ahead = {**ahead, **ffn_weights(i + 1), **ple_weights(i + 1)}
            ahead = {k: v for k, v in ahead.items() if k not in ready}
        xs = hosting(ahead, lambda c: _ffn(
            xs, row(norm_ffn[i]), w_gate, w_up, w_down, p.reshape(depth, m, -1), i, row(norm_ple[i]),
            w_pg, row(ple_b_gate[i]), w_pp, tm=tm, tf=tf, casts=c))
    return xs.reshape(batch, seq, d)
```

```python
import functools
import math

import jax
import jax.numpy as jnp
from jax import lax
from jax.experimental import pallas as pl
from jax.experimental.pallas import tpu as pltpu

N_MIXERS = 3
CHUNK = 64
CONV_WIDTH = 31
POOL_WINDOWS = (2, 4, 8, 16)
HEAD_DIM = 64
WINDOW_CHUNKS = 2
QBLOCK = 128
NUM_BUCKETS = 32
REL_MAX_DIST = 128
EPS = 1e-6
NEG_INF = -1e30

V7X_VMEM_BYTES = 64 * 1024 * 1024
V7X_LANES = 128
V7X_SUBLANES = 8
BF16_ROWS = 2 * V7X_SUBLANES
VMEM_LIMIT_BYTES = V7X_VMEM_BYTES * 7 // 8

CONV_HALO = 32
POOL_HALO = 16
CONV_STRIP = V7X_LANES

ROW_TILE = 512
WIDE_ROW_TILE = 1024
COL_TILE = 1024
FF_TILE = 512
CAST_ROWS = 256
BIAS_HEADS = 4

BF16 = jnp.bfloat16
F32 = jnp.float32


def _params(*semantics):
    return pltpu.CompilerParams(dimension_semantics=semantics, vmem_limit_bytes=VMEM_LIMIT_BYTES)


def _rms_norm(x, g):
    ms = jnp.mean(x * x, axis=-1, keepdims=True)
    return x * lax.rsqrt(ms + EPS) * g


def _dot(a, b):
    return jnp.dot(a, b, preferred_element_type=F32)


def _tile(n, target, multiple=V7X_LANES):
    if n <= target:
        return n
    t = target - target % multiple
    while n % t:
        t -= multiple
    return t


def _copy_tiles(src_ref, dst_ref):
    tile = dst_ref.shape[-1]
    for t in range(dst_ref.shape[0]):
        dst_ref[t] = src_ref[:, t * tile:(t + 1) * tile].astype(BF16)


def _cast(stack, layer, tile):
    _, r, c = stack.shape
    rb = _tile(r, CAST_ROWS, BF16_ROWS)
    return pl.pallas_call(
        _copy_tiles,
        grid=(r // rb,),
        in_specs=[pl.BlockSpec((None, rb, c), lambda i: (layer, i, 0))],
        out_specs=pl.BlockSpec((c // tile, rb, tile), lambda i: (0, i, 0)),
        out_shape=jax.ShapeDtypeStruct((c // tile, r, tile), BF16),
        compiler_params=_params("parallel"),
        name="weight_cast",
    )(stack)


def _cast_plan(sources, gi, gj, j_outer):
    in_specs, out_specs, out_shapes, row_only = [], [], [], []
    for stack, layer, tile in sources:
        _, r, c = stack.shape
        once = False
        if r % (gi * gj * BF16_ROWS) == 0:
            rows, cols = r // (gi * gj), c
            src_map = lambda i, j: (i * gj + j, 0)
            dst_map = lambda i, j: (0, i * gj + j, 0)
        elif r % (gi * BF16_ROWS) == 0 and c % gj == 0 and (c // gj) % tile == 0:
            rows, cols = r // gi, c // gj
            src_map = lambda i, j: (i, j)
            dst_map = lambda i, j: (j, i, 0)
        elif r % (gj * BF16_ROWS) == 0 and c % gi == 0 and (c // gi) % tile == 0:
            rows, cols = r // gj, c // gi
            src_map = lambda i, j: (j, i)
            dst_map = lambda i, j: (i, j, 0)
        elif r % (gi * BF16_ROWS) == 0 and not j_outer:
            rows, cols = r // gi, c
            src_map = lambda i, j: (i, 0)
            dst_map = lambda i, j: (0, i, 0)
            once = gj > 1
        else:
            n = max(k for k in range(1, gi * gj + 1) if r % (k * BF16_ROWS) == 0)
            step = (lambda i, j: j * gi + i) if j_outer else (lambda i, j: i * gj + j)
            rows, cols = r // n, c
            src_map = lambda i, j, n=n, step=step: (jnp.minimum(step(i, j), n - 1), 0)
            dst_map = lambda i, j, n=n, step=step: (0, jnp.minimum(step(i, j), n - 1), 0)
        if j_outer:
            src_map = (lambda f: lambda j, i: f(i, j))(src_map)
            dst_map = (lambda f: lambda j, i: f(i, j))(dst_map)
        src_map = (lambda f, l: lambda a, b: (l, *f(a, b)))(src_map, layer)
        in_specs.append(pl.BlockSpec((None, rows, cols), src_map))
        out_specs.append(pl.BlockSpec((cols // tile, rows, tile), dst_map))
        out_shapes.append(jax.ShapeDtypeStruct((c // tile, r, tile), BF16))
        row_only.append(once)
    return in_specs, out_specs, out_shapes, row_only


def _fused_call(body, *, name, grid, j_outer, in_specs, inputs, out_spec, out_shape, scratch=(), casts=()):
    n_in, n_cast = len(inputs), len(casts)
    gi, gj = (grid[1], grid[0]) if j_outer else grid
    cast_in, cast_out, cast_shapes, row_only = _cast_plan(casts, gi, gj, j_outer)

    def kernel(*refs):
        ins, rest = refs[:n_in], refs[n_in:]
        cast_src, rest = rest[:n_cast], rest[n_cast:]
        out, rest = rest[0], rest[1:]
        cast_dst, scr = rest[:n_cast], rest[n_cast:]
        for src, dst, once in zip(cast_src, cast_dst, row_only):
            if once:
                pl.when(pl.program_id(1) == 0)(functools.partial(_copy_tiles, src, dst))
            else:
                _copy_tiles(src, dst)
        body(*ins, out, *scr)

    res = pl.pallas_call(
        kernel,
        grid=grid,
        in_specs=[*in_specs, *cast_in],
        out_specs=[out_spec, *cast_out],
        out_shape=[out_shape, *cast_shapes],
        scratch_shapes=list(scratch),
        compiler_params=_params("parallel", "parallel" if j_outer else "arbitrary"),
        name=name,
    )(*inputs, *[stack for stack, _, _ in casts])
    return res[0], list(res[1:])


def _conv_in_body(x_ref, g_ref, wa_ref, wg_ref, ba_ref, bg_ref, u_ref):
    h = _rms_norm(x_ref[...], g_ref[...]).astype(BF16)
    a = _dot(h, wa_ref[...]) + ba_ref[...]
    gate = _dot(h, wg_ref[...]) + bg_ref[...]
    u_ref[...] = a * jax.nn.sigmoid(gate)


def _conv_in(x, gain, w_in, b_in, *, tm, tn, casts):
    m, d = x.shape
    nj = d // tn
    return _fused_call(
        _conv_in_body,
        name="conv_in",
        grid=(nj, m // tm),
        j_outer=True,
        in_specs=[
            pl.BlockSpec((tm, d), lambda j, i: (i, 0)),
            pl.BlockSpec((1, d), lambda j, i: (0, 0)),
            pl.BlockSpec((None, d, tn), lambda j, i: (j, 0, 0)),
            pl.BlockSpec((None, d, tn), lambda j, i: (j + nj, 0, 0)),
            pl.BlockSpec((1, tn), lambda j, i: (0, j)),
            pl.BlockSpec((1, tn), lambda j, i: (0, j + nj)),
        ],
        inputs=(x, gain, w_in, w_in, b_in, b_in),
        out_spec=pl.BlockSpec((tm, tn), lambda j, i: (i, j)),
        out_shape=jax.ShapeDtypeStruct((m, d), F32),
        casts=casts,
    )


def _conv_out_kernel(u_ref, uprev_ref, wdw_ref, bdw_ref, lng_ref, lnb_ref, wout_ref, bout_ref, x_ref,
                     o_ref, ext_ref, conv_ref, *, tm):
    i = pl.program_id(0)
    d = u_ref.shape[1]
    first = CONV_HALO - (CONV_WIDTH - 1)

    for s in range(d // CONV_STRIP):
        cols = slice(s * CONV_STRIP, (s + 1) * CONV_STRIP)
        ext_ref[s, 0:CONV_HALO, :] = jnp.where(i == 0, 0.0, uprev_ref[:, cols])
        ext_ref[s, CONV_HALO:, :] = u_ref[:, cols]
        acc = jnp.broadcast_to(bdw_ref[:, cols], (tm, CONV_STRIP))
        for k in range(CONV_WIDTH):
            acc = acc + wdw_ref[k:k + 1, cols] * ext_ref[s, first + k:first + k + tm, :]
        conv_ref[:, cols] = acc

    conv = conv_ref[...]
    mu = jnp.mean(conv, axis=-1, keepdims=True)
    xc = conv - mu
    y = xc * lax.rsqrt(jnp.mean(xc * xc, axis=-1, keepdims=True) + EPS)
    y = y * lng_ref[...] + lnb_ref[...]
    v = (y * jax.nn.sigmoid(y)).astype(BF16)
    tn = wout_ref.shape[-1]
    for t in range(wout_ref.shape[0]):
        cols = slice(t * tn, (t + 1) * tn)
        o_ref[:, cols] = x_ref[:, cols] + _dot(v, wout_ref[t]) + bout_ref[:, cols]


def _conv_out(u, x, w_dw, b_dw, ln_g, ln_b, w_out, b_out, *, tm):
    m, d = u.shape
    hb = tm // CONV_HALO
    const = lambda i: (0, 0)
    return pl.pallas_call(
        functools.partial(_conv_out_kernel, tm=tm),
        grid=(m // tm,),
        in_specs=[
            pl.BlockSpec((tm, d), lambda i: (i, 0)),
            pl.BlockSpec((CONV_HALO, d), lambda i: (jnp.maximum(i * hb - 1, 0), 0)),
            pl.BlockSpec((CONV_WIDTH, d), const),
            pl.BlockSpec((1, d), const),
            pl.BlockSpec((1, d), const),
            pl.BlockSpec((1, d), const),
            pl.BlockSpec(w_out.shape, lambda i: (0, 0, 0), pipeline_mode=pl.Buffered(1)),
            pl.BlockSpec((1, d), const),
            pl.BlockSpec((tm, d), lambda i: (i, 0)),
        ],
        out_specs=pl.BlockSpec((tm, d), lambda i: (i, 0)),
        out_shape=jax.ShapeDtypeStruct((m, d), F32),
        scratch_shapes=[
            pltpu.VMEM((d // CONV_STRIP, tm + CONV_HALO, CONV_STRIP), F32),
            pltpu.VMEM((tm, d), F32),
        ],
        compiler_params=_params("parallel"),
        name="conv_out",
    )(u, u, w_dw, b_dw, ln_g, ln_b, w_out, b_out, x)


def _pool_kernel(x_ref, xprev_ref, g_ref, w_ref, scale_ref, o_ref, ext_ref, *, tm):
    i = pl.program_id(0)
    d = x_ref.shape[1]
    gc = d // len(POOL_WINDOWS)
    x = x_ref[...]
    h = _rms_norm(x, g_ref[...])
    hprev = jnp.where(i == 0, 0.0, _rms_norm(xprev_ref[...], g_ref[...]))
    t = i * tm + lax.broadcasted_iota(jnp.int32, (tm, 1), 0)
    strips = gc // V7X_LANES
    for g, w in enumerate(POOL_WINDOWS):
        cnt = jnp.minimum(t + 1, w).astype(F32)
        mixes = []
        for s in range(g * strips, (g + 1) * strips):
            lanes = slice(s * V7X_LANES, (s + 1) * V7X_LANES)
            ext_ref[s, 0:POOL_HALO, :] = hprev[:, lanes]
            ext_ref[s, POOL_HALO:, :] = h[:, lanes]
            total = h[:, lanes]
            for k in range(1, w):
                total = total + ext_ref[s, POOL_HALO - k:POOL_HALO - k + tm, :]
            mixes.append(total / cnt - h[:, lanes])
        cols = slice(g * gc, (g + 1) * gc)
        mix = jnp.concatenate(mixes, axis=1).astype(BF16)
        y = _dot(mix, w_ref[g]) * scale_ref[:, cols]
        o_ref[:, cols] = x[:, cols] + y


def _pool(x, gain, w_grp, scale, *, tm):
    m, d = x.shape
    ng, gc, _ = w_grp.shape
    hb = tm // POOL_HALO
    kernel = functools.partial(_pool_kernel, tm=tm)
    return pl.pallas_call(
        kernel,
        grid=(m // tm,),
        in_specs=[
            pl.BlockSpec((tm, d), lambda i: (i, 0)),
            pl.BlockSpec((POOL_HALO, d), lambda i: (jnp.maximum(i * hb - 1, 0), 0)),
            pl.BlockSpec((1, d), lambda i: (0, 0)),
            pl.BlockSpec((ng, gc, gc), lambda i: (0, 0, 0)),
            pl.BlockSpec((1, d), lambda i: (0, 0)),
        ],
        out_specs=pl.BlockSpec((tm, d), lambda i: (i, 0)),
        out_shape=jax.ShapeDtypeStruct((m, d), F32),
        scratch_shapes=[pltpu.VMEM((d // V7X_LANES, tm + POOL_HALO, V7X_LANES), F32)],
        compiler_params=_params("parallel"),
        name="pool",
    )(x, x, gain, w_grp, scale)


def _norm_matmul_body(x_ref, g_ref, w_ref, o_ref):
    o_ref[...] = _dot(_rms_norm(x_ref[...], g_ref[...]).astype(BF16), w_ref[...])


def _norm_matmul(x, gain, w, *, tm, tn, casts):
    m, d = x.shape
    n = w.shape[0] * tn
    return _fused_call(
        _norm_matmul_body,
        name="qkv_proj",
        grid=(n // tn, m // tm),
        j_outer=True,
        in_specs=[
            pl.BlockSpec((tm, d), lambda j, i: (i, 0)),
            pl.BlockSpec((1, d), lambda j, i: (0, 0)),
            pl.BlockSpec((None, d, tn), lambda j, i: (j, 0, 0)),
        ],
        inputs=(x, gain, w),
        out_spec=pl.BlockSpec((tm, tn), lambda j, i: (i, j)),
        out_shape=jax.ShapeDtypeStruct((m, n), F32),
        casts=casts,
    )


def _t5_bucket(rel):
    nb = NUM_BUCKETS // 2
    n = -rel
    ret = jnp.where(n < 0, nb, 0)
    n = jnp.abs(n)
    max_exact = nb // 2
    nf = jnp.maximum(n, 1).astype(F32)
    large = max_exact + (jnp.log(nf / max_exact) / math.log(REL_MAX_DIST / max_exact)
                         * (nb - max_exact)).astype(jnp.int32)
    large = jnp.minimum(large, nb - 1)
    return ret + jnp.where(n < max_exact, n, large)


def _band_bias_body(bucket_ref, rel_bias_ref, o_ref):
    heads = o_ref.shape[0]
    bucket = bucket_ref[...]
    for k in range(heads):
        h = pl.program_id(0) * heads + k
        acc = jnp.zeros(bucket.shape, F32)
        for b in range(NUM_BUCKETS):
            acc = jnp.where(bucket == b, rel_bias_ref[b, h], acc)
        o_ref[k] = acc


def _band_bias(rel_bias, *, casts):
    n_heads = rel_bias.shape[1]
    heads = _tile(n_heads, BIAS_HEADS, 1)
    i = jnp.arange(QBLOCK)[:, None]
    j = jnp.arange(2 * QBLOCK)[None, :]
    bucket = _t5_bucket(j - QBLOCK - i).astype(jnp.int32).T
    return _fused_call(
        _band_bias_body,
        name="band_bias",
        grid=(n_heads // heads, 1),
        j_outer=False,
        in_specs=[
            pl.BlockSpec((2 * QBLOCK, QBLOCK), lambda h, j: (0, 0)),
            pl.BlockSpec(memory_space=pltpu.SMEM),
        ],
        inputs=(bucket, rel_bias),
        out_spec=pl.BlockSpec((heads, 2 * QBLOCK, QBLOCK), lambda h, j: (h, 0, 0)),
        out_shape=jax.ShapeDtypeStruct((n_heads, 2 * QBLOCK, QBLOCK), F32),
        casts=casts,
    )


def _half_rms_norm(x, g, low):
    sq = x * x
    ms_lo = jnp.sum(jnp.where(low, sq, 0.0), axis=-1, keepdims=True) * (1.0 / HEAD_DIM)
    ms_hi = jnp.sum(jnp.where(low, 0.0, sq), axis=-1, keepdims=True) * (1.0 / HEAD_DIM)
    r = jnp.where(low, lax.rsqrt(ms_lo + EPS), lax.rsqrt(ms_hi + EPS))
    return x * r * g


def _attn_kernel(sinks_ref, q_ref, kc_ref, kp_ref, vc_ref, vp_ref, bias_ref, qg_ref, kg_ref, o_ref,
                 *, n_kv, group):
    n = pl.program_id(0)
    pair = 2 * HEAD_DIM
    n_heads = n_kv * group
    kj = lax.broadcasted_iota(jnp.int32, (2 * QBLOCK, QBLOCK), 0)
    qi = lax.broadcasted_iota(jnp.int32, (2 * QBLOCK, QBLOCK), 1)
    qc = qi // CHUNK
    kc = kj // CHUNK - QBLOCK // CHUNK
    visible = (kc <= qc) & (kc >= qc - WINDOW_CHUNKS) & ((n > 0) | (kj >= QBLOCK))
    low = lax.broadcasted_iota(jnp.int32, (1, pair), 1) < HEAD_DIM
    scale = HEAD_DIM ** -0.5

    q_heads = []
    zeros = jnp.zeros((HEAD_DIM, QBLOCK), BF16)
    for c in range(n_heads // 2):
        qt = q_ref[:, c * pair:(c + 1) * pair].T
        for qhalf in range(2):
            qh = qt[qhalf * HEAD_DIM:(qhalf + 1) * HEAD_DIM]
            ms = jnp.mean(qh * qh, axis=0, keepdims=True)
            q_heads.append((qh * lax.rsqrt(ms + EPS) * qg_ref[...] * scale).astype(BF16))
    k_all = jnp.concatenate([ref[:, c * pair:(c + 1) * pair]
                             for c in range(n_kv // 2) for ref in (kp_ref, kc_ref)], axis=0)
    k_all = _half_rms_norm(k_all, kg_ref[...], low).astype(BF16)

    logits, values = [], []
    heads_per_tile = 2 * group
    for hp in range(n_kv // 2):
        kcols = slice(hp * pair, (hp + 1) * pair)
        values.append(jnp.concatenate([vp_ref[:, kcols].T, vc_ref[:, kcols].T], axis=1).astype(BF16))
        stacked = []
        for head in range(hp * heads_per_tile, (hp + 1) * heads_per_tile):
            rows = [q_heads[head], zeros] if (head // group) % 2 == 0 else [zeros, q_heads[head]]
            stacked.append(jnp.concatenate(rows, axis=0))
        k2 = k_all[hp * 2 * QBLOCK:(hp + 1) * 2 * QBLOCK]
        st = _dot(k2, jnp.concatenate(stacked, axis=1))
        for g in range(heads_per_tile):
            cols = st[:, g * QBLOCK:(g + 1) * QBLOCK]
            logits.append(jnp.where(visible, cols + bias_ref[hp * heads_per_tile + g], NEG_INF))

    s = jnp.concatenate(logits, axis=1)
    sink = jnp.concatenate([jnp.full((1, QBLOCK), sinks_ref[h], F32) for h in range(n_heads)], axis=1)
    mx = jnp.maximum(jnp.max(s, axis=0, keepdims=True), sink)
    e = jnp.exp(s - mx)
    denom = jnp.sum(e, axis=0, keepdims=True) + jnp.exp(sink - mx)
    r = 1.0 / denom
    e = e.astype(BF16)

    for hp in range(n_kv // 2):
        cols = slice(hp * heads_per_tile * QBLOCK, (hp + 1) * heads_per_tile * QBLOCK)
        ot = _dot(values[hp], e[:, cols]) * r[:, cols]
        for c in range(group):
            parts = []
            for qhalf in range(2):
                g = 2 * c + qhalf
                half = (g // group) % 2
                parts.append(ot[half * HEAD_DIM:(half + 1) * HEAD_DIM, g * QBLOCK:(g + 1) * QBLOCK])
            col = (hp * group + c) * pair
            o_ref[:, col:col + pair] = jnp.concatenate(parts, axis=0).T.astype(BF16)


def _attention(qkv, bias, sinks, q_gain, k_gain, *, n_heads, n_kv, casts):
    m = qkv.shape[0]
    group = n_heads // n_kv
    dq = n_heads * HEAD_DIM
    dkv = n_kv * HEAD_DIM
    kblk = dq // dkv
    prev = lambda n: jnp.maximum(n - 1, 0)
    return _fused_call(
        functools.partial(_attn_kernel, n_kv=n_kv, group=group),
        name="band_attention",
        grid=(m // QBLOCK, 1),
        j_outer=False,
        in_specs=[
            pl.BlockSpec(memory_space=pltpu.SMEM),
            pl.BlockSpec((QBLOCK, dq), lambda n, j: (n, 0)),
            pl.BlockSpec((QBLOCK, dkv), lambda n, j: (n, kblk)),
            pl.BlockSpec((QBLOCK, dkv), lambda n, j: (prev(n), kblk)),
            pl.BlockSpec((QBLOCK, dkv), lambda n, j: (n, kblk + 1)),
            pl.BlockSpec((QBLOCK, dkv), lambda n, j: (prev(n), kblk + 1)),
            pl.BlockSpec((n_heads, 2 * QBLOCK, QBLOCK), lambda n, j: (0, 0, 0)),
            pl.BlockSpec((HEAD_DIM, QBLOCK), lambda n, j: (0, 0)),
            pl.BlockSpec((1, 2 * HEAD_DIM), lambda n, j: (0, 0)),
        ],
        inputs=(sinks, qkv, qkv, qkv, qkv, qkv, bias, q_gain, k_gain),
        out_spec=pl.BlockSpec((QBLOCK, dq), lambda n, j: (n, 0)),
        out_shape=jax.ShapeDtypeStruct((m, dq), BF16),
        casts=casts,
    )


def _matmul_residual_body(a_ref, w_ref, x_ref, o_ref):
    o_ref[...] = x_ref[...] + _dot(a_ref[...], w_ref[...])


def _matmul_residual(a, w, x, *, tm, tn, casts):
    m, k = a.shape
    n = w.shape[0] * tn
    return _fused_call(
        _matmul_residual_body,
        name="attn_out_proj",
        grid=(n // tn, m // tm),
        j_outer=True,
        in_specs=[
            pl.BlockSpec((tm, k), lambda j, i: (i, 0)),
            pl.BlockSpec((None, k, tn), lambda j, i: (j, 0, 0)),
            pl.BlockSpec((tm, tn), lambda j, i: (i, j)),
        ],
        inputs=(a, w, x),
        out_spec=pl.BlockSpec((tm, tn), lambda j, i: (i, j)),
        out_shape=jax.ShapeDtypeStruct((m, n), F32),
        casts=casts,
    )


def _ffn_body(x_ref, g_ref, wg_ref, wu_ref, wd_ref, p_ref, pg_ref, wpg_ref, bpg_ref, wpp_ref, o_ref, h_ref):
    j = pl.program_id(1)

    @pl.when(j == 0)
    def _():
        x = x_ref[...]
        h_ref[...] = _rms_norm(x, g_ref[...]).astype(BF16)
        o_ref[...] = x

    h = h_ref[...]
    gate = _dot(h, wg_ref[...])
    up = _dot(h, wu_ref[...])
    act = (gate * jax.nn.sigmoid(gate) * up).astype(BF16)
    o_ref[...] += _dot(act, wd_ref[...])

    @pl.when(j == pl.num_programs(1) - 1)
    def _():
        x1 = o_ref[...]
        h1 = _rms_norm(x1, pg_ref[...]).astype(BF16)
        pb = p_ref[...].astype(BF16)
        tn = wpg_ref.shape[-1]
        for t in range(wpg_ref.shape[0]):
            cols = slice(t * tn, (t + 1) * tn)
            pgate = jax.nn.sigmoid(_dot(h1, wpg_ref[t]) + bpg_ref[:, cols])
            o_ref[:, cols] = x1[:, cols] + pgate * _dot(pb, wpp_ref[t])


def _ffn(x, gain, w_gate, w_up, w_down, p, layer, ple_gain, w_pgate, b_pgate, w_pproj, *, tm, tf, casts):
    m, d = x.shape
    f = w_gate.shape[0] * tf
    pd = p.shape[-1]
    resident = lambda a: pl.BlockSpec(a.shape, lambda i, j: (0,) * a.ndim, pipeline_mode=pl.Buffered(1))
    return _fused_call(
        _ffn_body,
        name="swiglu_ffn_gated_embedding",
        grid=(m // tm, f // tf),
        j_outer=False,
        in_specs=[
            pl.BlockSpec((tm, d), lambda i, j: (i, 0)),
            pl.BlockSpec((1, d), lambda i, j: (0, 0)),
            pl.BlockSpec((None, d, tf), lambda i, j: (j, 0, 0)),
            pl.BlockSpec((None, d, tf), lambda i, j: (j, 0, 0)),
            pl.BlockSpec((None, tf, d), lambda i, j: (0, j, 0)),
            pl.BlockSpec((None, tm, pd), lambda i, j: (layer, i, 0)),
            pl.BlockSpec((1, d), lambda i, j: (0, 0)),
            resident(w_pgate),
            pl.BlockSpec((1, d), lambda i, j: (0, 0)),
            resident(w_pproj),
        ],
        inputs=(x, gain, w_gate, w_up, w_down, p, ple_gain, w_pgate, b_pgate, w_pproj),
        out_spec=pl.BlockSpec((tm, d), lambda i, j: (i, 0)),
        out_shape=jax.ShapeDtypeStruct((m, d), F32),
        scratch=[pltpu.VMEM((tm, d), BF16)],
        casts=casts,
    )


def kernel(x, p, norm_mix, norm_ffn, norm_ple, conv_w_in, conv_b_in, conv_w_dw, conv_b_dw, conv_ln_g,
           conv_ln_b, conv_w_out, conv_b_out, pool_w, pool_scale, attn_w_qkv, attn_q_norm, attn_k_norm,
           attn_sinks, attn_w_o, rel_bias, ffn_w_gate, ffn_w_up, ffn_w_down, ple_w_proj, ple_w_gate,
           ple_b_gate):
    batch, seq, d = x.shape
    depth = norm_mix.shape[0]
    n_heads = d // HEAD_DIM
    n_kv = (attn_w_qkv.shape[-1] // HEAD_DIM - n_heads) // 2
    group = n_heads // n_kv
    assert n_kv % 2 == 0 and group % 2 == 0, "attention kernel pairs heads into 128-lane tiles"
    assert d % (len(POOL_WINDOWS) * V7X_LANES) == 0
    m = batch * seq
    assert batch == 1 and m % QBLOCK == 0, "row tiles assume one sequence"

    tm = _tile(m, ROW_TILE)
    tm2 = _tile(m, WIDE_ROW_TILE)
    tn = _tile(d, COL_TILE)
    tf = _tile(ffn_w_gate.shape[-1], FF_TILE)
    tq = _tile(attn_w_qkv.shape[-1], COL_TILE)
    row = lambda v: v.reshape(1, -1)

    pool_w2 = pool_w.reshape(pool_w.shape[0], -1, pool_w.shape[-1])

    def mixer_weights(i):
        kind, l = i % N_MIXERS, i // N_MIXERS
        if kind == 0:
            return {("conv_in", i): (conv_w_in, l, tn), ("conv_out", i): (conv_w_out, l, tn)}
        if kind == 1:
            return {("pool", i): (pool_w2, l, pool_w2.shape[-1])}
        return {("qkv", i): (attn_w_qkv, l, tq), ("attn_o", i): (attn_w_o, l, tn)}

    def ffn_weights(i):
        return {("gate", i): (ffn_w_gate, i, tf), ("up", i): (ffn_w_up, i, tf), ("down", i): (ffn_w_down, i, d)}

    def ple_weights(i):
        return {("ple_gate", i): (ple_w_gate, i, tn), ("ple_proj", i): (ple_w_proj, i, tn)}

    ready = {}

    def bf16(key, source):
        return ready.pop(key) if key in ready else _cast(*source)

    def hosting(weights, call):
        out, copies = call(list(weights.values()))
        ready.update(zip(weights.keys(), copies))
        return out

    xs = x.reshape(m, d)
    bias = None
    if depth >= N_MIXERS:
        bias = hosting(mixer_weights(0), lambda c: _band_bias(rel_bias, casts=c))
    for i in range(depth):
        kind, l = i % N_MIXERS, i // N_MIXERS
        gain = row(norm_mix[i])
        mine = {} if ("gate", i) in ready else {**ffn_weights(i), **ple_weights(i)}
        if kind == 0:
            w_in, w_out = (bf16(k, w) for k, w in mixer_weights(i).items())
            u = hosting(mine, lambda c: _conv_in(xs, gain, w_in, row(conv_b_in[l]), tm=tm, tn=tn, casts=c))
            xs = _conv_out(u, xs, conv_w_dw[l], row(conv_b_dw[l]), row(conv_ln_g[l]), row(conv_ln_b[l]),
                           w_out, row(conv_b_out[l]), tm=tm)
        elif kind == 1:
            (w_grp,) = (bf16(k, w) for k, w in mixer_weights(i).items())
            w_grp = w_grp.reshape(pool_w.shape[1:])
            xs = _pool(xs, gain, w_grp, row(pool_scale[l]), tm=tm)
        else:
            w_qkv, w_o = (bf16(k, w) for k, w in mixer_weights(i).items())
            qkv, _ = _norm_matmul(xs, gain, w_qkv, tm=tm2, tn=tq, casts=[])
            o = hosting({**mine, **(mixer_weights(i + 1) if i + 1 < depth else {})}, lambda c: _attention(
                qkv, bias, attn_sinks[l], jnp.broadcast_to(attn_q_norm[l][:, None], (HEAD_DIM, QBLOCK)),
                row(jnp.tile(attn_k_norm[l], 2)),
                n_heads=n_heads, n_kv=n_kv, casts=c))
            xs, _ = _matmul_residual(o, w_o, xs, tm=tm2, tn=tn, casts=[])
        w_gate, w_up, w_down = (bf16(k, w) for k, w in ffn_weights(i).items())
        w_pg, w_pp = (bf16(k, w) for k, w in ple_weights(i).items())
        ahead = {}
        if i + 1 < depth:
            ahead = mixer_weights(i + 1)
            if (i + 1) % N_MIXERS == 1:
                ahead = {**ahead, **ffn_weights(i + 1), **ple_weights(i + 1)}
            ahead = {k: v for k, v in ahead.items() if k not in ready}
        xs = hosting(ahead, lambda c: _ffn(
            xs, row(norm_ffn[i]), w_gate, w_up, w_down, p.reshape(depth, m, -1), i, row(norm_ple[i]),
            w_pg, row(ple_b_gate[i]), w_pp, tm=tm, tf=tf, casts=c))
    return xs.reshape(batch, seq, d)
```

```python
import functools
import math

import jax
import jax.numpy as jnp
from jax import lax
from jax.experimental import pallas as pl
from jax.experimental.pallas import tpu as pltpu

N_MIXERS = 3
CHUNK = 64
CONV_WIDTH = 31
POOL_WINDOWS = (2, 4, 8, 16)
HEAD_DIM = 64
WINDOW_CHUNKS = 2
QBLOCK = 128
NUM_BUCKETS = 32
REL_MAX_DIST = 128
EPS = 1e-6
NEG_INF = -1e30

V7X_VMEM_BYTES = 64 * 1024 * 1024
V7X_LANES = 128
V7X_SUBLANES = 8
BF16_ROWS = 2 * V7X_SUBLANES
VMEM_LIMIT_BYTES = V7X_VMEM_BYTES * 7 // 8

CONV_HALO = 32
POOL_HALO = 16
CONV_STRIP = V7X_LANES

ROW_TILE = 512
WIDE_ROW_TILE = 1024
COL_TILE = 1024
FF_TILE = 512
CAST_ROWS = 256
BIAS_HEADS = 4

BF16 = jnp.bfloat16
F32 = jnp.float32


def _params(*semantics):
    return pltpu.CompilerParams(dimension_semantics=semantics, vmem_limit_bytes=VMEM_LIMIT_BYTES)


def _rms_norm(x, g):
    ms = jnp.mean(x * x, axis=-1, keepdims=True)
    return x * lax.rsqrt(ms + EPS) * g


def _dot(a, b):
    return jnp.dot(a, b, preferred_element_type=F32)


def _tile(n, target, multiple=V7X_LANES):
    if n <= target:
        return n
    t = target - target % multiple
    while n % t:
        t -= multiple
    return t


def _copy_tiles(src_ref, dst_ref):
    tile = dst_ref.shape[-1]
    for t in range(dst_ref.shape[0]):
        dst_ref[t] = src_ref[:, t * tile:(t + 1) * tile].astype(BF16)


def _cast(stack, layer, tile):
    _, r, c = stack.shape
    rb = _tile(r, CAST_ROWS, BF16_ROWS)
    return pl.pallas_call(
        _copy_tiles,
        grid=(r // rb,),
        in_specs=[pl.BlockSpec((None, rb, c), lambda i: (layer, i, 0))],
        out_specs=pl.BlockSpec((c // tile, rb, tile), lambda i: (0, i, 0)),
        out_shape=jax.ShapeDtypeStruct((c // tile, r, tile), BF16),
        compiler_params=_params("parallel"),
        name="weight_cast",
    )(stack)


def _cast_plan(sources, gi, gj, j_outer):
    in_specs, out_specs, out_shapes, row_only = [], [], [], []
    for stack, layer, tile in sources:
        _, r, c = stack.shape
        once = False
        if r % (gi * gj * BF16_ROWS) == 0:
            rows, cols = r // (gi * gj), c
            src_map = lambda i, j: (i * gj + j, 0)
            dst_map = lambda i, j: (0, i * gj + j, 0)
        elif r % (gi * BF16_ROWS) == 0 and c % gj == 0 and (c // gj) % tile == 0:
            rows, cols = r // gi, c // gj
            src_map = lambda i, j: (i, j)
            dst_map = lambda i, j: (j, i, 0)
        elif r % (gj * BF16_ROWS) == 0 and c % gi == 0 and (c // gi) % tile == 0:
            rows, cols = r // gj, c // gi
            src_map = lambda i, j: (j, i)
            dst_map = lambda i, j: (i, j, 0)
        elif r % (gi * BF16_ROWS) == 0 and not j_outer:
            rows, cols = r // gi, c
            src_map = lambda i, j: (i, 0)
            dst_map = lambda i, j: (0, i, 0)
            once = gj > 1
        else:
            n = max(k for k in range(1, gi * gj + 1) if r % (k * BF16_ROWS) == 0)
            step = (lambda i, j: j * gi + i) if j_outer else (lambda i, j: i * gj + j)
            rows, cols = r // n, c
            src_map = lambda i, j, n=n, step=step: (jnp.minimum(step(i, j), n - 1), 0)
            dst_map = lambda i, j, n=n, step=step: (0, jnp.minimum(step(i, j), n - 1), 0)
        if j_outer:
            src_map = (lambda f: lambda j, i: f(i, j))(src_map)
            dst_map = (lambda f: lambda j, i: f(i, j))(dst_map)
        src_map = (lambda f, l: lambda a, b: (l, *f(a, b)))(src_map, layer)
        in_specs.append(pl.BlockSpec((None, rows, cols), src_map))
        out_specs.append(pl.BlockSpec((cols // tile, rows, tile), dst_map))
        out_shapes.append(jax.ShapeDtypeStruct((c // tile, r, tile), BF16))
        row_only.append(once)
    return in_specs, out_specs, out_shapes, row_only


def _fused_call(body, *, name, grid, j_outer, in_specs, inputs, out_spec, out_shape, scratch=(), casts=()):
    n_in, n_cast = len(inputs), len(casts)
    gi, gj = (grid[1], grid[0]) if j_outer else grid
    cast_in, cast_out, cast_shapes, row_only = _cast_plan(casts, gi, gj, j_outer)

    def kernel(*refs):
        ins, rest = refs[:n_in], refs[n_in:]
        cast_src, rest = rest[:n_cast], rest[n_cast:]
        out, rest = rest[0], rest[1:]
        cast_dst, scr = rest[:n_cast], rest[n_cast:]
        for src, dst, once in zip(cast_src, cast_dst, row_only):
            if once:
                pl.when(pl.program_id(1) == 0)(functools.partial(_copy_tiles, src, dst))
            else:
                _copy_tiles(src, dst)
        body(*ins, out, *scr)

    res = pl.pallas_call(
        kernel,
        grid=grid,
        in_specs=[*in_specs, *cast_in],
        out_specs=[out_spec, *cast_out],
        out_shape=[out_shape, *cast_shapes],
        scratch_shapes=list(scratch),
        compiler_params=_params("parallel", "parallel" if j_outer else "arbitrary"),
        name=name,
    )(*inputs, *[stack for stack, _, _ in casts])
    return res[0], list(res[1:])


def _conv_in_body(x_ref, g_ref, wa_ref, wg_ref, ba_ref, bg_ref, u_ref):
    h = _rms_norm(x_ref[...], g_ref[...]).astype(BF16)
    a = _dot(h, wa_ref[...]) + ba_ref[...]
    gate = _dot(h, wg_ref[...]) + bg_ref[...]
    u_ref[...] = a * jax.nn.sigmoid(gate)


def _conv_in(x, gain, w_in, b_in, *, tm, tn, casts):
    m, d = x.shape
    nj = d // tn
    return _fused_call(
        _conv_in_body,
        name="conv_in",
        grid=(nj, m // tm),
        j_outer=True,
        in_specs=[
            pl.BlockSpec((tm, d), lambda j, i: (i, 0)),
            pl.BlockSpec((1, d), lambda j, i: (0, 0)),
            pl.BlockSpec((None, d, tn), lambda j, i: (j, 0, 0)),
            pl.BlockSpec((None, d, tn), lambda j, i: (j + nj, 0, 0)),
            pl.BlockSpec((1, tn), lambda j, i: (0, j)),
            pl.BlockSpec((1, tn), lambda j, i: (0, j + nj)),
        ],
        inputs=(x, gain, w_in, w_in, b_in, b_in),
        out_spec=pl.BlockSpec((tm, tn), lambda j, i: (i, j)),
        out_shape=jax.ShapeDtypeStruct((m, d), F32),
        casts=casts,
    )


def _conv_out_kernel(u_ref, uprev_ref, wdw_ref, bdw_ref, lng_ref, lnb_ref, wout_ref, bout_ref, x_ref,
                     o_ref, ext_ref, conv_ref, *, tm):
    i = pl.program_id(0)
    d = u_ref.shape[1]
    first = CONV_HALO - (CONV_WIDTH - 1)

    for s in range(d // CONV_STRIP):
        cols = slice(s * CONV_STRIP, (s + 1) * CONV_STRIP)
        ext_ref[s, 0:CONV_HALO, :] = jnp.where(i == 0, 0.0, uprev_ref[:, cols])
        ext_ref[s, CONV_HALO:, :] = u_ref[:, cols]
        acc = jnp.broadcast_to(bdw_ref[:, cols], (tm, CONV_STRIP))
        for k in range(CONV_WIDTH):
            acc = acc + wdw_ref[k:k + 1, cols] * ext_ref[s, first + k:first + k + tm, :]
        conv_ref[:, cols] = acc

    conv = conv_ref[...]
    mu = jnp.mean(conv, axis=-1, keepdims=True)
    xc = conv - mu
    y = xc * lax.rsqrt(jnp.mean(xc * xc, axis=-1, keepdims=True) + EPS)
    y = y * lng_ref[...] + lnb_ref[...]
    v = (y * jax.nn.sigmoid(y)).astype(BF16)
    tn = wout_ref.shape[-1]
    for t in range(wout_ref.shape[0]):
        cols = slice(t * tn, (t + 1) * tn)
        o_ref[:, cols] = x_ref[:, cols] + _dot(v, wout_ref[t]) + bout_ref[:, cols]


def _conv_out(u, x, w_dw, b_dw, ln_g, ln_b, w_out, b_out, *, tm):
    m, d = u.shape
    hb = tm // CONV_HALO
    const = lambda i: (0, 0)
    return pl.pallas_call(
        functools.partial(_conv_out_kernel, tm=tm),
        grid=(m // tm,),
        in_specs=[
            pl.BlockSpec((tm, d), lambda i: (i, 0)),
            pl.BlockSpec((CONV_HALO, d), lambda i: (jnp.maximum(i * hb - 1, 0), 0)),
            pl.BlockSpec((CONV_WIDTH, d), const),
            pl.BlockSpec((1, d), const),
            pl.BlockSpec((1, d), const),
            pl.BlockSpec((1, d), const),
            pl.BlockSpec(w_out.shape, lambda i: (0, 0, 0), pipeline_mode=pl.Buffered(1)),
            pl.BlockSpec((1, d), const),
            pl.BlockSpec((tm, d), lambda i: (i, 0)),
        ],
        out_specs=pl.BlockSpec((tm, d), lambda i: (i, 0)),
        out_shape=jax.ShapeDtypeStruct((m, d), F32),
        scratch_shapes=[
            pltpu.VMEM((d // CONV_STRIP, tm + CONV_HALO, CONV_STRIP), F32),
            pltpu.VMEM((tm, d), F32),
        ],
        compiler_params=_params("parallel"),
        name="conv_out",
    )(u, u, w_dw, b_dw, ln_g, ln_b, w_out, b_out, x)


def _pool_kernel(x_ref, xprev_ref, g_ref, w_ref, scale_ref, o_ref, ext_ref, *, tm):
    i = pl.program_id(0)
    d = x_ref.shape[1]
    gc = d // len(POOL_WINDOWS)
    x = x_ref[...]
    h = _rms_norm(x, g_ref[...])
    hprev = jnp.where(i == 0, 0.0, _rms_norm(xprev_ref[...], g_ref[...]))
    t = i * tm + lax.broadcasted_iota(jnp.int32, (tm, 1), 0)
    strips = gc // V7X_LANES
    for g, w in enumerate(POOL_WINDOWS):
        cnt = jnp.minimum(t + 1, w).astype(F32)
        mixes = []
        for s in range(g * strips, (g + 1) * strips):
            lanes = slice(s * V7X_LANES, (s + 1) * V7X_LANES)
            ext_ref[s, 0:POOL_HALO, :] = hprev[:, lanes]
            ext_ref[s, POOL_HALO:, :] = h[:, lanes]
            total = h[:, lanes]
            for k in range(1, w):
                total = total + ext_ref[s, POOL_HALO - k:POOL_HALO - k + tm, :]
            mixes.append(total / cnt - h[:, lanes])
        cols = slice(g * gc, (g + 1) * gc)
        mix = jnp.concatenate(mixes, axis=1).astype(BF16)
        y = _dot(mix, w_ref[g]) * scale_ref[:, cols]
        o_ref[:, cols] = x[:, cols] + y


def _pool(x, gain, w_grp, scale, *, tm):
    m, d = x.shape
    ng, gc, _ = w_grp.shape
    hb = tm // POOL_HALO
    kernel = functools.partial(_pool_kernel, tm=tm)
    return pl.pallas_call(
        kernel,
        grid=(m // tm,),
        in_specs=[
            pl.BlockSpec((tm, d), lambda i: (i, 0)),
            pl.BlockSpec((POOL_HALO, d), lambda i: (jnp.maximum(i * hb - 1, 0), 0)),
            pl.BlockSpec((1, d), lambda i: (0, 0)),
            pl.BlockSpec((ng, gc, gc), lambda i: (0, 0, 0)),
            pl.BlockSpec((1, d), lambda i: (0, 0)),
        ],
        out_specs=pl.BlockSpec((tm, d), lambda i: (i, 0)),
        out_shape=jax.ShapeDtypeStruct((m, d), F32),
        scratch_shapes=[pltpu.VMEM((d // V7X_LANES, tm + POOL_HALO, V7X_LANES), F32)],
        compiler_params=_params("parallel"),
        name="pool",
    )(x, x, gain, w_grp, scale)


def _norm_matmul_body(x_ref, g_ref, w_ref, o_ref, h_ref):
    @pl.when(pl.program_id(1) == 0)
    def _():
        h_ref[...] = _rms_norm(x_ref[...], g_ref[...]).astype(BF16)

    o_ref[...] = _dot(h_ref[...], w_ref[...])


def _norm_matmul(x, gain, w, *, tm, tn, casts):
    m, d = x.shape
    n = w.shape[0] * tn
    return _fused_call(
        _norm_matmul_body,
        name="qkv_proj",
        grid=(m // tm, n // tn),
        j_outer=False,
        in_specs=[
            pl.BlockSpec((tm, d), lambda i, j: (i, 0)),
            pl.BlockSpec((1, d), lambda i, j: (0, 0)),
            pl.BlockSpec((None, d, tn), lambda i, j: (j, 0, 0)),
        ],
        inputs=(x, gain, w),
        out_spec=pl.BlockSpec((tm, tn), lambda i, j: (i, j)),
        out_shape=jax.ShapeDtypeStruct((m, n), F32),
        scratch=[pltpu.VMEM((tm, d), BF16)],
        casts=casts,
    )


def _t5_bucket(rel):
    nb = NUM_BUCKETS // 2
    n = -rel
    ret = jnp.where(n < 0, nb, 0)
    n = jnp.abs(n)
    max_exact = nb // 2
    nf = jnp.maximum(n, 1).astype(F32)
    large = max_exact + (jnp.log(nf / max_exact) / math.log(REL_MAX_DIST / max_exact)
                         * (nb - max_exact)).astype(jnp.int32)
    large = jnp.minimum(large, nb - 1)
    return ret + jnp.where(n < max_exact, n, large)


def _band_bias_body(bucket_ref, rel_bias_ref, o_ref):
    heads = o_ref.shape[0]
    bucket = bucket_ref[...]
    for k in range(heads):
        h = pl.program_id(0) * heads + k
        acc = jnp.zeros(bucket.shape, F32)
        for b in range(NUM_BUCKETS):
            acc = jnp.where(bucket == b, rel_bias_ref[b, h], acc)
        o_ref[k] = acc


def _band_bias(rel_bias, *, casts):
    n_heads = rel_bias.shape[1]
    heads = _tile(n_heads, BIAS_HEADS, 1)
    i = jnp.arange(QBLOCK)[:, None]
    j = jnp.arange(2 * QBLOCK)[None, :]
    bucket = _t5_bucket(j - QBLOCK - i).astype(jnp.int32).T
    return _fused_call(
        _band_bias_body,
        name="band_bias",
        grid=(n_heads // heads, 1),
        j_outer=False,
        in_specs=[
            pl.BlockSpec((2 * QBLOCK, QBLOCK), lambda h, j: (0, 0)),
            pl.BlockSpec(memory_space=pltpu.SMEM),
        ],
        inputs=(bucket, rel_bias),
        out_spec=pl.BlockSpec((heads, 2 * QBLOCK, QBLOCK), lambda h, j: (h, 0, 0)),
        out_shape=jax.ShapeDtypeStruct((n_heads, 2 * QBLOCK, QBLOCK), F32),
        casts=casts,
    )


def _half_rms_norm(x, g, low):
    sq = x * x
    ms_lo = jnp.sum(jnp.where(low, sq, 0.0), axis=-1, keepdims=True) * (1.0 / HEAD_DIM)
    ms_hi = jnp.sum(jnp.where(low, 0.0, sq), axis=-1, keepdims=True) * (1.0 / HEAD_DIM)
    r = jnp.where(low, lax.rsqrt(ms_lo + EPS), lax.rsqrt(ms_hi + EPS))
    return x * r * g


def _attn_kernel(sinks_ref, q_ref, kc_ref, kp_ref, vc_ref, vp_ref, bias_ref, qg_ref, kg_ref, o_ref,
                 *, n_kv, group):
    n = pl.program_id(0)
    pair = 2 * HEAD_DIM
    n_heads = n_kv * group
    kj = lax.broadcasted_iota(jnp.int32, (2 * QBLOCK, QBLOCK), 0)
    qi = lax.broadcasted_iota(jnp.int32, (2 * QBLOCK, QBLOCK), 1)
    qc = qi // CHUNK
    kc = kj // CHUNK - QBLOCK // CHUNK
    visible = (kc <= qc) & (kc >= qc - WINDOW_CHUNKS) & ((n > 0) | (kj >= QBLOCK))
    low = lax.broadcasted_iota(jnp.int32, (1, pair), 1) < HEAD_DIM
    scale = HEAD_DIM ** -0.5

    q_heads = []
    zeros = jnp.zeros((HEAD_DIM, QBLOCK), BF16)
    for c in range(n_heads // 2):
        qt = q_ref[:, c * pair:(c + 1) * pair].T
        for qhalf in range(2):
            qh = qt[qhalf * HEAD_DIM:(qhalf + 1) * HEAD_DIM]
            ms = jnp.mean(qh * qh, axis=0, keepdims=True)
            q_heads.append((qh * lax.rsqrt(ms + EPS) * qg_ref[...] * scale).astype(BF16))
    k_all = jnp.concatenate([ref[:, c * pair:(c + 1) * pair]
                             for c in range(n_kv // 2) for ref in (kp_ref, kc_ref)], axis=0)
    k_all = _half_rms_norm(k_all, kg_ref[...], low).astype(BF16)

    logits, values = [], []
    heads_per_tile = 2 * group
    for hp in range(n_kv // 2):
        kcols = slice(hp * pair, (hp + 1) * pair)
        values.append(jnp.concatenate([vp_ref[:, kcols].T, vc_ref[:, kcols].T], axis=1).astype(BF16))
        stacked = []
        for head in range(hp * heads_per_tile, (hp + 1) * heads_per_tile):
            rows = [q_heads[head], zeros] if (head // group) % 2 == 0 else [zeros, q_heads[head]]
            stacked.append(jnp.concatenate(rows, axis=0))
        k2 = k_all[hp * 2 * QBLOCK:(hp + 1) * 2 * QBLOCK]
        st = _dot(k2, jnp.concatenate(stacked, axis=1))
        for g in range(heads_per_tile):
            cols = st[:, g * QBLOCK:(g + 1) * QBLOCK]
            logits.append(jnp.where(visible, cols + bias_ref[hp * heads_per_tile + g], NEG_INF))

    s = jnp.concatenate(logits, axis=1)
    sink = jnp.concatenate([jnp.full((1, QBLOCK), sinks_ref[h], F32) for h in range(n_heads)], axis=1)
    mx = jnp.maximum(jnp.max(s, axis=0, keepdims=True), sink)
    e = jnp.exp(s - mx)
    denom = jnp.sum(e, axis=0, keepdims=True) + jnp.exp(sink - mx)
    r = 1.0 / denom
    e = e.astype(BF16)

    for hp in range(n_kv // 2):
        cols = slice(hp * heads_per_tile * QBLOCK, (hp + 1) * heads_per_tile * QBLOCK)
        ot = _dot(values[hp], e[:, cols]) * r[:, cols]
        for c in range(group):
            parts = []
            for qhalf in range(2):
                g = 2 * c + qhalf
                half = (g // group) % 2
                parts.append(ot[half * HEAD_DIM:(half + 1) * HEAD_DIM, g * QBLOCK:(g + 1) * QBLOCK])
            col = (hp * group + c) * pair
            o_ref[:, col:col + pair] = jnp.concatenate(parts, axis=0).T.astype(BF16)


def _attention(qkv, bias, sinks, q_gain, k_gain, *, n_heads, n_kv, casts):
    m = qkv.shape[0]
    group = n_heads // n_kv
    dq = n_heads * HEAD_DIM
    dkv = n_kv * HEAD_DIM
    kblk = dq // dkv
    prev = lambda n: jnp.maximum(n - 1, 0)
    return _fused_call(
        functools.partial(_attn_kernel, n_kv=n_kv, group=group),
        name="band_attention",
        grid=(m // QBLOCK, 1),
        j_outer=False,
        in_specs=[
            pl.BlockSpec(memory_space=pltpu.SMEM),
            pl.BlockSpec((QBLOCK, dq), lambda n, j: (n, 0)),
            pl.BlockSpec((QBLOCK, dkv), lambda n, j: (n, kblk)),
            pl.BlockSpec((QBLOCK, dkv), lambda n, j: (prev(n), kblk)),
            pl.BlockSpec((QBLOCK, dkv), lambda n, j: (n, kblk + 1)),
            pl.BlockSpec((QBLOCK, dkv), lambda n, j: (prev(n), kblk + 1)),
            pl.BlockSpec((n_heads, 2 * QBLOCK, QBLOCK), lambda n, j: (0, 0, 0)),
            pl.BlockSpec((HEAD_DIM, QBLOCK), lambda n, j: (0, 0)),
            pl.BlockSpec((1, 2 * HEAD_DIM), lambda n, j: (0, 0)),
        ],
        inputs=(sinks, qkv, qkv, qkv, qkv, qkv, bias, q_gain, k_gain),
        out_spec=pl.BlockSpec((QBLOCK, dq), lambda n, j: (n, 0)),
        out_shape=jax.ShapeDtypeStruct((m, dq), BF16),
        casts=casts,
    )


def _matmul_residual_body(a_ref, w_ref, x_ref, o_ref):
    o_ref[...] = x_ref[...] + _dot(a_ref[...], w_ref[...])


def _matmul_residual(a, w, x, *, tm, tn, casts):
    m, k = a.shape
    n = w.shape[0] * tn
    return _fused_call(
        _matmul_residual_body,
        name="attn_out_proj",
        grid=(n // tn, m // tm),
        j_outer=True,
        in_specs=[
            pl.BlockSpec((tm, k), lambda j, i: (i, 0)),
            pl.BlockSpec((None, k, tn), lambda j, i: (j, 0, 0)),
            pl.BlockSpec((tm, tn), lambda j, i: (i, j)),
        ],
        inputs=(a, w, x),
        out_spec=pl.BlockSpec((tm, tn), lambda j, i: (i, j)),
        out_shape=jax.ShapeDtypeStruct((m, n), F32),
        casts=casts,
    )


def _ffn_body(x_ref, g_ref, wg_ref, wu_ref, wd_ref, p_ref, pg_ref, wpg_ref, bpg_ref, wpp_ref, o_ref, h_ref):
    j = pl.program_id(1)

    @pl.when(j == 0)
    def _():
        x = x_ref[...]
        h_ref[...] = _rms_norm(x, g_ref[...]).astype(BF16)
        o_ref[...] = x

    h = h_ref[...]
    gate = _dot(h, wg_ref[...])
    up = _dot(h, wu_ref[...])
    act = (gate * jax.nn.sigmoid(gate) * up).astype(BF16)
    o_ref[...] += _dot(act, wd_ref[...])

    @pl.when(j == pl.num_programs(1) - 1)
    def _():
        x1 = o_ref[...]
        h1 = _rms_norm(x1, pg_ref[...]).astype(BF16)
        pb = p_ref[...].astype(BF16)
        tn = wpg_ref.shape[-1]
        for t in range(wpg_ref.shape[0]):
            cols = slice(t * tn, (t + 1) * tn)
            pgate = jax.nn.sigmoid(_dot(h1, wpg_ref[t]) + bpg_ref[:, cols])
            o_ref[:, cols] = x1[:, cols] + pgate * _dot(pb, wpp_ref[t])


def _ffn(x, gain, w_gate, w_up, w_down, p, layer, ple_gain, w_pgate, b_pgate, w_pproj, *, tm, tf, casts):
    m, d = x.shape
    f = w_gate.shape[0] * tf
    pd = p.shape[-1]
    resident = lambda a: pl.BlockSpec(a.shape, lambda i, j: (0,) * a.ndim, pipeline_mode=pl.Buffered(1))
    return _fused_call(
        _ffn_body,
        name="swiglu_ffn_gated_embedding",
        grid=(m // tm, f // tf),
        j_outer=False,
        in_specs=[
            pl.BlockSpec((tm, d), lambda i, j: (i, 0)),
            pl.BlockSpec((1, d), lambda i, j: (0, 0)),
            pl.BlockSpec((None, d, tf), lambda i, j: (j, 0, 0)),
            pl.BlockSpec((None, d, tf), lambda i, j: (j, 0, 0)),
            pl.BlockSpec((None, tf, d), lambda i, j: (0, j, 0)),
            pl.BlockSpec((None, tm, pd), lambda i, j: (layer, i, 0)),
            pl.BlockSpec((1, d), lambda i, j: (0, 0)),
            resident(w_pgate),
            pl.BlockSpec((1, d), lambda i, j: (0, 0)),
            resident(w_pproj),
        ],
        inputs=(x, gain, w_gate, w_up, w_down, p, ple_gain, w_pgate, b_pgate, w_pproj),
        out_spec=pl.BlockSpec((tm, d), lambda i, j: (i, 0)),
        out_shape=jax.ShapeDtypeStruct((m, d), F32),
        scratch=[pltpu.VMEM((tm, d), BF16)],
        casts=casts,
    )


def kernel(x, p, norm_mix, norm_ffn, norm_ple, conv_w_in, conv_b_in, conv_w_dw, conv_b_dw, conv_ln_g,
           conv_ln_b, conv_w_out, conv_b_out, pool_w, pool_scale, attn_w_qkv, attn_q_norm, attn_k_norm,
           attn_sinks, attn_w_o, rel_bias, ffn_w_gate, ffn_w_up, ffn_w_down, ple_w_proj, ple_w_gate,
           ple_b_gate):
    batch, seq, d = x.shape
    depth = norm_mix.shape[0]
    n_heads = d // HEAD_DIM
    n_kv = (attn_w_qkv.shape[-1] // HEAD_DIM - n_heads) // 2
    group = n_heads // n_kv
    assert n_kv % 2 == 0 and group % 2 == 0, "attention kernel pairs heads into 128-lane tiles"
    assert d % (len(POOL_WINDOWS) * V7X_LANES) == 0
    m = batch * seq
    assert batch == 1 and m % QBLOCK == 0, "row tiles assume one sequence"

    tm = _tile(m, ROW_TILE)
    tm2 = _tile(m, WIDE_ROW_TILE)
    tn = _tile(d, COL_TILE)
    tf = _tile(ffn_w_gate.shape[-1], FF_TILE)
    tq = _tile(attn_w_qkv.shape[-1], COL_TILE)
    row = lambda v: v.reshape(1, -1)

    pool_w2 = pool_w.reshape(pool_w.shape[0], -1, pool_w.shape[-1])

    def mixer_weights(i):
        kind, l = i % N_MIXERS, i // N_MIXERS
        if kind == 0:
            return {("conv_in", i): (conv_w_in, l, tn), ("conv_out", i): (conv_w_out, l, tn)}
        if kind == 1:
            return {("pool", i): (pool_w2, l, pool_w2.shape[-1])}
        return {("qkv", i): (attn_w_qkv, l, tq), ("attn_o", i): (attn_w_o, l, tn)}

    def ffn_weights(i):
        return {("gate", i): (ffn_w_gate, i, tf), ("up", i): (ffn_w_up, i, tf), ("down", i): (ffn_w_down, i, d)}

    def ple_weights(i):
        return {("ple_gate", i): (ple_w_gate, i, tn), ("ple_proj", i): (ple_w_proj, i, tn)}

    ready = {}

    def bf16(key, source):
        return ready.pop(key) if key in ready else _cast(*source)

    def hosting(weights, call):
        out, copies = call(list(weights.values()))
        ready.update(zip(weights.keys(), copies))
        return out

    xs = x.reshape(m, d)
    bias = None
    if depth >= N_MIXERS:
        bias = hosting(mixer_weights(0), lambda c: _band_bias(rel_bias, casts=c))
    for i in range(depth):
        kind, l = i % N_MIXERS, i // N_MIXERS
        gain = row(norm_mix[i])
        mine = {} if ("gate", i) in ready else {**ffn_weights(i), **ple_weights(i)}
        if kind == 0:
            w_in, w_out = (bf16(k, w) for k, w in mixer_weights(i).items())
            u = hosting(mine, lambda c: _conv_in(xs, gain, w_in, row(conv_b_in[l]), tm=tm, tn=tn, casts=c))
            xs = _conv_out(u, xs, conv_w_dw[l], row(conv_b_dw[l]), row(conv_ln_g[l]), row(conv_ln_b[l]),
                           w_out, row(conv_b_out[l]), tm=tm)
        elif kind == 1:
            (w_grp,) = (bf16(k, w) for k, w in mixer_weights(i).items())
            w_grp = w_grp.reshape(pool_w.shape[1:])
            xs = _pool(xs, gain, w_grp, row(pool_scale[l]), tm=tm)
        else:
            w_qkv, w_o = (bf16(k, w) for k, w in mixer_weights(i).items())
            qkv, _ = _norm_matmul(xs, gain, w_qkv, tm=tm2, tn=tq, casts=[])
            o = hosting({**mine, **(mixer_weights(i + 1) if i + 1 < depth else {})}, lambda c: _attention(
                qkv, bias, attn_sinks[l], jnp.broadcast_to(attn_q_norm[l][:, None], (HEAD_DIM, QBLOCK)),
                row(jnp.tile(attn_k_norm[l], 2)),
                n_heads=n_heads, n_kv=n_kv, casts=c))
            xs, _ = _matmul_residual(o, w_o, xs, tm=tm2, tn=tn, casts=[])
        w_gate, w_up, w_down = (bf16(k, w) for k, w in ffn_weights(i).items())
        w_pg, w_pp = (bf16(k, w) for k, w in ple_weights(i).items())
        ahead = {}
        if i + 1 < depth:
            ahead = mixer_weights(i + 1)
            if (i + 1) % N_MIXERS == 1:
                ahead = {**ahead, **ffn_weights(i + 1), **ple_weights(i + 1)}
            ahead = {k: v for k, v in ahead.items() if k not in ready}
        xs = hosting(ahead, lambda c: _ffn(
            xs, row(norm_ffn[i]), w_gate, w_up, w_down, p.reshape(depth, m, -1), i, row(norm_ple[i]),
            w_pg, row(ple_b_gate[i]), w_pp, tm=tm, tf=tf, casts=c))
    return xs.reshape(batch, seq, d)
```

```python
import functools
import math

import jax
import jax.numpy as jnp
from jax import lax
from jax.experimental import pallas as pl
from jax.experimental.pallas import tpu as pltpu

N_MIXERS = 3
CHUNK = 64
CONV_WIDTH = 31
POOL_WINDOWS = (2, 4, 8, 16)
HEAD_DIM = 64
WINDOW_CHUNKS = 2
QBLOCK = 128
NUM_BUCKETS = 32
REL_MAX_DIST = 128
EPS = 1e-6
NEG_INF = -1e30

V7X_VMEM_BYTES = 64 * 1024 * 1024
V7X_LANES = 128
V7X_SUBLANES = 8
BF16_ROWS = 2 * V7X_SUBLANES
VMEM_LIMIT_BYTES = V7X_VMEM_BYTES * 7 // 8

CONV_HALO = 32
POOL_HALO = 16
CONV_STRIP = V7X_LANES

ROW_TILE = 512
WIDE_ROW_TILE = 1024
COL_TILE = 1024
FF_TILE = 512
CAST_ROWS = 256
BIAS_HEADS = 4

BF16 = jnp.bfloat16
F32 = jnp.float32


def _params(*semantics):
    return pltpu.CompilerParams(dimension_semantics=semantics, vmem_limit_bytes=VMEM_LIMIT_BYTES)


def _rms_norm(x, g):
    ms = jnp.mean(x * x, axis=-1, keepdims=True)
    return x * lax.rsqrt(ms + EPS) * g


def _dot(a, b):
    return jnp.dot(a, b, preferred_element_type=F32)


def _tile(n, target, multiple=V7X_LANES):
    if n <= target:
        return n
    t = target - target % multiple
    while n % t:
        t -= multiple
    return t


def _copy_tiles(src_ref, dst_ref):
    tile = dst_ref.shape[-1]
    for t in range(dst_ref.shape[0]):
        dst_ref[t] = src_ref[:, t * tile:(t + 1) * tile].astype(BF16)


def _cast(stack, layer, tile):
    _, r, c = stack.shape
    rb = _tile(r, CAST_ROWS, BF16_ROWS)
    return pl.pallas_call(
        _copy_tiles,
        grid=(r // rb,),
        in_specs=[pl.BlockSpec((None, rb, c), lambda i: (layer, i, 0))],
        out_specs=pl.BlockSpec((c // tile, rb, tile), lambda i: (0, i, 0)),
        out_shape=jax.ShapeDtypeStruct((c // tile, r, tile), BF16),
        compiler_params=_params("parallel"),
        name="weight_cast",
    )(stack)


def _cast_plan(sources, gi, gj, j_outer):
    in_specs, out_specs, out_shapes, row_only = [], [], [], []
    for stack, layer, tile in sources:
        _, r, c = stack.shape
        once = False
        if r % (gi * gj * BF16_ROWS) == 0:
            rows, cols = r // (gi * gj), c
            src_map = lambda i, j: (i * gj + j, 0)
            dst_map = lambda i, j: (0, i * gj + j, 0)
        elif r % (gi * BF16_ROWS) == 0 and c % gj == 0 and (c // gj) % tile == 0:
            rows, cols = r // gi, c // gj
            src_map = lambda i, j: (i, j)
            dst_map = lambda i, j: (j, i, 0)
        elif r % (gj * BF16_ROWS) == 0 and c % gi == 0 and (c // gi) % tile == 0:
            rows, cols = r // gj, c // gi
            src_map = lambda i, j: (j, i)
            dst_map = lambda i, j: (i, j, 0)
        elif r % (gi * BF16_ROWS) == 0 and not j_outer:
            rows, cols = r // gi, c
            src_map = lambda i, j: (i, 0)
            dst_map = lambda i, j: (0, i, 0)
            once = gj > 1
        else:
            n = max(k for k in range(1, gi * gj + 1) if r % (k * BF16_ROWS) == 0)
            step = (lambda i, j: j * gi + i) if j_outer else (lambda i, j: i * gj + j)
            rows, cols = r // n, c
            src_map = lambda i, j, n=n, step=step: (jnp.minimum(step(i, j), n - 1), 0)
            dst_map = lambda i, j, n=n, step=step: (0, jnp.minimum(step(i, j), n - 1), 0)
        if j_outer:
            src_map = (lambda f: lambda j, i: f(i, j))(src_map)
            dst_map = (lambda f: lambda j, i: f(i, j))(dst_map)
        src_map = (lambda f, l: lambda a, b: (l, *f(a, b)))(src_map, layer)
        in_specs.append(pl.BlockSpec((None, rows, cols), src_map))
        out_specs.append(pl.BlockSpec((cols // tile, rows, tile), dst_map))
        out_shapes.append(jax.ShapeDtypeStruct((c // tile, r, tile), BF16))
        row_only.append(once)
    return in_specs, out_specs, out_shapes, row_only


def _fused_call(body, *, name, grid, j_outer, in_specs, inputs, out_spec, out_shape, scratch=(), casts=()):
    n_in, n_cast = len(inputs), len(casts)
    gi, gj = (grid[1], grid[0]) if j_outer else grid
    cast_in, cast_out, cast_shapes, row_only = _cast_plan(casts, gi, gj, j_outer)

    def kernel(*refs):
        ins, rest = refs[:n_in], refs[n_in:]
        cast_src, rest = rest[:n_cast], rest[n_cast:]
        out, rest = rest[0], rest[1:]
        cast_dst, scr = rest[:n_cast], rest[n_cast:]
        for src, dst, once in zip(cast_src, cast_dst, row_only):
            if once:
                pl.when(pl.program_id(1) == 0)(functools.partial(_copy_tiles, src, dst))
            else:
                _copy_tiles(src, dst)
        body(*ins, out, *scr)

    res = pl.pallas_call(
        kernel,
        grid=grid,
        in_specs=[*in_specs, *cast_in],
        out_specs=[out_spec, *cast_out],
        out_shape=[out_shape, *cast_shapes],
        scratch_shapes=list(scratch),
        compiler_params=_params("parallel", "parallel" if j_outer else "arbitrary"),
        name=name,
    )(*inputs, *[stack for stack, _, _ in casts])
    return res[0], list(res[1:])


def _conv_in_body(x_ref, g_ref, wa_ref, wg_ref, ba_ref, bg_ref, u_ref):
    h = _rms_norm(x_ref[...], g_ref[...]).astype(BF16)
    a = _dot(h, wa_ref[...]) + ba_ref[...]
    gate = _dot(h, wg_ref[...]) + bg_ref[...]
    u_ref[...] = a * jax.nn.sigmoid(gate)


def _conv_in(x, gain, w_in, b_in, *, tm, tn, casts):
    m, d = x.shape
    nj = d // tn
    return _fused_call(
        _conv_in_body,
        name="conv_in",
        grid=(nj, m // tm),
        j_outer=True,
        in_specs=[
            pl.BlockSpec((tm, d), lambda j, i: (i, 0)),
            pl.BlockSpec((1, d), lambda j, i: (0, 0)),
            pl.BlockSpec((None, d, tn), lambda j, i: (j, 0, 0)),
            pl.BlockSpec((None, d, tn), lambda j, i: (j + nj, 0, 0)),
            pl.BlockSpec((1, tn), lambda j, i: (0, j)),
            pl.BlockSpec((1, tn), lambda j, i: (0, j + nj)),
        ],
        inputs=(x, gain, w_in, w_in, b_in, b_in),
        out_spec=pl.BlockSpec((tm, tn), lambda j, i: (i, j)),
        out_shape=jax.ShapeDtypeStruct((m, d), F32),
        casts=casts,
    )


def _conv_out_kernel(u_ref, uprev_ref, wdw_ref, bdw_ref, lng_ref, lnb_ref, wout_ref, bout_ref, x_ref,
                     o_ref, ext_ref, conv_ref, *, tm):
    i = pl.program_id(0)
    d = u_ref.shape[1]
    first = CONV_HALO - (CONV_WIDTH - 1)

    for s in range(d // CONV_STRIP):
        cols = slice(s * CONV_STRIP, (s + 1) * CONV_STRIP)
        ext_ref[s, 0:CONV_HALO, :] = jnp.where(i == 0, 0.0, uprev_ref[:, cols])
        ext_ref[s, CONV_HALO:, :] = u_ref[:, cols]
        acc = jnp.broadcast_to(bdw_ref[:, cols], (tm, CONV_STRIP))
        for k in range(CONV_WIDTH):
            acc = acc + wdw_ref[k:k + 1, cols] * ext_ref[s, first + k:first + k + tm, :]
        conv_ref[:, cols] = acc

    conv = conv_ref[...]
    mu = jnp.mean(conv, axis=-1, keepdims=True)
    xc = conv - mu
    y = xc * lax.rsqrt(jnp.mean(xc * xc, axis=-1, keepdims=True) + EPS)
    y = y * lng_ref[...] + lnb_ref[...]
    v = (y * jax.nn.sigmoid(y)).astype(BF16)
    tn = wout_ref.shape[-1]
    for t in range(wout_ref.shape[0]):
        cols = slice(t * tn, (t + 1) * tn)
        o_ref[:, cols] = x_ref[:, cols] + _dot(v, wout_ref[t]) + bout_ref[:, cols]


def _conv_out(u, x, w_dw, b_dw, ln_g, ln_b, w_out, b_out, *, tm):
    m, d = u.shape
    hb = tm // CONV_HALO
    const = lambda i: (0, 0)
    return pl.pallas_call(
        functools.partial(_conv_out_kernel, tm=tm),
        grid=(m // tm,),
        in_specs=[
            pl.BlockSpec((tm, d), lambda i: (i, 0)),
            pl.BlockSpec((CONV_HALO, d), lambda i: (jnp.maximum(i * hb - 1, 0), 0)),
            pl.BlockSpec((CONV_WIDTH, d), const),
            pl.BlockSpec((1, d), const),
            pl.BlockSpec((1, d), const),
            pl.BlockSpec((1, d), const),
            pl.BlockSpec(w_out.shape, lambda i: (0, 0, 0), pipeline_mode=pl.Buffered(1)),
            pl.BlockSpec((1, d), const),
            pl.BlockSpec((tm, d), lambda i: (i, 0)),
        ],
        out_specs=pl.BlockSpec((tm, d), lambda i: (i, 0)),
        out_shape=jax.ShapeDtypeStruct((m, d), F32),
        scratch_shapes=[
            pltpu.VMEM((d // CONV_STRIP, tm + CONV_HALO, CONV_STRIP), F32),
            pltpu.VMEM((tm, d), F32),
        ],
        compiler_params=_params("parallel"),
        name="conv_out",
    )(u, u, w_dw, b_dw, ln_g, ln_b, w_out, b_out, x)


def _pool_kernel(x_ref, xprev_ref, g_ref, w_ref, scale_ref, o_ref, ext_ref, *, tm):
    i = pl.program_id(0)
    d = x_ref.shape[1]
    gc = d // len(POOL_WINDOWS)
    x = x_ref[...]
    h = _rms_norm(x, g_ref[...])
    hprev = jnp.where(i == 0, 0.0, _rms_norm(xprev_ref[...], g_ref[...]))
    t = i * tm + lax.broadcasted_iota(jnp.int32, (tm, 1), 0)
    strips = gc // V7X_LANES
    for g, w in enumerate(POOL_WINDOWS):
        cnt = jnp.minimum(t + 1, w).astype(F32)
        mixes = []
        for s in range(g * strips, (g + 1) * strips):
            lanes = slice(s * V7X_LANES, (s + 1) * V7X_LANES)
            ext_ref[s, 0:POOL_HALO, :] = hprev[:, lanes]
            ext_ref[s, POOL_HALO:, :] = h[:, lanes]
            total = h[:, lanes]
            for k in range(1, w):
                total = total + ext_ref[s, POOL_HALO - k:POOL_HALO - k + tm, :]
            mixes.append(total / cnt - h[:, lanes])
        cols = slice(g * gc, (g + 1) * gc)
        mix = jnp.concatenate(mixes, axis=1).astype(BF16)
        y = _dot(mix, w_ref[g]) * scale_ref[:, cols]
        o_ref[:, cols] = x[:, cols] + y


def _pool(x, gain, w_grp, scale, *, tm):
    m, d = x.shape
    ng, gc, _ = w_grp.shape
    hb = tm // POOL_HALO
    kernel = functools.partial(_pool_kernel, tm=tm)
    return pl.pallas_call(
        kernel,
        grid=(m // tm,),
        in_specs=[
            pl.BlockSpec((tm, d), lambda i: (i, 0)),
            pl.BlockSpec((POOL_HALO, d), lambda i: (jnp.maximum(i * hb - 1, 0), 0)),
            pl.BlockSpec((1, d), lambda i: (0, 0)),
            pl.BlockSpec((ng, gc, gc), lambda i: (0, 0, 0)),
            pl.BlockSpec((1, d), lambda i: (0, 0)),
        ],
        out_specs=pl.BlockSpec((tm, d), lambda i: (i, 0)),
        out_shape=jax.ShapeDtypeStruct((m, d), F32),
        scratch_shapes=[pltpu.VMEM((d // V7X_LANES, tm + POOL_HALO, V7X_LANES), F32)],
        compiler_params=_params("parallel"),
        name="pool",
    )(x, x, gain, w_grp, scale)


def _norm_matmul_body(x_ref, g_ref, w_ref, o_ref):
    o_ref[...] = _dot(_rms_norm(x_ref[...], g_ref[...]).astype(BF16), w_ref[...])


def _norm_matmul(x, gain, w, *, tm, tn, casts):
    m, d = x.shape
    n = w.shape[0] * tn
    return _fused_call(
        _norm_matmul_body,
        name="qkv_proj",
        grid=(n // tn, m // tm),
        j_outer=True,
        in_specs=[
            pl.BlockSpec((tm, d), lambda j, i: (i, 0)),
            pl.BlockSpec((1, d), lambda j, i: (0, 0)),
            pl.BlockSpec((None, d, tn), lambda j, i: (j, 0, 0)),
        ],
        inputs=(x, gain, w),
        out_spec=pl.BlockSpec((tm, tn), lambda j, i: (i, j)),
        out_shape=jax.ShapeDtypeStruct((m, n), F32),
        casts=casts,
    )


def _t5_bucket(rel):
    nb = NUM_BUCKETS // 2
    n = -rel
    ret = jnp.where(n < 0, nb, 0)
    n = jnp.abs(n)
    max_exact = nb // 2
    nf = jnp.maximum(n, 1).astype(F32)
    large = max_exact + (jnp.log(nf / max_exact) / math.log(REL_MAX_DIST / max_exact)
                         * (nb - max_exact)).astype(jnp.int32)
    large = jnp.minimum(large, nb - 1)
    return ret + jnp.where(n < max_exact, n, large)


def _band_bias_body(bucket_ref, rel_bias_ref, o_ref):
    heads = o_ref.shape[0]
    bucket = bucket_ref[...]
    for k in range(heads):
        h = pl.program_id(0) * heads + k
        acc = jnp.zeros(bucket.shape, F32)
        for b in range(NUM_BUCKETS):
            acc = jnp.where(bucket == b, rel_bias_ref[b, h], acc)
        o_ref[k] = acc


def _band_bias(rel_bias, *, casts):
    n_heads = rel_bias.shape[1]
    heads = _tile(n_heads, BIAS_HEADS, 1)
    i = jnp.arange(QBLOCK)[:, None]
    j = jnp.arange(2 * QBLOCK)[None, :]
    bucket = _t5_bucket(j - QBLOCK - i).astype(jnp.int32).T
    return _fused_call(
        _band_bias_body,
        name="band_bias",
        grid=(n_heads // heads, 1),
        j_outer=False,
        in_specs=[
            pl.BlockSpec((2 * QBLOCK, QBLOCK), lambda h, j: (0, 0)),
            pl.BlockSpec(memory_space=pltpu.SMEM),
        ],
        inputs=(bucket, rel_bias),
        out_spec=pl.BlockSpec((heads, 2 * QBLOCK, QBLOCK), lambda h, j: (h, 0, 0)),
        out_shape=jax.ShapeDtypeStruct((n_heads, 2 * QBLOCK, QBLOCK), F32),
        casts=casts,
    )


def _attn_kernel(sinks_ref, q_ref, kc_ref, kp_ref, vc_ref, vp_ref, bias_ref, qg_ref, kg_ref, o_ref,
                 *, n_kv, group):
    n = pl.program_id(0)
    pair = 2 * HEAD_DIM
    n_heads = n_kv * group
    kj = lax.broadcasted_iota(jnp.int32, (2 * QBLOCK, QBLOCK), 0)
    qi = lax.broadcasted_iota(jnp.int32, (2 * QBLOCK, QBLOCK), 1)
    qc = qi // CHUNK
    kc = kj // CHUNK - QBLOCK // CHUNK
    visible = (kc <= qc) & (kc >= qc - WINDOW_CHUNKS) & ((n > 0) | (kj >= QBLOCK))
    scale = HEAD_DIM ** -0.5

    q_heads = []
    zeros = jnp.zeros((HEAD_DIM, QBLOCK), BF16)
    for c in range(n_heads // 2):
        qt = q_ref[:, c * pair:(c + 1) * pair].T
        for qhalf in range(2):
            qh = qt[qhalf * HEAD_DIM:(qhalf + 1) * HEAD_DIM]
            ms = jnp.mean(qh * qh, axis=0, keepdims=True)
            q_heads.append((qh * lax.rsqrt(ms + EPS) * qg_ref[...] * scale).astype(BF16))
    k_tiles = []
    for c in range(n_kv // 2):
        blocks = []
        for ref in (kp_ref, kc_ref):
            kt = ref[:, c * pair:(c + 1) * pair].T
            halves = []
            for half in range(2):
                kh = kt[half * HEAD_DIM:(half + 1) * HEAD_DIM]
                ms = jnp.mean(kh * kh, axis=0, keepdims=True)
                halves.append(kh * lax.rsqrt(ms + EPS) * kg_ref[...])
            blocks.append(jnp.concatenate(halves, axis=0).T)
        k_tiles.append(jnp.concatenate(blocks, axis=0).astype(BF16))

    logits, values = [], []
    heads_per_tile = 2 * group
    for hp in range(n_kv // 2):
        kcols = slice(hp * pair, (hp + 1) * pair)
        values.append(jnp.concatenate([vp_ref[:, kcols].T, vc_ref[:, kcols].T], axis=1).astype(BF16))
        stacked = []
        for head in range(hp * heads_per_tile, (hp + 1) * heads_per_tile):
            rows = [q_heads[head], zeros] if (head // group) % 2 == 0 else [zeros, q_heads[head]]
            stacked.append(jnp.concatenate(rows, axis=0))
        k2 = k_tiles[hp]
        st = _dot(k2, jnp.concatenate(stacked, axis=1))
        for g in range(heads_per_tile):
            cols = st[:, g * QBLOCK:(g + 1) * QBLOCK]
            logits.append(jnp.where(visible, cols + bias_ref[hp * heads_per_tile + g], NEG_INF))

    s = jnp.concatenate(logits, axis=1)
    sink = jnp.concatenate([jnp.full((1, QBLOCK), sinks_ref[h], F32) for h in range(n_heads)], axis=1)
    mx = jnp.maximum(jnp.max(s, axis=0, keepdims=True), sink)
    e = jnp.exp(s - mx)
    denom = jnp.sum(e, axis=0, keepdims=True) + jnp.exp(sink - mx)
    r = 1.0 / denom
    e = e.astype(BF16)

    for hp in range(n_kv // 2):
        cols = slice(hp * heads_per_tile * QBLOCK, (hp + 1) * heads_per_tile * QBLOCK)
        ot = _dot(values[hp], e[:, cols]) * r[:, cols]
        for c in range(group):
            parts = []
            for qhalf in range(2):
                g = 2 * c + qhalf
                half = (g // group) % 2
                parts.append(ot[half * HEAD_DIM:(half + 1) * HEAD_DIM, g * QBLOCK:(g + 1) * QBLOCK])
            col = (hp * group + c) * pair
            o_ref[:, col:col + pair] = jnp.concatenate(parts, axis=0).T.astype(BF16)


def _attention(qkv, bias, sinks, q_gain, k_gain, *, n_heads, n_kv, casts):
    m = qkv.shape[0]
    group = n_heads // n_kv
    dq = n_heads * HEAD_DIM
    dkv = n_kv * HEAD_DIM
    kblk = dq // dkv
    prev = lambda n: jnp.maximum(n - 1, 0)
    return _fused_call(
        functools.partial(_attn_kernel, n_kv=n_kv, group=group),
        name="band_attention",
        grid=(m // QBLOCK, 1),
        j_outer=False,
        in_specs=[
            pl.BlockSpec(memory_space=pltpu.SMEM),
            pl.BlockSpec((QBLOCK, dq), lambda n, j: (n, 0)),
            pl.BlockSpec((QBLOCK, dkv), lambda n, j: (n, kblk)),
            pl.BlockSpec((QBLOCK, dkv), lambda n, j: (prev(n), kblk)),
            pl.BlockSpec((QBLOCK, dkv), lambda n, j: (n, kblk + 1)),
            pl.BlockSpec((QBLOCK, dkv), lambda n, j: (prev(n), kblk + 1)),
            pl.BlockSpec((n_heads, 2 * QBLOCK, QBLOCK), lambda n, j: (0, 0, 0)),
            pl.BlockSpec((HEAD_DIM, QBLOCK), lambda n, j: (0, 0)),
            pl.BlockSpec((HEAD_DIM, QBLOCK), lambda n, j: (0, 0)),
        ],
        inputs=(sinks, qkv, qkv, qkv, qkv, qkv, bias, q_gain, k_gain),
        out_spec=pl.BlockSpec((QBLOCK, dq), lambda n, j: (n, 0)),
        out_shape=jax.ShapeDtypeStruct((m, dq), BF16),
        casts=casts,
    )


def _matmul_residual_body(a_ref, w_ref, x_ref, o_ref):
    o_ref[...] = x_ref[...] + _dot(a_ref[...], w_ref[...])


def _matmul_residual(a, w, x, *, tm, tn, casts):
    m, k = a.shape
    n = w.shape[0] * tn
    return _fused_call(
        _matmul_residual_body,
        name="attn_out_proj",
        grid=(n // tn, m // tm),
        j_outer=True,
        in_specs=[
            pl.BlockSpec((tm, k), lambda j, i: (i, 0)),
            pl.BlockSpec((None, k, tn), lambda j, i: (j, 0, 0)),
            pl.BlockSpec((tm, tn), lambda j, i: (i, j)),
        ],
        inputs=(a, w, x),
        out_spec=pl.BlockSpec((tm, tn), lambda j, i: (i, j)),
        out_shape=jax.ShapeDtypeStruct((m, n), F32),
        casts=casts,
    )


def _ffn_body(x_ref, g_ref, wg_ref, wu_ref, wd_ref, p_ref, pg_ref, wpg_ref, bpg_ref, wpp_ref, o_ref, h_ref):
    j = pl.program_id(1)

    @pl.when(j == 0)
    def _():
        x = x_ref[...]
        h_ref[...] = _rms_norm(x, g_ref[...]).astype(BF16)
        o_ref[...] = x

    h = h_ref[...]
    gate = _dot(h, wg_ref[...])
    up = _dot(h, wu_ref[...])
    act = (gate * jax.nn.sigmoid(gate) * up).astype(BF16)
    o_ref[...] += _dot(act, wd_ref[...])

    @pl.when(j == pl.num_programs(1) - 1)
    def _():
        x1 = o_ref[...]
        h1 = _rms_norm(x1, pg_ref[...]).astype(BF16)
        pb = p_ref[...].astype(BF16)
        tn = wpg_ref.shape[-1]
        for t in range(wpg_ref.shape[0]):
            cols = slice(t * tn, (t + 1) * tn)
            pgate = jax.nn.sigmoid(_dot(h1, wpg_ref[t]) + bpg_ref[:, cols])
            o_ref[:, cols] = x1[:, cols] + pgate * _dot(pb, wpp_ref[t])


def _ffn(x, gain, w_gate, w_up, w_down, p, layer, ple_gain, w_pgate, b_pgate, w_pproj, *, tm, tf, casts):
    m, d = x.shape
    f = w_gate.shape[0] * tf
    pd = p.shape[-1]
    resident = lambda a: pl.BlockSpec(a.shape, lambda i, j: (0,) * a.ndim, pipeline_mode=pl.Buffered(1))
    return _fused_call(
        _ffn_body,
        name="swiglu_ffn_gated_embedding",
        grid=(m // tm, f // tf),
        j_outer=False,
        in_specs=[
            pl.BlockSpec((tm, d), lambda i, j: (i, 0)),
            pl.BlockSpec((1, d), lambda i, j: (0, 0)),
            pl.BlockSpec((None, d, tf), lambda i, j: (j, 0, 0)),
            pl.BlockSpec((None, d, tf), lambda i, j: (j, 0, 0)),
            pl.BlockSpec((None, tf, d), lambda i, j: (0, j, 0)),
            pl.BlockSpec((None, tm, pd), lambda i, j: (layer, i, 0)),
            pl.BlockSpec((1, d), lambda i, j: (0, 0)),
            resident(w_pgate),
            pl.BlockSpec((1, d), lambda i, j: (0, 0)),
            resident(w_pproj),
        ],
        inputs=(x, gain, w_gate, w_up, w_down, p, ple_gain, w_pgate, b_pgate, w_pproj),
        out_spec=pl.BlockSpec((tm, d), lambda i, j: (i, 0)),
        out_shape=jax.ShapeDtypeStruct((m, d), F32),
        scratch=[pltpu.VMEM((tm, d), BF16)],
        casts=casts,
    )


def kernel(x, p, norm_mix, norm_ffn, norm_ple, conv_w_in, conv_b_in, conv_w_dw, conv_b_dw, conv_ln_g,
           conv_ln_b, conv_w_out, conv_b_out, pool_w, pool_scale, attn_w_qkv, attn_q_norm, attn_k_norm,
           attn_sinks, attn_w_o, rel_bias, ffn_w_gate, ffn_w_up, ffn_w_down, ple_w_proj, ple_w_gate,
           ple_b_gate):
    batch, seq, d = x.shape
    depth = norm_mix.shape[0]
    n_heads = d // HEAD_DIM
    n_kv = (attn_w_qkv.shape[-1] // HEAD_DIM - n_heads) // 2
    group = n_heads // n_kv
    assert n_kv % 2 == 0 and group % 2 == 0, "attention kernel pairs heads into 128-lane tiles"
    assert d % (len(POOL_WINDOWS) * V7X_LANES) == 0
    m = batch * seq
    assert batch == 1 and m % QBLOCK == 0, "row tiles assume one sequence"

    tm = _tile(m, ROW_TILE)
    tm2 = _tile(m, WIDE_ROW_TILE)
    tn = _tile(d, COL_TILE)
    tf = _tile(ffn_w_gate.shape[-1], FF_TILE)
    tq = _tile(attn_w_qkv.shape[-1], COL_TILE)
    row = lambda v: v.reshape(1, -1)

    pool_w2 = pool_w.reshape(pool_w.shape[0], -1, pool_w.shape[-1])

    def mixer_weights(i):
        kind, l = i % N_MIXERS, i // N_MIXERS
        if kind == 0:
            return {("conv_in", i): (conv_w_in, l, tn), ("conv_out", i): (conv_w_out, l, tn)}
        if kind == 1:
            return {("pool", i): (pool_w2, l, pool_w2.shape[-1])}
        return {("qkv", i): (attn_w_qkv, l, tq), ("attn_o", i): (attn_w_o, l, tn)}

    def ffn_weights(i):
        return {("gate", i): (ffn_w_gate, i, tf), ("up", i): (ffn_w_up, i, tf), ("down", i): (ffn_w_down, i, d)}

    def ple_weights(i):
        return {("ple_gate", i): (ple_w_gate, i, tn), ("ple_proj", i): (ple_w_proj, i, tn)}

    ready = {}

    def bf16(key, source):
        return ready.pop(key) if key in ready else _cast(*source)

    def hosting(weights, call):
        out, copies = call(list(weights.values()))
        ready.update(zip(weights.keys(), copies))
        return out

    xs = x.reshape(m, d)
    bias = None
    if depth >= N_MIXERS:
        bias = hosting(mixer_weights(0), lambda c: _band_bias(rel_bias, casts=c))
    for i in range(depth):
        kind, l = i % N_MIXERS, i // N_MIXERS
        gain = row(norm_mix[i])
        mine = {} if ("gate", i) in ready else {**ffn_weights(i), **ple_weights(i)}
        if kind == 0:
            w_in, w_out = (bf16(k, w) for k, w in mixer_weights(i).items())
            u = hosting(mine, lambda c: _conv_in(xs, gain, w_in, row(conv_b_in[l]), tm=tm, tn=tn, casts=c))
            xs = _conv_out(u, xs, conv_w_dw[l], row(conv_b_dw[l]), row(conv_ln_g[l]), row(conv_ln_b[l]),
                           w_out, row(conv_b_out[l]), tm=tm)
        elif kind == 1:
            (w_grp,) = (bf16(k, w) for k, w in mixer_weights(i).items())
            w_grp = w_grp.reshape(pool_w.shape[1:])
            xs = _pool(xs, gain, w_grp, row(pool_scale[l]), tm=tm)
        else:
            w_qkv, w_o = (bf16(k, w) for k, w in mixer_weights(i).items())
            qkv, _ = _norm_matmul(xs, gain, w_qkv, tm=tm2, tn=tq, casts=[])
            o = hosting({**mine, **(mixer_weights(i + 1) if i + 1 < depth else {})}, lambda c: _attention(
                qkv, bias, attn_sinks[l], jnp.broadcast_to(attn_q_norm[l][:, None], (HEAD_DIM, QBLOCK)),
                jnp.broadcast_to(attn_k_norm[l][:, None], (HEAD_DIM, QBLOCK)),
                n_heads=n_heads, n_kv=n_kv, casts=c))
            xs, _ = _matmul_residual(o, w_o, xs, tm=tm2, tn=tn, casts=[])
        w_gate, w_up, w_down = (bf16(k, w) for k, w in ffn_weights(i).items())
        w_pg, w_pp = (bf16(k, w) for k, w in ple_weights(i).items())
        ahead = {}
        if i + 1 < depth:
            ahead = mixer_weights(i + 1)
            if (i + 1) % N_MIXERS == 1:
                ahead = {**ahead, **ffn_weights(i + 1), **ple_weights(i + 1)}
            ahead = {k: v for k, v in ahead.items() if k not in ready}
        xs = hosting(ahead, lambda c: _ffn(
            xs, row(norm_ffn[i]), w_gate, w_up, w_down, p.reshape(depth, m, -1), i, row(norm_ple[i]),
            w_pg, row(ple_b_gate[i]), w_pp, tm=tm, tf=tf, casts=c))
    return xs.reshape(batch, seq, d)
```
